```python
import jax, jax.numpy as jnp
from jax import lax
import numpy as np

D_MODEL = 1024
BATCH = 8
SEQ = 2048
DEPTH = 2
DEC_BATCH = 128
DEC_SEQ = 1
PAST_LEN = 16384
PAGE_SIZE = 128

N_EVEN = (DEPTH + 1) // 2
N_ODD = DEPTH // 2
D_POOL = D_MODEL // 2
POOL_WINDOWS = (2, 4, 8, 16)
N_POOL_GROUPS = len(POOL_WINDOWS)
POOL_GROUP = D_POOL // N_POOL_GROUPS
POOL_HIST = max(POOL_WINDOWS) - 1
D_CONV = D_MODEL // 2
CONV_WIDTH = 3
CONV_HIST = CONV_WIDTH - 1
D_IN_EVEN = D_POOL + 3 * D_CONV
D_MIX_EVEN = D_POOL + D_CONV
D_GATE = D_MODEL
CHUNK = 128
N_SG_HEADS = 8
SG_HEAD = D_GATE // N_SG_HEADS
D_FF = -(-8 * D_MODEL // (3 * 256)) * 256
EPS = 1e-6

kernel_name = "hybrid_pool_shortconv_chunkgmlp_decode_step"


def rmsnorm(x, g):
    xf = x.astype(jnp.float32)
    y = xf * lax.rsqrt(jnp.mean(xf * xf, axis=-1, keepdims=True) + EPS)
    return (y * g.astype(jnp.float32)).astype(x.dtype)


def swiglu(h, w_gate, w_up, w_down):
    return (jax.nn.silu(h @ w_gate) * (h @ w_up)) @ w_down


def pool_mixer(hist, p, start, w_pool, scale):
    B, T, _ = p.shape
    full = jnp.concatenate([hist, p], axis=1)
    cs = jnp.cumsum(full.astype(jnp.float32), axis=1)
    cs = jnp.pad(cs, ((0, 0), (1, 0), (0, 0)))
    pos = start + jnp.arange(T)
    outs = []
    for g, w in enumerate(POOL_WINDOWS):
        sl = slice(g * POOL_GROUP, (g + 1) * POOL_GROUP)
        hi = cs[:, POOL_HIST + 1:POOL_HIST + 1 + T, sl]
        lo = cs[:, POOL_HIST + 1 - w:POOL_HIST + 1 - w + T, sl]
        cnt = jnp.minimum(pos + 1, w).astype(jnp.float32)[None, :, None]
        outs.append((hi - lo) / cnt)
    pooled = jnp.stack(outs, axis=2)
    pg = p.reshape(B, T, N_POOL_GROUPS, POOL_GROUP).astype(jnp.float32)
    d = (pooled - pg).astype(p.dtype)
    mixed = jnp.einsum('btgc,gcd->btgd', d, w_pool).reshape(B, T, D_POOL)
    return mixed * scale, full[:, -POOL_HIST:]


def conv_mixer(hist, xb, b_gate, c_gate, conv_w):
    T = xb.shape[1]
    full = jnp.concatenate([hist, c_gate * xb], axis=1)
    y = full[:, 0:T] * conv_w[0]
    for k in range(1, CONV_WIDTH):
        y = y + full[:, k:k + T] * conv_w[k]
    return b_gate * y, full[:, -CONV_HIST:]


def chunk_gating(z, g_v, w_s, b_s):
    B, T, _ = z.shape
    u, v = z[..., :D_GATE], z[..., D_GATE:]
    v = rmsnorm(v, g_v)
    Tp = -(-T // CHUNK) * CHUNK
    vp = jnp.pad(v, ((0, 0), (0, Tp - T), (0, 0))).reshape(B, Tp // CHUNK, CHUNK, N_SG_HEADS, SG_HEAD)
    mask = jnp.tril(jnp.ones((CHUNK, CHUNK), dtype=bool))
    ws = jnp.where(mask[None], w_s, jnp.zeros((), w_s.dtype))
    mixed = jnp.einsum('hts,bnshd->bnthd', ws, vp) + b_s.T[None, None, :, :, None]
    mixed = mixed.reshape(B, Tp, D_GATE)[:, :T]
    return u * mixed, v


def run_group(x, start, pool_hist, conv_hist, norm_mix, norm_ffn, norm_final,
              w_in_even, w_pool, pool_scale, conv_w, w_out_even,
              w_in_odd, norm_sg, w_s, b_s, w_out_odd,
              ffn_w_gate, ffn_w_up, ffn_w_down):
    pool_new, conv_new, v_new = [], [], []
    for layer in range(DEPTH):
        h = rmsnorm(x, norm_mix[layer])
        if layer % 2 == 0:
            e = layer // 2
            z = h @ w_in_even[e]
            p, xb, bg, cg = jnp.split(z, [D_POOL, D_POOL + D_CONV, D_POOL + 2 * D_CONV], axis=-1)
            a_out, ph = pool_mixer(pool_hist[e], p, start, w_pool[e], pool_scale[e])
            b_out, ch = conv_mixer(conv_hist[e], xb, bg, cg, conv_w[e])
            mix = jnp.concatenate([a_out, b_out], axis=-1) @ w_out_even[e]
            pool_new.append(ph)
            conv_new.append(ch)
        else:
            o = layer // 2
            z = jax.nn.gelu(h @ w_in_odd[o], approximate=False)
            c_out, v = chunk_gating(z, norm_sg[o], w_s[o], b_s[o])
            mix = c_out @ w_out_odd[o]
            v_new.append(v)
        x = x + mix
        h = rmsnorm(x, norm_ffn[layer])
        x = x + swiglu(h, ffn_w_gate[layer], ffn_w_up[layer], ffn_w_down[layer])
    return rmsnorm(x, norm_final), jnp.stack(pool_new), jnp.stack(conv_new), jnp.stack(v_new)


def setup_inputs(seed: int = 0) -> dict:
    key = jax.random.key(seed)
    ks = jax.random.split(key, 24)
    nrm = lambda k, shape, s: jax.random.normal(k, shape, jnp.float32) * s
    return {
        "x_prompt": nrm(ks[0], (BATCH, SEQ, D_MODEL), 1.0),
        "x_sample": nrm(ks[1], (DEC_BATCH, DEC_SEQ, D_MODEL), 1.0),
        "state_pool": nrm(ks[2], (N_EVEN, DEC_BATCH, POOL_HIST, D_POOL), 1.0),
        "state_conv": nrm(ks[3], (N_EVEN, DEC_BATCH, CONV_HIST, D_CONV), 1.0),
        "norm_mix": 1.0 + nrm(ks[4], (DEPTH, D_MODEL), 0.05),
        "norm_ffn": 1.0 + nrm(ks[5], (DEPTH, D_MODEL), 0.05),
        "norm_final": 1.0 + nrm(ks[6], (D_MODEL,), 0.05),
        "w_in_even": nrm(ks[7], (N_EVEN, D_MODEL, D_IN_EVEN), D_MODEL ** -0.5),
        "w_pool": nrm(ks[8], (N_EVEN, N_POOL_GROUPS, POOL_GROUP, POOL_GROUP), POOL_GROUP ** -0.5),
        "pool_scale": 1.0 + nrm(ks[9], (N_EVEN, D_POOL), 0.1),
        "conv_w": nrm(ks[10], (N_EVEN, CONV_WIDTH, D_CONV), CONV_WIDTH ** -0.5),
        "w_out_even": nrm(ks[11], (N_EVEN, D_MIX_EVEN, D_MODEL), D_MIX_EVEN ** -0.5),
        "w_in_odd": nrm(ks[12], (N_ODD, D_MODEL, 2 * D_GATE), D_MODEL ** -0.5),
        "norm_sg": 1.0 + nrm(ks[13], (N_ODD, D_GATE), 0.05),
        "w_s": nrm(ks[14], (N_ODD, N_SG_HEADS, CHUNK, CHUNK), CHUNK ** -0.5),
        "b_s": 1.0 + nrm(ks[15], (N_ODD, N_SG_HEADS, CHUNK), 0.1),
        "w_out_odd": nrm(ks[16], (N_ODD, D_GATE, D_MODEL), D_GATE ** -0.5),
        "ffn_w_gate": nrm(ks[17], (DEPTH, D_MODEL, D_FF), D_MODEL ** -0.5),
        "ffn_w_up": nrm(ks[18], (DEPTH, D_MODEL, D_FF), D_MODEL ** -0.5),
        "ffn_w_down": nrm(ks[19], (DEPTH, D_FF, D_MODEL), D_FF ** -0.5),
    }


def reference(x_prompt, x_sample, state_pool, state_conv, norm_mix, norm_ffn, norm_final,
              w_in_even, w_pool, pool_scale, conv_w, w_out_even,
              w_in_odd, norm_sg, w_s, b_s, w_out_odd,
              ffn_w_gate, ffn_w_up, ffn_w_down):
    weights = (norm_mix, norm_ffn, norm_final, w_in_even, w_pool, pool_scale, conv_w, w_out_even,
               w_in_odd, norm_sg, w_s, b_s, w_out_odd, ffn_w_gate, ffn_w_up, ffn_w_down)
    pool0 = jnp.zeros((N_EVEN, BATCH, POOL_HIST, D_POOL), x_prompt.dtype)
    conv0 = jnp.zeros((N_EVEN, BATCH, CONV_HIST, D_CONV), x_prompt.dtype)
    y_prompt, pool_p, conv_p, _v_p = run_group(x_prompt, 0, pool0, conv0, *weights)
    y_sample, pool_s, conv_s, v_s = run_group(x_sample, PAST_LEN, state_pool, state_conv, *weights)
    return (y_prompt, y_sample, pool_p, pool_s, conv_p, conv_s, v_s)
```

```python
import functools
import math

import jax
import jax.numpy as jnp
from jax import lax
from jax.experimental import pallas as pl
from jax.experimental.pallas import tpu as pltpu

POOL_WINDOWS = (2, 4, 8, 16)
POOL_GROUP = 128
POOL_HIST = max(POOL_WINDOWS) - 1
CONV_WIDTH = 3
CONV_HIST = CONV_WIDTH - 1
CHUNK = 128
N_SG_HEADS = 8
EPS = 1e-6
PAST_LEN = 16384

V7X_SUBLANES = 8
V7X_MXU_DIM = 256
V7X_VMEM_LIMIT_BYTES = 56 * 1024 * 1024

POOL_PAD = -(-POOL_HIST // V7X_SUBLANES) * V7X_SUBLANES
CONV_PAD = -(-CONV_HIST // V7X_SUBLANES) * V7X_SUBLANES

PROMPT_TILE = 512


def _bf16_dot(a, b):
  return jnp.dot(a.astype(jnp.bfloat16), b, preferred_element_type=jnp.float32)


def _rmsnorm(x, g):
  y = x * lax.rsqrt(jnp.mean(x * x, axis=-1, keepdims=True) + EPS)
  return y * g


def _gelu_exact(x):
  return 0.5 * x * (1.0 + lax.erf(x * math.sqrt(0.5)))


def _pool_project(d, w_bd_ref, scale):
  halves = []
  for i in range(w_bd_ref.shape[0]):
    lo = i * V7X_MXU_DIM
    halves.append(_bf16_dot(d[:, lo:lo + V7X_MXU_DIM], w_bd_ref[i]))
  return jnp.concatenate(halves, axis=-1) * scale


def _ffn(x, g_ffn, wg_ref, wu_ref, wd_ref):
  h = _rmsnorm(x, g_ffn).astype(jnp.bfloat16)
  gate = jnp.dot(h, wg_ref[...], preferred_element_type=jnp.float32)
  up = jnp.dot(h, wu_ref[...], preferred_element_type=jnp.float32)
  act = gate * jax.nn.sigmoid(gate) * up
  return x + _bf16_dot(act, wd_ref[...])


def _even_prompt_kernel(x_ref, g_mix_ref, w_in_ref, w_pool_ref, pool_scale_ref,
                        conv_w_ref, w_out_ref, g_ffn_ref, wg_ref, wu_ref,
                        wd_ref, y_ref, pool_state_ref, conv_state_ref,
                        p_ext, c_ext, *, tile, d_pool, d_conv):
  t = pl.program_id(1)

  @pl.when(t == 0)
  def _():
    p_ext[0:POOL_PAD, :] = jnp.zeros((POOL_PAD, d_pool), jnp.float32)
    c_ext[0:CONV_PAD, :] = jnp.zeros((CONV_PAD, d_conv), jnp.float32)

  x = x_ref[...]
  h = _rmsnorm(x, g_mix_ref[...])
  z = _bf16_dot(h, w_in_ref[...])
  p = z[:, :d_pool]
  xb = z[:, d_pool:d_pool + d_conv]
  bg = z[:, d_pool + d_conv:d_pool + 2 * d_conv]
  cg = z[:, d_pool + 2 * d_conv:]

  p_ext[POOL_PAD:POOL_PAD + tile, :] = p
  pos = t * tile + lax.broadcasted_iota(jnp.int32, (tile, POOL_GROUP), 0)
  d_groups = []
  for g, w in enumerate(POOL_WINDOWS):
    lo = g * POOL_GROUP
    acc = p[:, lo:lo + POOL_GROUP]
    for k in range(1, w):
      acc = acc + p_ext[pl.ds(POOL_PAD - k, tile), lo:lo + POOL_GROUP]
    cnt = jnp.minimum(pos + 1, w).astype(jnp.float32)
    d_groups.append(acc / cnt - p[:, lo:lo + POOL_GROUP])
  a_out = _pool_project(jnp.concatenate(d_groups, axis=-1), w_pool_ref,
                        pool_scale_ref[...])

  c = cg * xb
  c_ext[CONV_PAD:CONV_PAD + tile, :] = c
  y = c_ext[pl.ds(CONV_PAD - 2, tile), :] * conv_w_ref[0:1, :]
  y = y + c_ext[pl.ds(CONV_PAD - 1, tile), :] * conv_w_ref[1:2, :]
  y = y + c * conv_w_ref[2:3, :]
  b_out = bg * y

  mix = _bf16_dot(jnp.concatenate([a_out, b_out], axis=-1), w_out_ref[...])
  y_ref[...] = _ffn(x + mix, g_ffn_ref[...], wg_ref, wu_ref, wd_ref)

  @pl.when(t == pl.num_programs(1) - 1)
  def _():
    pool_state_ref[...] = p_ext[pl.ds(POOL_PAD + tile - POOL_HIST, POOL_HIST), :]
    conv_state_ref[...] = c_ext[pl.ds(CONV_PAD + tile - CONV_HIST, CONV_HIST), :]

  p_ext[0:POOL_PAD, :] = p_ext[tile:tile + POOL_PAD, :]
  c_ext[0:CONV_PAD, :] = c_ext[tile:tile + CONV_PAD, :]


def _even_sample_kernel(x_ref, pool_hist_ref, conv_hist_ref, g_mix_ref,
                        w_in_ref, w_pool_ref, pool_scale_ref, conv_w_ref,
                        w_out_ref, g_ffn_ref, wg_ref, wu_ref, wd_ref,
                        y_ref, pool_state_ref, conv_state_ref,
                        *, start, d_pool, d_conv):
  x = x_ref[...]
  h = _rmsnorm(x, g_mix_ref[...])
  z = _bf16_dot(h, w_in_ref[...])
  p = z[:, :d_pool]
  xb = z[:, d_pool:d_pool + d_conv]
  bg = z[:, d_pool + d_conv:d_pool + 2 * d_conv]
  cg = z[:, d_pool + 2 * d_conv:]

  d_groups = []
  for g, w in enumerate(POOL_WINDOWS):
    lo = g * POOL_GROUP
    acc = p[:, lo:lo + POOL_GROUP]
    for k in range(1, w):
      acc = acc + pool_hist_ref[POOL_HIST - k, :, lo:lo + POOL_GROUP]
    cnt = float(min(start + 1, w))
    d_groups.append(acc / cnt - p[:, lo:lo + POOL_GROUP])
  a_out = _pool_project(jnp.concatenate(d_groups, axis=-1), w_pool_ref,
                        pool_scale_ref[...])

  c = cg * xb
  y = conv_hist_ref[0] * conv_w_ref[0:1, :]
  y = y + conv_hist_ref[1] * conv_w_ref[1:2, :]
  y = y + c * conv_w_ref[2:3, :]
  b_out = bg * y

  mix = _bf16_dot(jnp.concatenate([a_out, b_out], axis=-1), w_out_ref[...])
  y_ref[...] = _ffn(x + mix, g_ffn_ref[...], wg_ref, wu_ref, wd_ref)

  pool_state_ref[0:POOL_HIST - 1] = pool_hist_ref[1:POOL_HIST]
  pool_state_ref[POOL_HIST - 1] = p
  conv_state_ref[0:CONV_HIST - 1] = conv_hist_ref[1:CONV_HIST]
  conv_state_ref[CONV_HIST - 1] = c


def _odd_in_proj(x, g_mix, w_in_ref, g_v, d_gate):
  h = _rmsnorm(x, g_mix)
  z = _gelu_exact(_bf16_dot(h, w_in_ref[...]))
  return z[:, :d_gate], _rmsnorm(z[:, d_gate:], g_v)


def _odd_prompt_kernel(x_ref, g_mix_ref, w_in_ref, g_v_ref, w_s_ref, b_s_ref,
                       w_out_ref, g_ffn_ref, wg_ref, wu_ref, wd_ref,
                       g_final_ref, y_ref, *, tile, d_gate):
  x = x_ref[...]
  u, v = _odd_in_proj(x, g_mix_ref[...], w_in_ref, g_v_ref[...], d_gate)
  vb = v.astype(jnp.bfloat16)

  n_chunks = tile // CHUNK
  head = d_gate // N_SG_HEADS
  row = lax.broadcasted_iota(jnp.int32, (CHUNK, CHUNK), 0)
  col = lax.broadcasted_iota(jnp.int32, (CHUNK, CHUNK), 1)
  causal = row >= col
  per_head = []
  for hd in range(N_SG_HEADS):
    w_h = jnp.where(causal, w_s_ref[hd], 0.0).astype(jnp.bfloat16)
    rhs = jnp.concatenate(
        [vb[c * CHUNK:(c + 1) * CHUNK, hd * head:(hd + 1) * head]
         for c in range(n_chunks)], axis=1)
    o = jnp.dot(w_h, rhs, preferred_element_type=jnp.float32)
    per_head.append(o + b_s_ref[:, hd:hd + 1])
  mixed = jnp.concatenate(
      [jnp.concatenate([o[:, c * head:(c + 1) * head] for o in per_head], axis=1)
       for c in range(n_chunks)], axis=0)

  mix = _bf16_dot(u * mixed, w_out_ref[...])
  y = _ffn(x + mix, g_ffn_ref[...], wg_ref, wu_ref, wd_ref)
  y_ref[...] = _rmsnorm(y, g_final_ref[...])


def _odd_sample_kernel(x_ref, g_mix_ref, w_in_ref, g_v_ref, w_s_ref, b_s_ref,
                       w_out_ref, g_ffn_ref, wg_ref, wu_ref, wd_ref,
                       g_final_ref, y_ref, v_ref, *, d_gate):
  x = x_ref[...]
  u, v = _odd_in_proj(x, g_mix_ref[...], w_in_ref, g_v_ref[...], d_gate)
  v_ref[...] = v

  head = d_gate // N_SG_HEADS
  rows = x.shape[0]
  vb = v.astype(jnp.bfloat16).astype(jnp.float32)
  per_head = []
  for hd in range(N_SG_HEADS):
    w00 = w_s_ref[hd, 0:1, 0:1].astype(jnp.bfloat16).astype(jnp.float32)
    b0 = b_s_ref[0:1, hd:hd + 1]
    per_head.append(vb[:, hd * head:(hd + 1) * head] * jnp.broadcast_to(w00, (rows, head))
                    + jnp.broadcast_to(b0, (rows, head)))
  mixed = jnp.concatenate(per_head, axis=1)

  mix = _bf16_dot(u * mixed, w_out_ref[...])
  y = _ffn(x + mix, g_ffn_ref[...], wg_ref, wu_ref, wd_ref)
  y_ref[...] = _rmsnorm(y, g_final_ref[...])


def _resident(arr):
  zeros = (0,) * arr.ndim
  return pl.BlockSpec(arr.shape, lambda *_: zeros, pipeline_mode=pl.Buffered(1))


def _params(n_grid_dims):
  return pltpu.CompilerParams(
      dimension_semantics=("arbitrary",) * n_grid_dims,
      vmem_limit_bytes=V7X_VMEM_LIMIT_BYTES)


def _even_prompt(x, weights, tile):
  batch, seq, d_model = x.shape
  d_pool = weights[3].shape[-1]
  d_conv = weights[4].shape[-1]
  tok = pl.BlockSpec((None, tile, d_model), lambda b, t: (b, t, 0))
  state = lambda rows, ch: pl.BlockSpec((None, rows, ch), lambda b, t: (b, 0, 0))
  return pl.pallas_call(
      functools.partial(_even_prompt_kernel, tile=tile, d_pool=d_pool, d_conv=d_conv),
      grid=(batch, seq // tile),
      in_specs=[tok] + [_resident(w) for w in weights],
      out_specs=[tok, state(POOL_HIST, d_pool), state(CONV_HIST, d_conv)],
      out_shape=[jax.ShapeDtypeStruct(x.shape, x.dtype),
                 jax.ShapeDtypeStruct((batch, POOL_HIST, d_pool), x.dtype),
                 jax.ShapeDtypeStruct((batch, CONV_HIST, d_conv), x.dtype)],
      scratch_shapes=[pltpu.VMEM((POOL_PAD + tile, d_pool), jnp.float32),
                      pltpu.VMEM((CONV_PAD + tile, d_conv), jnp.float32)],
      compiler_params=_params(2),
      name="even_layer_prompt",
  )(x, *weights)


def _even_sample(x, pool_hist, conv_hist, weights, start):
  rows, d_model = x.shape
  d_pool = pool_hist.shape[-1]
  d_conv = conv_hist.shape[-1]
  operands = (x, pool_hist, conv_hist) + tuple(weights)
  return pl.pallas_call(
      functools.partial(_even_sample_kernel, start=start, d_pool=d_pool, d_conv=d_conv),
      grid=(1,),
      in_specs=[_resident(a) for a in operands],
      out_specs=[pl.BlockSpec(x.shape, lambda i: (0, 0)),
                 pl.BlockSpec(pool_hist.shape, lambda i: (0, 0, 0)),
                 pl.BlockSpec(conv_hist.shape, lambda i: (0, 0, 0))],
      out_shape=[jax.ShapeDtypeStruct(x.shape, x.dtype),
                 jax.ShapeDtypeStruct(pool_hist.shape, x.dtype),
                 jax.ShapeDtypeStruct(conv_hist.shape, x.dtype)],
      compiler_params=_params(1),
      name="even_layer_sample",
  )(*operands)


def _odd_prompt(x, weights, tile):
  batch, seq, d_model = x.shape
  d_gate = weights[2].shape[-1]
  tok = pl.BlockSpec((None, tile, d_model), lambda b, t: (b, t, 0))
  return pl.pallas_call(
      functools.partial(_odd_prompt_kernel, tile=tile, d_gate=d_gate),
      grid=(batch, seq // tile),
      in_specs=[tok] + [_resident(w) for w in weights],
      out_specs=tok,
      out_shape=jax.ShapeDtypeStruct(x.shape, x.dtype),
      compiler_params=_params(2),
      name="odd_layer_prompt",
  )(x, *weights)


def _odd_sample(x, weights):
  rows, d_model = x.shape
  d_gate = weights[2].shape[-1]
  operands = (x,) + tuple(weights)
  return pl.pallas_call(
      functools.partial(_odd_sample_kernel, d_gate=d_gate),
      grid=(1,),
      in_specs=[_resident(a) for a in operands],
      out_specs=[pl.BlockSpec(x.shape, lambda i: (0, 0)),
                 pl.BlockSpec((rows, d_gate), lambda i: (0, 0))],
      out_shape=[jax.ShapeDtypeStruct(x.shape, x.dtype),
                 jax.ShapeDtypeStruct((rows, d_gate), x.dtype)],
      compiler_params=_params(1),
      name="odd_layer_sample",
  )(*operands)


def _pool_block_diag(w_pool):
  per_tile = V7X_MXU_DIM // POOL_GROUP
  n_tiles = w_pool.shape[0] // per_tile
  out = jnp.zeros((n_tiles, V7X_MXU_DIM, V7X_MXU_DIM), jnp.bfloat16)
  for g in range(w_pool.shape[0]):
    i, j = divmod(g, per_tile)
    lo = j * POOL_GROUP
    out = out.at[i, lo:lo + POOL_GROUP, lo:lo + POOL_GROUP].set(
        w_pool[g].astype(jnp.bfloat16))
  return out


def kernel(x_prompt, x_sample, state_pool, state_conv, norm_mix, norm_ffn, norm_final, w_in_even, w_pool, pool_scale, conv_w, w_out_even, w_in_odd, norm_sg, w_s, b_s, w_out_odd, ffn_w_gate, ffn_w_up, ffn_w_down):
  depth = norm_mix.shape[0]
  assert depth == 2 and w_in_even.shape[0] == 1 and w_in_odd.shape[0] == 1
  dec_batch, dec_seq, d_model = x_sample.shape
  assert dec_seq == 1
  assert x_prompt.shape[1] % PROMPT_TILE == 0 and PROMPT_TILE % CHUNK == 0

  bf = lambda w: w.astype(jnp.bfloat16)
  row = lambda v: v.reshape(1, -1)

  def ffn_weights(layer):
    return (row(norm_ffn[layer]), bf(ffn_w_gate[layer]), bf(ffn_w_up[layer]),
            bf(ffn_w_down[layer]))

  even_w = (row(norm_mix[0]), bf(w_in_even[0]), _pool_block_diag(w_pool[0]),
            row(pool_scale[0]), conv_w[0], bf(w_out_even[0])) + ffn_weights(0)
  odd_w = (row(norm_mix[1]), bf(w_in_odd[0]), row(norm_sg[0]), w_s[0],
           b_s[0].T, bf(w_out_odd[0])) + ffn_weights(1) + (row(norm_final),)

  x1_p, pool_p, conv_p = _even_prompt(x_prompt, even_w, PROMPT_TILE)
  y_prompt = _odd_prompt(x1_p, odd_w, PROMPT_TILE)

  pool_hist = jnp.swapaxes(state_pool[0], 0, 1)
  conv_hist = jnp.swapaxes(state_conv[0], 0, 1)
  x1_s, pool_s, conv_s = _even_sample(x_sample[:, 0, :], pool_hist, conv_hist,
                                      even_w, PAST_LEN)
  y_s, v_s = _odd_sample(x1_s, odd_w)

  return (y_prompt,
          y_s[:, None, :],
          pool_p[None],
          jnp.swapaxes(pool_s, 0, 1)[None],
          conv_p[None],
          jnp.swapaxes(conv_s, 0, 1)[None],
          v_s[None, :, None, :])
```

```python
import functools
import math

import jax
import jax.numpy as jnp
from jax import lax
from jax.experimental import pallas as pl
from jax.experimental.pallas import tpu as pltpu

POOL_WINDOWS = (2, 4, 8, 16)
POOL_GROUP = 128
POOL_HIST = max(POOL_WINDOWS) - 1
CONV_WIDTH = 3
CONV_HIST = CONV_WIDTH - 1
CHUNK = 128
N_SG_HEADS = 8
EPS = 1e-6
PAST_LEN = 16384

V7X_SUBLANES = 8
V7X_BF16_SUBLANES = 16
V7X_MXU_DIM = 256
V7X_VMEM_LIMIT_BYTES = 56 * 1024 * 1024

POOL_PAD = -(-POOL_HIST // V7X_SUBLANES) * V7X_SUBLANES
CONV_PAD = -(-CONV_HIST // V7X_SUBLANES) * V7X_SUBLANES

PROMPT_TILE = 512


def _bf16_dot(a, b):
  return jnp.dot(a.astype(jnp.bfloat16), b, preferred_element_type=jnp.float32)


def _rmsnorm(x, g):
  y = x * lax.rsqrt(jnp.mean(x * x, axis=-1, keepdims=True) + EPS)
  return y * g


def _gelu_exact(x):
  return 0.5 * x * (1.0 + lax.erf(x * math.sqrt(0.5)))


def _pool_project(d, w_bd_ref, scale):
  halves = []
  for i in range(w_bd_ref.shape[0]):
    lo = i * V7X_MXU_DIM
    halves.append(_bf16_dot(d[:, lo:lo + V7X_MXU_DIM], w_bd_ref[i]))
  return jnp.concatenate(halves, axis=-1) * scale


def _ffn(x, g_ffn, wg_ref, wu_ref, wd_ref):
  h = _rmsnorm(x, g_ffn).astype(jnp.bfloat16)
  gate = jnp.dot(h, wg_ref[...], preferred_element_type=jnp.float32)
  up = jnp.dot(h, wu_ref[...], preferred_element_type=jnp.float32)
  act = gate * jax.nn.sigmoid(gate) * up
  return x + _bf16_dot(act, wd_ref[...])


def _even_in_proj(x, w, d_pool, d_conv):
  h = _rmsnorm(x, w["g_mix"][...])
  z = _bf16_dot(h, w["w_in"][...])
  return (z[:, :d_pool], z[:, d_pool:d_pool + d_conv],
          z[:, d_pool + d_conv:d_pool + 2 * d_conv], z[:, d_pool + 2 * d_conv:])


def _even_out(x, d, conv_y, bg, w):
  a_out = _pool_project(d, w["w_pool"], w["pool_scale"][...])
  mix = _bf16_dot(jnp.concatenate([a_out, bg * conv_y], axis=-1), w["w_out"][...])
  return _ffn(x + mix, w["g_ffn"][...], w["wg"], w["wu"], w["wd"])


def _even_prompt_tile(t, x_ref, w, y_ref, pool_state_ref, conv_state_ref,
                      p_ext, c_ext, *, tile, last_t, d_pool, d_conv):
  @pl.when(t == 0)
  def _():
    p_ext[0:POOL_PAD, :] = jnp.zeros((POOL_PAD, d_pool), jnp.float32)
    c_ext[0:CONV_PAD, :] = jnp.zeros((CONV_PAD, d_conv), jnp.float32)

  x = x_ref[...]
  p, xb, bg, cg = _even_in_proj(x, w, d_pool, d_conv)

  p_ext[POOL_PAD:POOL_PAD + tile, :] = p
  pos = t * tile + lax.broadcasted_iota(jnp.int32, (tile, POOL_GROUP), 0)
  d_groups = []
  for g, win in enumerate(POOL_WINDOWS):
    lo = g * POOL_GROUP
    acc = p[:, lo:lo + POOL_GROUP]
    for k in range(1, win):
      acc = acc + p_ext[pl.ds(POOL_PAD - k, tile), lo:lo + POOL_GROUP]
    cnt = jnp.minimum(pos + 1, win).astype(jnp.float32)
    d_groups.append(acc / cnt - p[:, lo:lo + POOL_GROUP])

  c = cg * xb
  c_ext[CONV_PAD:CONV_PAD + tile, :] = c
  conv_w = w["conv_w"]
  y = c_ext[pl.ds(CONV_PAD - 2, tile), :] * conv_w[0:1, :]
  y = y + c_ext[pl.ds(CONV_PAD - 1, tile), :] * conv_w[1:2, :]
  y = y + c * conv_w[2:3, :]

  y_ref[...] = _even_out(x, jnp.concatenate(d_groups, axis=-1), y, bg, w)

  @pl.when(t == last_t)
  def _():
    pool_state_ref[...] = p_ext[pl.ds(POOL_PAD + tile - POOL_HIST, POOL_HIST), :]
    conv_state_ref[...] = c_ext[pl.ds(CONV_PAD + tile - CONV_HIST, CONV_HIST), :]

  p_ext[0:POOL_PAD, :] = p_ext[tile:tile + POOL_PAD, :]
  c_ext[0:CONV_PAD, :] = c_ext[tile:tile + CONV_PAD, :]


def _even_sample_rows(x_ref, pool_hist_ref, conv_hist_ref, w, y_ref, p_ref,
                      c_ref, *, start, d_pool, d_conv):
  x = x_ref[...]
  p, xb, bg, cg = _even_in_proj(x, w, d_pool, d_conv)
  p_ref[...] = p

  d_groups = []
  for g, win in enumerate(POOL_WINDOWS):
    lo = g * POOL_GROUP
    acc = p[:, lo:lo + POOL_GROUP]
    for k in range(1, win):
      acc = acc + pool_hist_ref[POOL_HIST - k, :, lo:lo + POOL_GROUP]
    cnt = float(min(start + 1, win))
    d_groups.append(acc / cnt - p[:, lo:lo + POOL_GROUP])

  c = cg * xb
  c_ref[...] = c
  conv_w = w["conv_w"]
  y = conv_hist_ref[0] * conv_w[0:1, :]
  y = y + conv_hist_ref[1] * conv_w[1:2, :]
  y = y + c * conv_w[2:3, :]

  y_ref[...] = _even_out(x, jnp.concatenate(d_groups, axis=-1), y, bg, w)


EVEN_WEIGHT_NAMES = ("g_mix", "w_in", "w_pool", "pool_scale", "conv_w", "w_out",
                     "g_ffn", "wg", "wu", "wd")
ODD_WEIGHT_NAMES = ("g_mix", "w_in", "g_v", "w_s", "b_s_t", "w_out",
                    "g_ffn", "wg", "wu", "wd", "g_final")
N_CAST = 5


def _even_kernel(*refs, n_prompt_steps, tiles_per_seq, tile, start, d_pool, d_conv):
  x_ref, xs_ref, pool_hist_ref, conv_hist_ref = refs[:4]
  n_w = len(EVEN_WEIGHT_NAMES)
  w = dict(zip(EVEN_WEIGHT_NAMES, refs[4:4 + n_w]))
  cast_in = refs[4 + n_w:4 + n_w + N_CAST]
  outs = refs[4 + n_w + N_CAST:]
  y_ref, pool_state_ref, conv_state_ref, ys_ref, ps_ref, cs_ref = outs[:6]
  cast_out = outs[6:6 + N_CAST]
  p_ext, c_ext = outs[6 + N_CAST:]

  s = pl.program_id(0)

  @pl.when(s < n_prompt_steps)
  def _():
    for src, dst in zip(cast_in, cast_out):
      dst[...] = src[...].astype(dst.dtype)
    _even_prompt_tile(s % tiles_per_seq, x_ref, w, y_ref, pool_state_ref,
                      conv_state_ref, p_ext, c_ext, tile=tile,
                      last_t=tiles_per_seq - 1, d_pool=d_pool, d_conv=d_conv)

  @pl.when(s == n_prompt_steps)
  def _():
    _even_sample_rows(xs_ref, pool_hist_ref, conv_hist_ref, w, ys_ref, ps_ref,
                      cs_ref, start=start, d_pool=d_pool, d_conv=d_conv)


def _odd_in_proj(x, w, d_gate):
  h = _rmsnorm(x, w["g_mix"][...])
  z = _gelu_exact(_bf16_dot(h, w["w_in"][...]))
  return z[:, :d_gate], _rmsnorm(z[:, d_gate:], w["g_v"][...])


def _odd_out(x, gated, w):
  mix = _bf16_dot(gated, w["w_out"][...])
  y = _ffn(x + mix, w["g_ffn"][...], w["wg"], w["wu"], w["wd"])
  return _rmsnorm(y, w["g_final"][...])


def _odd_prompt_tile(x_ref, w, y_ref, *, tile, d_gate):
  x = x_ref[...]
  u, v = _odd_in_proj(x, w, d_gate)
  vb = v.astype(jnp.bfloat16)

  n_chunks = tile // CHUNK
  head = d_gate // N_SG_HEADS
  row = lax.broadcasted_iota(jnp.int32, (CHUNK, CHUNK), 0)
  col = lax.broadcasted_iota(jnp.int32, (CHUNK, CHUNK), 1)
  causal = row >= col
  per_head = []
  for hd in range(N_SG_HEADS):
    w_h = jnp.where(causal, w["w_s"][hd], 0.0).astype(jnp.bfloat16)
    rhs = jnp.concatenate(
        [vb[c * CHUNK:(c + 1) * CHUNK, hd * head:(hd + 1) * head]
         for c in range(n_chunks)], axis=1)
    o = jnp.dot(w_h, rhs, preferred_element_type=jnp.float32)
    per_head.append(o + w["b_s_t"][:, hd:hd + 1])
  mixed = jnp.concatenate(
      [jnp.concatenate([o[:, c * head:(c + 1) * head] for o in per_head], axis=1)
       for c in range(n_chunks)], axis=0)

  y_ref[...] = _odd_out(x, u * mixed, w)


def _odd_sample_rows(x_ref, w, y_ref, v_ref, *, d_gate):
  x = x_ref[...]
  u, v = _odd_in_proj(x, w, d_gate)
  v_ref[...] = v

  head = d_gate // N_SG_HEADS
  rows = x.shape[0]
  vb = v.astype(jnp.bfloat16).astype(jnp.float32)
  per_head = []
  for hd in range(N_SG_HEADS):
    w00 = w["w_s"][hd, 0:1, 0:1].astype(jnp.bfloat16).astype(jnp.float32)
    b0 = w["b_s_t"][0:1, hd:hd + 1]
    per_head.append(vb[:, hd * head:(hd + 1) * head] * jnp.broadcast_to(w00, (rows, head))
                    + jnp.broadcast_to(b0, (rows, head)))

  y_ref[...] = _odd_out(x, u * jnp.concatenate(per_head, axis=1), w)


def _odd_kernel(*refs, n_prompt_steps, tile, d_gate):
  x_ref, xs_ref = refs[:2]
  n_w = len(ODD_WEIGHT_NAMES)
  w = dict(zip(ODD_WEIGHT_NAMES, refs[2:2 + n_w]))
  y_ref, ys_ref, vs_ref = refs[2 + n_w:]

  s = pl.program_id(0)

  @pl.when(s < n_prompt_steps)
  def _():
    _odd_prompt_tile(x_ref, w, y_ref, tile=tile, d_gate=d_gate)

  @pl.when(s == n_prompt_steps)
  def _():
    _odd_sample_rows(xs_ref, w, ys_ref, vs_ref, d_gate=d_gate)


def _resident(arr):
  zeros = (0,) * arr.ndim
  return pl.BlockSpec(arr.shape, lambda s: zeros, pipeline_mode=pl.Buffered(1))


def _whole_out(shape):
  zeros = (0,) * len(shape)
  return pl.BlockSpec(shape, lambda s: zeros)


def _params():
  return pltpu.CompilerParams(dimension_semantics=("arbitrary",),
                              vmem_limit_bytes=V7X_VMEM_LIMIT_BYTES)


def _prompt_specs(x, tile):
  batch, seq, d_model = x.shape
  tiles_per_seq = seq // tile
  n_steps = batch * tiles_per_seq
  step = lambda s: jnp.minimum(s, n_steps - 1)
  tok = pl.BlockSpec((None, tile, d_model),
                     lambda s: (step(s) // tiles_per_seq, step(s) % tiles_per_seq, 0))
  per_seq = lambda rows, ch: pl.BlockSpec(
      (None, rows, ch), lambda s: (step(s) // tiles_per_seq, 0, 0))
  return n_steps, tiles_per_seq, step, tok, per_seq


def _cast_slab_spec(rows, cols, n_steps, step, layer):
  slab = rows // n_steps
  steps_per_slab = 1
  while slab % V7X_BF16_SUBLANES:
    slab *= 2
    steps_per_slab *= 2
  src = pl.BlockSpec((None, slab, cols), lambda s: (layer, step(s) // steps_per_slab, 0))
  dst = pl.BlockSpec((slab, cols), lambda s: (step(s) // steps_per_slab, 0))
  return src, dst


def _even_layer(x, xs, pool_hist, conv_hist, weights, next_mats, next_layer, tile, start):
  n_steps, tiles_per_seq, step, tok, per_seq = _prompt_specs(x, tile)
  batch = x.shape[0]
  rows, d_model = xs.shape
  d_pool = pool_hist.shape[-1]
  d_conv = conv_hist.shape[-1]
  cast_specs = [_cast_slab_spec(m.shape[1], m.shape[2], n_steps, step, next_layer[i])
                for i, m in enumerate(next_mats)]
  f32 = x.dtype
  return pl.pallas_call(
      functools.partial(_even_kernel, n_prompt_steps=n_steps,
                        tiles_per_seq=tiles_per_seq, tile=tile, start=start,
                        d_pool=d_pool, d_conv=d_conv),
      grid=(n_steps + 1,),
      in_specs=([tok, _resident(xs), _resident(pool_hist), _resident(conv_hist)]
                + [_resident(w) for w in weights] + [c[0] for c in cast_specs]),
      out_specs=([tok, per_seq(POOL_HIST, d_pool), per_seq(CONV_HIST, d_conv),
                  _whole_out(xs.shape), _whole_out((rows, d_pool)),
                  _whole_out((rows, d_conv))] + [c[1] for c in cast_specs]),
      out_shape=([jax.ShapeDtypeStruct(x.shape, f32),
                  jax.ShapeDtypeStruct((batch, POOL_HIST, d_pool), f32),
                  jax.ShapeDtypeStruct((batch, CONV_HIST, d_conv), f32),
                  jax.ShapeDtypeStruct(xs.shape, f32),
                  jax.ShapeDtypeStruct((rows, d_pool), f32),
                  jax.ShapeDtypeStruct((rows, d_conv), f32)]
                 + [jax.ShapeDtypeStruct(m.shape[1:], jnp.bfloat16) for m in next_mats]),
      scratch_shapes=[pltpu.VMEM((POOL_PAD + tile, d_pool), jnp.float32),
                      pltpu.VMEM((CONV_PAD + tile, d_conv), jnp.float32)],
      compiler_params=_params(),
      name="even_layer",
  )(x, xs, pool_hist, conv_hist, *weights, *next_mats)


def _odd_layer(x, xs, weights, tile):
  n_steps, _, _, tok, _ = _prompt_specs(x, tile)
  rows, d_model = xs.shape
  d_gate = weights[ODD_WEIGHT_NAMES.index("g_v")].shape[-1]
  return pl.pallas_call(
      functools.partial(_odd_kernel, n_prompt_steps=n_steps, tile=tile, d_gate=d_gate),
      grid=(n_steps + 1,),
      in_specs=[tok, _resident(xs)] + [_resident(w) for w in weights],
      out_specs=[tok, _whole_out(xs.shape), _whole_out((rows, d_gate))],
      out_shape=[jax.ShapeDtypeStruct(x.shape, x.dtype),
                 jax.ShapeDtypeStruct(xs.shape, x.dtype),
                 jax.ShapeDtypeStruct((rows, d_gate), x.dtype)],
      compiler_params=_params(),
      name="odd_layer",
  )(x, xs, *weights)


def _pool_block_diag(w_pool):
  per_tile = V7X_MXU_DIM // POOL_GROUP
  n_tiles = w_pool.shape[0] // per_tile
  out = jnp.zeros((n_tiles, V7X_MXU_DIM, V7X_MXU_DIM), jnp.bfloat16)
  for g in range(w_pool.shape[0]):
    i, j = divmod(g, per_tile)
    lo = j * POOL_GROUP
    out = out.at[i, lo:lo + POOL_GROUP, lo:lo + POOL_GROUP].set(
        w_pool[g].astype(jnp.bfloat16))
  return out


def kernel(x_prompt, x_sample, state_pool, state_conv, norm_mix, norm_ffn, norm_final, w_in_even, w_pool, pool_scale, conv_w, w_out_even, w_in_odd, norm_sg, w_s, b_s, w_out_odd, ffn_w_gate, ffn_w_up, ffn_w_down):
  depth = norm_mix.shape[0]
  assert depth == 2 and w_in_even.shape[0] == 1 and w_in_odd.shape[0] == 1
  assert x_sample.shape[1] == 1
  assert x_prompt.shape[1] % PROMPT_TILE == 0 and PROMPT_TILE % CHUNK == 0

  bf = lambda w: w.astype(jnp.bfloat16)
  row = lambda v: v.reshape(1, -1)

  even_w = (row(norm_mix[0]), bf(w_in_even[0]), _pool_block_diag(w_pool[0]),
            row(pool_scale[0]), conv_w[0], bf(w_out_even[0]), row(norm_ffn[0]),
            bf(ffn_w_gate[0]), bf(ffn_w_up[0]), bf(ffn_w_down[0]))
  odd_mats = (w_in_odd, w_out_odd, ffn_w_gate, ffn_w_up, ffn_w_down)
  odd_mat_layer = (0, 0, 1, 1, 1)

  pool_hist = jnp.swapaxes(state_pool[0], 0, 1)
  conv_hist = jnp.swapaxes(state_conv[0], 0, 1)

  (x1_p, pool_p, conv_p, x1_s, p_s, c_s, w_in_o, w_out_o, wg_o, wu_o,
   wd_o) = _even_layer(x_prompt, x_sample[:, 0, :], pool_hist, conv_hist, even_w,
                       odd_mats, odd_mat_layer, PROMPT_TILE, PAST_LEN)

  odd_w = (row(norm_mix[1]), w_in_o, row(norm_sg[0]), w_s[0], b_s[0].T, w_out_o,
           row(norm_ffn[1]), wg_o, wu_o, wd_o, row(norm_final))
  y_p, y_s, v_s = _odd_layer(x1_p, x1_s, odd_w, PROMPT_TILE)

  pool_s = jnp.concatenate([state_pool[:, :, 1:], p_s[None, :, None, :]], axis=2)
  conv_s = jnp.concatenate([state_conv[:, :, 1:], c_s[None, :, None, :]], axis=2)

  return (y_p, y_s[:, None, :], pool_p[None], pool_s, conv_p[None], conv_s,
          v_s[None, :, None, :])
```

```python
import functools
import math

import jax
import jax.numpy as jnp
from jax import lax
from jax.experimental import pallas as pl
from jax.experimental.pallas import tpu as pltpu

POOL_WINDOWS = (2, 4, 8, 16)
POOL_GROUP = 128
POOL_HIST = max(POOL_WINDOWS) - 1
CONV_WIDTH = 3
CONV_HIST = CONV_WIDTH - 1
CHUNK = 128
N_SG_HEADS = 8
EPS = 1e-6
PAST_LEN = 16384

V7X_SUBLANES = 8
V7X_LANES = 128
V7X_BF16_SUBLANES = 16
V7X_MXU_DIM = 256
V7X_VMEM_LIMIT_BYTES = 60000 * 1024

POOL_PAD = -(-POOL_HIST // V7X_SUBLANES) * V7X_SUBLANES
CONV_PAD = -(-CONV_HIST // V7X_SUBLANES) * V7X_SUBLANES

PROMPT_TILE = 512


def _bf16_dot(a, b):
  return jnp.dot(a.astype(jnp.bfloat16), b, preferred_element_type=jnp.float32)


def _rmsnorm(x, g):
  y = x * lax.rsqrt(jnp.mean(x * x, axis=-1, keepdims=True) + EPS)
  return y * g


def _gelu_exact(x):
  return 0.5 * x * (1.0 + lax.erf(x * math.sqrt(0.5)))


def _skew_cols(n):
  return V7X_LANES if (n // V7X_LANES) % V7X_SUBLANES == 0 else 0


def _skewed_dot(a, w_ref, n):
  return jnp.dot(a, w_ref[:, :n], preferred_element_type=jnp.float32)


def _pool_project(d, w_bd_ref, scale):
  halves = []
  for i in range(w_bd_ref.shape[0]):
    lo = i * V7X_MXU_DIM
    halves.append(_bf16_dot(d[:, lo:lo + V7X_MXU_DIM], w_bd_ref[i]))
  return jnp.concatenate(halves, axis=-1) * scale


def _ffn_stages(load_x1, w, emit, final_norm):
  st = {}

  def norm():
    st["x"] = load_x1()
    st["h"] = _rmsnorm(st["x"], w["g_ffn"][...]).astype(jnp.bfloat16)

  def gate_up():
    st["gate"] = jnp.dot(st["h"], w["wg"][...], preferred_element_type=jnp.float32)
    st["up"] = jnp.dot(st["h"], w["wu"][...], preferred_element_type=jnp.float32)

  def activate():
    gate = st["gate"]
    st["act"] = (gate * jax.nn.sigmoid(gate) * st["up"]).astype(jnp.bfloat16)

  def down():
    y = st["x"] + _skewed_dot(st["act"], w["wd"], st["x"].shape[1])
    emit(_rmsnorm(y, w["g_final"][...]) if final_norm else y)

  return [norm, gate_up, activate, down]


def _even_mix_stages(load_x, w, emit, mixers, d_pool, d_conv):
  st = {}

  def norm():
    st["x"] = load_x()
    st["h"] = _rmsnorm(st["x"], w["g_mix"][...]).astype(jnp.bfloat16)

  def in_proj():
    st["z"] = _skewed_dot(st["h"], w["w_in"], d_pool + 3 * d_conv)

  def mix():
    z = st["z"]
    p = z[:, :d_pool]
    xb = z[:, d_pool:d_pool + d_conv]
    bg = z[:, d_pool + d_conv:d_pool + 2 * d_conv]
    cg = z[:, d_pool + 2 * d_conv:]
    pooled, conv_y = mixers(p, cg * xb)
    a_out = _pool_project(pooled - p, w["w_pool"], w["pool_scale"][...])
    st["mixed"] = jnp.concatenate([a_out, bg * conv_y], axis=-1).astype(jnp.bfloat16)

  def out_proj():
    emit(st["x"] + _skewed_dot(st["mixed"], w["w_out"], st["x"].shape[1]))

  return [norm, in_proj, mix, out_proj]


def _even_prompt_mixers(t, p_ext, c_ext, conv_w, tile):
  def mixers(p, c):
    p_ext[POOL_PAD:POOL_PAD + tile, :] = p
    pos = t * tile + lax.broadcasted_iota(jnp.int32, (tile, POOL_GROUP), 0)
    pooled = []
    for g, win in enumerate(POOL_WINDOWS):
      lo = g * POOL_GROUP
      acc = p[:, lo:lo + POOL_GROUP]
      for k in range(1, win):
        acc = acc + p_ext[pl.ds(POOL_PAD - k, tile), lo:lo + POOL_GROUP]
      pooled.append(acc / jnp.minimum(pos + 1, win).astype(jnp.float32))
    c_ext[CONV_PAD:CONV_PAD + tile, :] = c
    y = c_ext[pl.ds(CONV_PAD - 2, tile), :] * conv_w[0:1, :]
    y = y + c_ext[pl.ds(CONV_PAD - 1, tile), :] * conv_w[1:2, :]
    y = y + c * conv_w[2:3, :]
    return jnp.concatenate(pooled, axis=-1), y
  return mixers


def _pool_history_sums_kernel(pool_hist_ref, sums_ref):
  for g, win in enumerate(POOL_WINDOWS):
    lo = g * POOL_GROUP
    acc = pool_hist_ref[POOL_HIST - 1, :, lo:lo + POOL_GROUP]
    for k in range(2, win):
      acc = acc + pool_hist_ref[POOL_HIST - k, :, lo:lo + POOL_GROUP]
    sums_ref[:, lo:lo + POOL_GROUP] = acc


def _even_sample_mixers(pool_sums_ref, conv_hist_ref, conv_w, p_ref, c_ref, start):
  def mixers(p, c):
    p_ref[...] = p
    c_ref[...] = c
    pooled = []
    for g, win in enumerate(POOL_WINDOWS):
      lo = g * POOL_GROUP
      acc = p[:, lo:lo + POOL_GROUP] + pool_sums_ref[:, lo:lo + POOL_GROUP]
      pooled.append(acc / float(min(start + 1, win)))
    y = conv_hist_ref[0] * conv_w[0:1, :]
    y = y + conv_hist_ref[1] * conv_w[1:2, :]
    y = y + c * conv_w[2:3, :]
    return jnp.concatenate(pooled, axis=-1), y
  return mixers


def _even_mix_head(t, p_ext, c_ext):
  @pl.when(t == 0)
  def _():
    p_ext[0:POOL_PAD, :] = jnp.zeros((POOL_PAD, p_ext.shape[1]), jnp.float32)
    c_ext[0:CONV_PAD, :] = jnp.zeros((CONV_PAD, c_ext.shape[1]), jnp.float32)


def _even_mix_tail(t, pool_state_ref, conv_state_ref, p_ext, c_ext, *, tile, last_t):
  @pl.when(t == last_t)
  def _():
    pool_state_ref[...] = p_ext[pl.ds(POOL_PAD + tile - POOL_HIST, POOL_HIST), :]
    conv_state_ref[...] = c_ext[pl.ds(CONV_PAD + tile - CONV_HIST, CONV_HIST), :]

  p_ext[0:POOL_PAD, :] = p_ext[tile:tile + POOL_PAD, :]
  c_ext[0:CONV_PAD, :] = c_ext[tile:tile + CONV_PAD, :]


EVEN_WEIGHT_NAMES = ("g_mix", "w_in", "w_pool", "pool_scale", "conv_w", "w_out",
                     "g_ffn", "wg", "wu", "wd")
ODD_WEIGHT_NAMES = ("g_mix", "w_in", "g_v", "w_s", "b_s_t", "w_out",
                    "g_ffn", "wg", "wu", "wd", "g_final")
N_CAST = 5


def _run(stages):
  for stage in stages:
    stage()


def _pipelined_steps(s, n_tiles, mix_stages, ffn_stages):
  @pl.when(s == 0)
  def _():
    _run(mix_stages())

  @pl.when((s > 0) & (s < n_tiles))
  def _():
    mix, ffn = mix_stages(), ffn_stages()
    _run([mix[0], mix[1], ffn[0], ffn[1], mix[2], ffn[2], ffn[3], mix[3]])

  @pl.when(s == n_tiles)
  def _():
    _run(ffn_stages())


def _set(ref):
  def emit(value):
    ref[...] = value
  return emit


def _even_kernel(*refs, n_tiles, tiles_per_seq, tile, start, d_pool, d_conv):
  x_ref, xs_ref, pool_sums_ref, conv_hist_ref = refs[:4]
  n_w = len(EVEN_WEIGHT_NAMES)
  w = dict(zip(EVEN_WEIGHT_NAMES, refs[4:4 + n_w]))
  cast_in = refs[4 + n_w:4 + n_w + N_CAST]
  outs = refs[4 + n_w + N_CAST:]
  y_ref, pool_state_ref, conv_state_ref, ys_ref, ps_ref, cs_ref = outs[:6]
  cast_out = outs[6:6 + N_CAST]
  p_ext, c_ext, x1_buf = outs[6 + N_CAST:]

  s = pl.program_id(0)
  t = s % tiles_per_seq
  slot = s % 2

  @pl.when(s < n_tiles)
  def _():
    _even_mix_head(t, p_ext, c_ext)
    for src, dst in zip(cast_in, cast_out):
      n = src.shape[1]
      dst[:, :n] = src[...].astype(dst.dtype)
      if dst.shape[1] > n:
        dst[:, n:] = jnp.zeros((dst.shape[0], dst.shape[1] - n), dst.dtype)

  def mix_stages():
    return _even_mix_stages(
        lambda: x_ref[...], w, _set(x1_buf.at[slot]),
        _even_prompt_mixers(t, p_ext, c_ext, w["conv_w"], tile), d_pool, d_conv)

  def ffn_stages():
    return _ffn_stages(lambda: x1_buf[1 - slot], w, _set(y_ref), False)

  _pipelined_steps(s, n_tiles, mix_stages, ffn_stages)

  @pl.when(s < n_tiles)
  def _():
    _even_mix_tail(t, pool_state_ref, conv_state_ref, p_ext, c_ext, tile=tile,
                   last_t=tiles_per_seq - 1)

  @pl.when(s == n_tiles + 1)
  def _():
    st = {}
    mixers = _even_sample_mixers(pool_sums_ref, conv_hist_ref, w["conv_w"],
                                 ps_ref, cs_ref, start)
    _run(_even_mix_stages(lambda: xs_ref[...], w, lambda v: st.update(x1=v),
                          mixers, d_pool, d_conv))
    _run(_ffn_stages(lambda: st["x1"], w, _set(ys_ref), False))


def _odd_mix_stages(load_x, w, emit, gating, d_gate):
  st = {}

  def norm():
    st["x"] = load_x()
    st["h"] = _rmsnorm(st["x"], w["g_mix"][...]).astype(jnp.bfloat16)

  def in_proj():
    st["z"] = _skewed_dot(st["h"], w["w_in"], 2 * d_gate)

  def gate():
    z = _gelu_exact(st["z"])
    v = _rmsnorm(z[:, d_gate:], w["g_v"][...])
    st["gated"] = (z[:, :d_gate] * gating(v)).astype(jnp.bfloat16)

  def out_proj():
    emit(st["x"] + _skewed_dot(st["gated"], w["w_out"], st["x"].shape[1]))

  return [norm, in_proj, gate, out_proj]


def _odd_prompt_gating(w, tile, d_gate):
  def gating(v):
    vb = v.astype(jnp.bfloat16)
    n_chunks = tile // CHUNK
    head = d_gate // N_SG_HEADS
    row = lax.broadcasted_iota(jnp.int32, (CHUNK, CHUNK), 0)
    col = lax.broadcasted_iota(jnp.int32, (CHUNK, CHUNK), 1)
    causal = row >= col
    per_head = []
    for hd in range(N_SG_HEADS):
      w_h = jnp.where(causal, w["w_s"][hd], 0.0).astype(jnp.bfloat16)
      rhs = jnp.concatenate(
          [vb[c * CHUNK:(c + 1) * CHUNK, hd * head:(hd + 1) * head]
           for c in range(n_chunks)], axis=1)
      o = jnp.dot(w_h, rhs, preferred_element_type=jnp.float32)
      per_head.append(o + w["b_s_t"][:, hd:hd + 1])
    return jnp.concatenate(
        [jnp.concatenate([o[:, c * head:(c + 1) * head] for o in per_head], axis=1)
         for c in range(n_chunks)], axis=0)
  return gating


def _odd_sample_gating(w, v_ref, d_gate):
  def gating(v):
    v_ref[...] = v
    head = d_gate // N_SG_HEADS
    rows = v.shape[0]
    vb = v.astype(jnp.bfloat16).astype(jnp.float32)
    per_head = []
    for hd in range(N_SG_HEADS):
      w00 = w["w_s"][hd, 0:1, 0:1].astype(jnp.bfloat16).astype(jnp.float32)
      b0 = w["b_s_t"][0:1, hd:hd + 1]
      per_head.append(vb[:, hd * head:(hd + 1) * head] * jnp.broadcast_to(w00, (rows, head))
                      + jnp.broadcast_to(b0, (rows, head)))
    return jnp.concatenate(per_head, axis=1)
  return gating


def _odd_kernel(*refs, n_tiles, tile, d_gate):
  x_ref, xs_ref = refs[:2]
  n_w = len(ODD_WEIGHT_NAMES)
  w = dict(zip(ODD_WEIGHT_NAMES, refs[2:2 + n_w]))
  y_ref, ys_ref, vs_ref, x1_buf = refs[2 + n_w:]

  s = pl.program_id(0)
  slot = s % 2

  def mix_stages():
    return _odd_mix_stages(lambda: x_ref[...], w, _set(x1_buf.at[slot]),
                           _odd_prompt_gating(w, tile, d_gate), d_gate)

  def ffn_stages():
    return _ffn_stages(lambda: x1_buf[1 - slot], w, _set(y_ref), True)

  _pipelined_steps(s, n_tiles, mix_stages, ffn_stages)

  @pl.when(s == n_tiles + 1)
  def _():
    st = {}
    _run(_odd_mix_stages(lambda: xs_ref[...], w, lambda v: st.update(x1=v),
                         _odd_sample_gating(w, vs_ref, d_gate), d_gate))
    _run(_ffn_stages(lambda: st["x1"], w, _set(ys_ref), True))


def _resident(arr):
  zeros = (0,) * arr.ndim
  return pl.BlockSpec(arr.shape, lambda s: zeros, pipeline_mode=pl.Buffered(1))


def _whole_out(shape):
  zeros = (0,) * len(shape)
  return pl.BlockSpec(shape, lambda s: zeros)


def _params():
  return pltpu.CompilerParams(dimension_semantics=("arbitrary",),
                              vmem_limit_bytes=V7X_VMEM_LIMIT_BYTES)


def _prompt_specs(x, tile):
  batch, seq, d_model = x.shape
  tiles_per_seq = seq // tile
  n_tiles = batch * tiles_per_seq
  mix_tile = lambda s: jnp.clip(s, 0, n_tiles - 1)
  ffn_tile = lambda s: jnp.clip(s - 1, 0, n_tiles - 1)
  block = lambda which: pl.BlockSpec(
      (None, tile, d_model),
      lambda s: (which(s) // tiles_per_seq, which(s) % tiles_per_seq, 0))
  per_seq = lambda rows, ch: pl.BlockSpec(
      (None, rows, ch), lambda s: (mix_tile(s) // tiles_per_seq, 0, 0))
  return n_tiles, tiles_per_seq, mix_tile, block(mix_tile), block(ffn_tile), per_seq


def _cast_slab_spec(rows, cols, n_steps, step, layer):
  slab = rows // n_steps
  steps_per_slab = 1
  while slab % V7X_BF16_SUBLANES:
    slab *= 2
    steps_per_slab *= 2
  src = pl.BlockSpec((None, slab, cols), lambda s: (layer, step(s) // steps_per_slab, 0))
  dst = pl.BlockSpec((slab, cols + _skew_cols(cols)),
                     lambda s: (step(s) // steps_per_slab, 0))
  return src, dst


def _pool_history_sums(pool_hist):
  rows, d_pool = pool_hist.shape[1:]
  return pl.pallas_call(
      _pool_history_sums_kernel,
      grid=(1,),
      in_specs=[pl.BlockSpec(pool_hist.shape, lambda i: (0, 0, 0))],
      out_specs=pl.BlockSpec((rows, d_pool), lambda i: (0, 0)),
      out_shape=jax.ShapeDtypeStruct((rows, d_pool), pool_hist.dtype),
      name="pool_history_sums",
  )(pool_hist)


def _even_layer(x, xs, pool_sums, conv_hist, weights, next_mats, next_layer, tile, start):
  n_tiles, tiles_per_seq, step, tok_in, tok_out, per_seq = _prompt_specs(x, tile)
  batch = x.shape[0]
  rows, d_model = xs.shape
  d_pool = pool_sums.shape[-1]
  d_conv = conv_hist.shape[-1]
  cast_specs = [_cast_slab_spec(m.shape[1], m.shape[2], n_tiles, step, next_layer[i])
                for i, m in enumerate(next_mats)]
  f32 = x.dtype
  return pl.pallas_call(
      functools.partial(_even_kernel, n_tiles=n_tiles,
                        tiles_per_seq=tiles_per_seq, tile=tile, start=start,
                        d_pool=d_pool, d_conv=d_conv),
      grid=(n_tiles + 2,),
      in_specs=([tok_in, _resident(xs), _resident(pool_sums), _resident(conv_hist)]
                + [_resident(w) for w in weights] + [c[0] for c in cast_specs]),
      out_specs=([tok_out, per_seq(POOL_HIST, d_pool), per_seq(CONV_HIST, d_conv),
                  _whole_out(xs.shape), _whole_out((rows, d_pool)),
                  _whole_out((rows, d_conv))] + [c[1] for c in cast_specs]),
      out_shape=([jax.ShapeDtypeStruct(x.shape, f32),
                  jax.ShapeDtypeStruct((batch, POOL_HIST, d_pool), f32),
                  jax.ShapeDtypeStruct((batch, CONV_HIST, d_conv), f32),
                  jax.ShapeDtypeStruct(xs.shape, f32),
                  jax.ShapeDtypeStruct((rows, d_pool), f32),
                  jax.ShapeDtypeStruct((rows, d_conv), f32)]
                 + [jax.ShapeDtypeStruct((m.shape[1], m.shape[2] + _skew_cols(m.shape[2])),
                                         jnp.bfloat16) for m in next_mats]),
      scratch_shapes=[pltpu.VMEM((POOL_PAD + tile, d_pool), jnp.float32),
                      pltpu.VMEM((CONV_PAD + tile, d_conv), jnp.float32),
                      pltpu.VMEM((2, tile, d_model), jnp.float32)],
      compiler_params=_params(),
      name="even_layer",
  )(x, xs, pool_sums, conv_hist, *weights, *next_mats)


def _odd_layer(x, xs, weights, tile):
  n_tiles, _, _, tok_in, tok_out, _ = _prompt_specs(x, tile)
  rows, d_model = xs.shape
  d_gate = weights[ODD_WEIGHT_NAMES.index("g_v")].shape[-1]
  return pl.pallas_call(
      functools.partial(_odd_kernel, n_tiles=n_tiles, tile=tile, d_gate=d_gate),
      grid=(n_tiles + 2,),
      in_specs=[tok_in, _resident(xs)] + [_resident(w) for w in weights],
      out_specs=[tok_out, _whole_out(xs.shape), _whole_out((rows, d_gate))],
      out_shape=[jax.ShapeDtypeStruct(x.shape, x.dtype),
                 jax.ShapeDtypeStruct(xs.shape, x.dtype),
                 jax.ShapeDtypeStruct((rows, d_gate), x.dtype)],
      scratch_shapes=[pltpu.VMEM((2, tile, d_model), jnp.float32)],
      compiler_params=_params(),
      name="odd_layer",
  )(x, xs, *weights)


def _pool_block_diag(w_pool):
  per_tile = V7X_MXU_DIM // POOL_GROUP
  n_tiles = w_pool.shape[0] // per_tile
  out = jnp.zeros((n_tiles, V7X_MXU_DIM, V7X_MXU_DIM), jnp.bfloat16)
  for g in range(w_pool.shape[0]):
    i, j = divmod(g, per_tile)
    lo = j * POOL_GROUP
    out = out.at[i, lo:lo + POOL_GROUP, lo:lo + POOL_GROUP].set(
        w_pool[g].astype(jnp.bfloat16))
  return out


def kernel(x_prompt, x_sample, state_pool, state_conv, norm_mix, norm_ffn, norm_final, w_in_even, w_pool, pool_scale, conv_w, w_out_even, w_in_odd, norm_sg, w_s, b_s, w_out_odd, ffn_w_gate, ffn_w_up, ffn_w_down):
  depth = norm_mix.shape[0]
  assert depth == 2 and w_in_even.shape[0] == 1 and w_in_odd.shape[0] == 1
  assert x_sample.shape[1] == 1
  assert x_prompt.shape[1] % PROMPT_TILE == 0 and PROMPT_TILE % CHUNK == 0

  def bf(w):
    return jnp.pad(w.astype(jnp.bfloat16), ((0, 0), (0, _skew_cols(w.shape[1]))))

  row = lambda v: v.reshape(1, -1)

  even_w = (row(norm_mix[0]), bf(w_in_even[0]), _pool_block_diag(w_pool[0]),
            row(pool_scale[0]), conv_w[0], bf(w_out_even[0]), row(norm_ffn[0]),
            bf(ffn_w_gate[0]), bf(ffn_w_up[0]), bf(ffn_w_down[0]))
  odd_mats = (w_in_odd, w_out_odd, ffn_w_gate, ffn_w_up, ffn_w_down)
  odd_mat_layer = (0, 0, 1, 1, 1)

  pool_hist = jnp.swapaxes(state_pool[0], 0, 1)
  conv_hist = jnp.swapaxes(state_conv[0], 0, 1)

  (x1_p, pool_p, conv_p, x1_s, p_s, c_s, w_in_o, w_out_o, wg_o, wu_o,
   wd_o) = _even_layer(x_prompt, x_sample[:, 0, :], _pool_history_sums(pool_hist),
                       conv_hist, even_w,
                       odd_mats, odd_mat_layer, PROMPT_TILE, PAST_LEN)

  odd_w = (row(norm_mix[1]), w_in_o, row(norm_sg[0]), w_s[0], b_s[0].T, w_out_o,
           row(norm_ffn[1]), wg_o, wu_o, wd_o, row(norm_final))
  y_p, y_s, v_s = _odd_layer(x1_p, x1_s, odd_w, PROMPT_TILE)

  pool_s = jnp.concatenate([state_pool[:, :, 1:], p_s[None, :, None, :]], axis=2)
  conv_s = jnp.concatenate([state_conv[:, :, 1:], c_s[None, :, None, :]], axis=2)

  return (y_p, y_s[:, None, :], pool_p[None], pool_s, conv_p[None], conv_s,
          v_s[None, :, None, :])
```

```python
import functools
import math

import jax
import jax.numpy as jnp
from jax import lax
from jax.experimental import pallas as pl
from jax.experimental.pallas import tpu as pltpu

POOL_WINDOWS = (2, 4, 8, 16)
POOL_GROUP = 128
POOL_HIST = max(POOL_WINDOWS) - 1
CONV_WIDTH = 3
CONV_HIST = CONV_WIDTH - 1
CHUNK = 128
N_SG_HEADS = 8
EPS = 1e-6
PAST_LEN = 16384

V7X_SUBLANES = 8
V7X_LANES = 128
V7X_BF16_SUBLANES = 16
V7X_MXU_DIM = 256
V7X_VMEM_LIMIT_BYTES = 60000 * 1024

POOL_PAD = -(-POOL_HIST // V7X_SUBLANES) * V7X_SUBLANES
CONV_PAD = -(-CONV_HIST // V7X_SUBLANES) * V7X_SUBLANES

PROMPT_TILE = 512


def _bf16_dot(a, b):
  return jnp.dot(a.astype(jnp.bfloat16), b, preferred_element_type=jnp.float32)


def _rmsnorm(x, g):
  y = x * lax.rsqrt(jnp.mean(x * x, axis=-1, keepdims=True) + EPS)
  return y * g


def _gelu_exact(x):
  return 0.5 * x * (1.0 + lax.erf(x * math.sqrt(0.5)))


def _skew_cols(n):
  return V7X_LANES if (n // V7X_LANES) % V7X_SUBLANES == 0 else 0


def _skewed_dot(a, w_ref, n):
  return jnp.dot(a, w_ref[:, :n], preferred_element_type=jnp.float32)


def _pool_project(d, w_bd_ref, scale):
  halves = []
  for i in range(w_bd_ref.shape[0]):
    lo = i * V7X_MXU_DIM
    halves.append(_bf16_dot(d[:, lo:lo + V7X_MXU_DIM], w_bd_ref[i]))
  return jnp.concatenate(halves, axis=-1) * scale


def _ffn_stages(load_x1, w, emit, final_norm):
  st = {}

  def norm():
    st["x"] = load_x1()
    st["h"] = _rmsnorm(st["x"], w["g_ffn"][...]).astype(jnp.bfloat16)

  def gate_up():
    st["gate"] = jnp.dot(st["h"], w["wg"][...], preferred_element_type=jnp.float32)
    st["up"] = jnp.dot(st["h"], w["wu"][...], preferred_element_type=jnp.float32)

  def activate():
    gate = st["gate"]
    st["act"] = (gate * jax.nn.sigmoid(gate) * st["up"]).astype(jnp.bfloat16)

  def down():
    y = st["x"] + _skewed_dot(st["act"], w["wd"], st["x"].shape[1])
    emit(_rmsnorm(y, w["g_final"][...]) if final_norm else y)

  return [norm, gate_up, activate, down]


def _even_mix_stages(load_x, w, emit, mixers, d_pool, d_conv):
  st = {}

  def norm():
    st["x"] = load_x()
    st["h"] = _rmsnorm(st["x"], w["g_mix"][...]).astype(jnp.bfloat16)

  def in_proj():
    st["z"] = _skewed_dot(st["h"], w["w_in"], d_pool + 3 * d_conv)

  def mix():
    z = st["z"]
    p = z[:, :d_pool]
    xb = z[:, d_pool:d_pool + d_conv]
    bg = z[:, d_pool + d_conv:d_pool + 2 * d_conv]
    cg = z[:, d_pool + 2 * d_conv:]
    pooled, conv_y = mixers(p, cg * xb)
    a_out = _pool_project(pooled - p, w["w_pool"], w["pool_scale"][...])
    st["mixed"] = jnp.concatenate([a_out, bg * conv_y], axis=-1).astype(jnp.bfloat16)

  def out_proj():
    emit(st["x"] + _skewed_dot(st["mixed"], w["w_out"], st["x"].shape[1]))

  return [norm, in_proj, mix, out_proj]


def _rows_back(x, k):
  return pltpu.roll(x, k, axis=0)


def _even_prompt_mixers(t, p_hist, c_hist, conv_w, tile):
  assert all(w & (w - 1) == 0 and w <= POOL_PAD for w in POOL_WINDOWS)

  def mixers(p, c):
    pos = t * tile + lax.broadcasted_iota(jnp.int32, (POOL_PAD, POOL_GROUP), 0)
    pooled = []
    for g, win in enumerate(POOL_WINDOWS):
      lo = g * POOL_GROUP
      sums = jnp.concatenate([p_hist[:, lo:lo + POOL_GROUP], p[:, lo:lo + POOL_GROUP]],
                             axis=0)
      span = 1
      while span < win:
        sums = sums + _rows_back(sums, span)
        span *= 2
      sums = sums[POOL_PAD:]
      head = sums[:POOL_PAD] / jnp.minimum(pos + 1, win).astype(jnp.float32)
      pooled.append(jnp.concatenate([head, sums[POOL_PAD:] * (1.0 / win)], axis=0))
    p_hist[...] = p[tile - POOL_PAD:, :]
    c_rows = jnp.concatenate([c_hist[...], c], axis=0)
    y = _rows_back(c_rows, 2)[CONV_PAD:] * conv_w[0:1, :]
    y = y + _rows_back(c_rows, 1)[CONV_PAD:] * conv_w[1:2, :]
    y = y + c * conv_w[2:3, :]
    c_hist[...] = c[tile - CONV_PAD:, :]
    return jnp.concatenate(pooled, axis=-1), y
  return mixers


def _pool_history_sums_kernel(pool_hist_ref, sums_ref):
  for g, win in enumerate(POOL_WINDOWS):
    lo = g * POOL_GROUP
    acc = pool_hist_ref[POOL_HIST - 1, :, lo:lo + POOL_GROUP]
    for k in range(2, win):
      acc = acc + pool_hist_ref[POOL_HIST - k, :, lo:lo + POOL_GROUP]
    sums_ref[:, lo:lo + POOL_GROUP] = acc


def _even_sample_mixers(pool_sums_ref, conv_hist_ref, conv_w, p_ref, c_ref, start):
  def mixers(p, c):
    p_ref[...] = p
    c_ref[...] = c
    pooled = []
    for g, win in enumerate(POOL_WINDOWS):
      lo = g * POOL_GROUP
      acc = p[:, lo:lo + POOL_GROUP] + pool_sums_ref[:, lo:lo + POOL_GROUP]
      pooled.append(acc / float(min(start + 1, win)))
    y = conv_hist_ref[0] * conv_w[0:1, :]
    y = y + conv_hist_ref[1] * conv_w[1:2, :]
    y = y + c * conv_w[2:3, :]
    return jnp.concatenate(pooled, axis=-1), y
  return mixers


def _even_mix_head(t, p_hist, c_hist):
  @pl.when(t == 0)
  def _():
    p_hist[...] = jnp.zeros(p_hist.shape, jnp.float32)
    c_hist[...] = jnp.zeros(c_hist.shape, jnp.float32)


def _even_mix_tail(t, pool_state_ref, conv_state_ref, p_hist, c_hist, *, last_t):
  @pl.when(t == last_t)
  def _():
    pool_state_ref[...] = p_hist[POOL_PAD - POOL_HIST:, :]
    conv_state_ref[...] = c_hist[CONV_PAD - CONV_HIST:, :]


EVEN_WEIGHT_NAMES = ("g_mix", "w_in", "w_pool", "pool_scale", "conv_w", "w_out",
                     "g_ffn", "wg", "wu", "wd")
ODD_WEIGHT_NAMES = ("g_mix", "w_in", "g_v", "w_s", "b_s_t", "w_out",
                    "g_ffn", "wg", "wu", "wd", "g_final")
N_CAST = 5


def _run(stages):
  for stage in stages:
    stage()


def _pipelined_steps(s, n_tiles, mix_stages, ffn_stages):
  @pl.when(s == 0)
  def _():
    _run(mix_stages())

  @pl.when((s > 0) & (s < n_tiles))
  def _():
    mix, ffn = mix_stages(), ffn_stages()
    _run([mix[0], mix[1], ffn[0], ffn[1], mix[2], ffn[2], ffn[3], mix[3]])

  @pl.when(s == n_tiles)
  def _():
    _run(ffn_stages())


def _set(ref):
  def emit(value):
    ref[...] = value
  return emit


def _even_kernel(*refs, n_tiles, tiles_per_seq, tile, start, d_pool, d_conv):
  x_ref, xs_ref, pool_sums_ref, conv_hist_ref = refs[:4]
  n_w = len(EVEN_WEIGHT_NAMES)
  w = dict(zip(EVEN_WEIGHT_NAMES, refs[4:4 + n_w]))
  cast_in = refs[4 + n_w:4 + n_w + N_CAST]
  outs = refs[4 + n_w + N_CAST:]
  y_ref, pool_state_ref, conv_state_ref, ys_ref, ps_ref, cs_ref = outs[:6]
  cast_out = outs[6:6 + N_CAST]
  p_hist, c_hist, x1_buf = outs[6 + N_CAST:]

  s = pl.program_id(0)
  t = s % tiles_per_seq
  slot = s % 2

  @pl.when(s < n_tiles)
  def _():
    _even_mix_head(t, p_hist, c_hist)
    for src, dst in zip(cast_in, cast_out):
      n = src.shape[1]
      dst[:, :n] = src[...].astype(dst.dtype)
      if dst.shape[1] > n:
        dst[:, n:] = jnp.zeros((dst.shape[0], dst.shape[1] - n), dst.dtype)

  def mix_stages():
    return _even_mix_stages(
        lambda: x_ref[...], w, _set(x1_buf.at[slot]),
        _even_prompt_mixers(t, p_hist, c_hist, w["conv_w"], tile), d_pool, d_conv)

  def ffn_stages():
    return _ffn_stages(lambda: x1_buf[1 - slot], w, _set(y_ref), False)

  _pipelined_steps(s, n_tiles, mix_stages, ffn_stages)

  @pl.when(s < n_tiles)
  def _():
    _even_mix_tail(t, pool_state_ref, conv_state_ref, p_hist, c_hist,
                   last_t=tiles_per_seq - 1)

  @pl.when(s == n_tiles + 1)
  def _():
    st = {}
    mixers = _even_sample_mixers(pool_sums_ref, conv_hist_ref, w["conv_w"],
                                 ps_ref, cs_ref, start)
    _run(_even_mix_stages(lambda: xs_ref[...], w, lambda v: st.update(x1=v),
                          mixers, d_pool, d_conv))
    _run(_ffn_stages(lambda: st["x1"], w, _set(ys_ref), False))


def _odd_mix_stages(load_x, w, emit, gating, d_gate):
  st = {}

  def norm():
    st["x"] = load_x()
    st["h"] = _rmsnorm(st["x"], w["g_mix"][...]).astype(jnp.bfloat16)

  def in_proj():
    st["z"] = _skewed_dot(st["h"], w["w_in"], 2 * d_gate)

  def gate():
    z = _gelu_exact(st["z"])
    v = _rmsnorm(z[:, d_gate:], w["g_v"][...])
    st["gated"] = (z[:, :d_gate] * gating(v)).astype(jnp.bfloat16)

  def out_proj():
    emit(st["x"] + _skewed_dot(st["gated"], w["w_out"], st["x"].shape[1]))

  return [norm, in_proj, gate, out_proj]


def _odd_prompt_gating(w, tile, d_gate):
  def gating(v):
    vb = v.astype(jnp.bfloat16)
    n_chunks = tile // CHUNK
    head = d_gate // N_SG_HEADS
    row = lax.broadcasted_iota(jnp.int32, (CHUNK, CHUNK), 0)
    col = lax.broadcasted_iota(jnp.int32, (CHUNK, CHUNK), 1)
    causal = row >= col
    per_head = []
    for hd in range(N_SG_HEADS):
      w_h = jnp.where(causal, w["w_s"][hd], 0.0).astype(jnp.bfloat16)
      rhs = jnp.concatenate(
          [vb[c * CHUNK:(c + 1) * CHUNK, hd * head:(hd + 1) * head]
           for c in range(n_chunks)], axis=1)
      o = jnp.dot(w_h, rhs, preferred_element_type=jnp.float32)
      per_head.append(o + w["b_s_t"][:, hd:hd + 1])
    return jnp.concatenate(
        [jnp.concatenate([o[:, c * head:(c + 1) * head] for o in per_head], axis=1)
         for c in range(n_chunks)], axis=0)
  return gating


def _odd_sample_gating(w, v_ref, d_gate):
  def gating(v):
    v_ref[...] = v
    head = d_gate // N_SG_HEADS
    rows = v.shape[0]
    vb = v.astype(jnp.bfloat16).astype(jnp.float32)
    per_head = []
    for hd in range(N_SG_HEADS):
      w00 = w["w_s"][hd, 0:1, 0:1].astype(jnp.bfloat16).astype(jnp.float32)
      b0 = w["b_s_t"][0:1, hd:hd + 1]
      per_head.append(vb[:, hd * head:(hd + 1) * head] * jnp.broadcast_to(w00, (rows, head))
                      + jnp.broadcast_to(b0, (rows, head)))
    return jnp.concatenate(per_head, axis=1)
  return gating


def _odd_kernel(*refs, n_tiles, tile, d_gate):
  x_ref, xs_ref = refs[:2]
  n_w = len(ODD_WEIGHT_NAMES)
  w = dict(zip(ODD_WEIGHT_NAMES, refs[2:2 + n_w]))
  y_ref, ys_ref, vs_ref, x1_buf = refs[2 + n_w:]

  s = pl.program_id(0)
  slot = s % 2

  def mix_stages():
    return _odd_mix_stages(lambda: x_ref[...], w, _set(x1_buf.at[slot]),
                           _odd_prompt_gating(w, tile, d_gate), d_gate)

  def ffn_stages():
    return _ffn_stages(lambda: x1_buf[1 - slot], w, _set(y_ref), True)

  _pipelined_steps(s, n_tiles, mix_stages, ffn_stages)

  @pl.when(s == n_tiles + 1)
  def _():
    st = {}
    _run(_odd_mix_stages(lambda: xs_ref[...], w, lambda v: st.update(x1=v),
                         _odd_sample_gating(w, vs_ref, d_gate), d_gate))
    _run(_ffn_stages(lambda: st["x1"], w, _set(ys_ref), True))


def _resident(arr):
  zeros = (0,) * arr.ndim
  return pl.BlockSpec(arr.shape, lambda s: zeros, pipeline_mode=pl.Buffered(1))


def _whole_out(shape):
  zeros = (0,) * len(shape)
  return pl.BlockSpec(shape, lambda s: zeros)


def _params():
  return pltpu.CompilerParams(dimension_semantics=("arbitrary",),
                              vmem_limit_bytes=V7X_VMEM_LIMIT_BYTES)


def _prompt_specs(x, tile):
  batch, seq, d_model = x.shape
  tiles_per_seq = seq // tile
  n_tiles = batch * tiles_per_seq
  mix_tile = lambda s: jnp.clip(s, 0, n_tiles - 1)
  ffn_tile = lambda s: jnp.clip(s - 1, 0, n_tiles - 1)
  block = lambda which: pl.BlockSpec(
      (None, tile, d_model),
      lambda s: (which(s) // tiles_per_seq, which(s) % tiles_per_seq, 0))
  per_seq = lambda rows, ch: pl.BlockSpec(
      (None, rows, ch), lambda s: (mix_tile(s) // tiles_per_seq, 0, 0))
  return n_tiles, tiles_per_seq, mix_tile, block(mix_tile), block(ffn_tile), per_seq


def _cast_slab_spec(rows, cols, n_steps, step, layer):
  slab = rows // n_steps
  steps_per_slab = 1
  while slab % V7X_BF16_SUBLANES:
    slab *= 2
    steps_per_slab *= 2
  src = pl.BlockSpec((None, slab, cols), lambda s: (layer, step(s) // steps_per_slab, 0))
  dst = pl.BlockSpec((slab, cols + _skew_cols(cols)),
                     lambda s: (step(s) // steps_per_slab, 0))
  return src, dst


def _pool_history_sums(pool_hist):
  rows, d_pool = pool_hist.shape[1:]
  return pl.pallas_call(
      _pool_history_sums_kernel,
      grid=(1,),
      in_specs=[pl.BlockSpec(pool_hist.shape, lambda i: (0, 0, 0))],
      out_specs=pl.BlockSpec((rows, d_pool), lambda i: (0, 0)),
      out_shape=jax.ShapeDtypeStruct((rows, d_pool), pool_hist.dtype),
      name="pool_history_sums",
  )(pool_hist)


def _even_layer(x, xs, pool_sums, conv_hist, weights, next_mats, next_layer, tile, start):
  n_tiles, tiles_per_seq, step, tok_in, tok_out, per_seq = _prompt_specs(x, tile)
  batch = x.shape[0]
  rows, d_model = xs.shape
  d_pool = pool_sums.shape[-1]
  d_conv = conv_hist.shape[-1]
  cast_specs = [_cast_slab_spec(m.shape[1], m.shape[2], n_tiles, step, next_layer[i])
                for i, m in enumerate(next_mats)]
  f32 = x.dtype
  return pl.pallas_call(
      functools.partial(_even_kernel, n_tiles=n_tiles,
                        tiles_per_seq=tiles_per_seq, tile=tile, start=start,
                        d_pool=d_pool, d_conv=d_conv),
      grid=(n_tiles + 2,),
      in_specs=([tok_in, _resident(xs), _resident(pool_sums), _resident(conv_hist)]
                + [_resident(w) for w in weights] + [c[0] for c in cast_specs]),
      out_specs=([tok_out, per_seq(POOL_HIST, d_pool), per_seq(CONV_HIST, d_conv),
                  _whole_out(xs.shape), _whole_out((rows, d_pool)),
                  _whole_out((rows, d_conv))] + [c[1] for c in cast_specs]),
      out_shape=([jax.ShapeDtypeStruct(x.shape, f32),
                  jax.ShapeDtypeStruct((batch, POOL_HIST, d_pool), f32),
                  jax.ShapeDtypeStruct((batch, CONV_HIST, d_conv), f32),
                  jax.ShapeDtypeStruct(xs.shape, f32),
                  jax.ShapeDtypeStruct((rows, d_pool), f32),
                  jax.ShapeDtypeStruct((rows, d_conv), f32)]
                 + [jax.ShapeDtypeStruct((m.shape[1], m.shape[2] + _skew_cols(m.shape[2])),
                                         jnp.bfloat16) for m in next_mats]),
      scratch_shapes=[pltpu.VMEM((POOL_PAD, d_pool), jnp.float32),
                      pltpu.VMEM((CONV_PAD, d_conv), jnp.float32),
                      pltpu.VMEM((2, tile, d_model), jnp.float32)],
      compiler_params=_params(),
      name="even_layer",
  )(x, xs, pool_sums, conv_hist, *weights, *next_mats)


def _odd_layer(x, xs, weights, tile):
  n_tiles, _, _, tok_in, tok_out, _ = _prompt_specs(x, tile)
  rows, d_model = xs.shape
  d_gate = weights[ODD_WEIGHT_NAMES.index("g_v")].shape[-1]
  return pl.pallas_call(
      functools.partial(_odd_kernel, n_tiles=n_tiles, tile=tile, d_gate=d_gate),
      grid=(n_tiles + 2,),
      in_specs=[tok_in, _resident(xs)] + [_resident(w) for w in weights],
      out_specs=[tok_out, _whole_out(xs.shape), _whole_out((rows, d_gate))],
      out_shape=[jax.ShapeDtypeStruct(x.shape, x.dtype),
                 jax.ShapeDtypeStruct(xs.shape, x.dtype),
                 jax.ShapeDtypeStruct((rows, d_gate), x.dtype)],
      scratch_shapes=[pltpu.VMEM((2, tile, d_model), jnp.float32)],
      compiler_params=_params(),
      name="odd_layer",
  )(x, xs, *weights)


def _pool_block_diag(w_pool):
  per_tile = V7X_MXU_DIM // POOL_GROUP
  n_tiles = w_pool.shape[0] // per_tile
  out = jnp.zeros((n_tiles, V7X_MXU_DIM, V7X_MXU_DIM), jnp.bfloat16)
  for g in range(w_pool.shape[0]):
    i, j = divmod(g, per_tile)
    lo = j * POOL_GROUP
    out = out.at[i, lo:lo + POOL_GROUP, lo:lo + POOL_GROUP].set(
        w_pool[g].astype(jnp.bfloat16))
  return out


def kernel(x_prompt, x_sample, state_pool, state_conv, norm_mix, norm_ffn, norm_final, w_in_even, w_pool, pool_scale, conv_w, w_out_even, w_in_odd, norm_sg, w_s, b_s, w_out_odd, ffn_w_gate, ffn_w_up, ffn_w_down):
  depth = norm_mix.shape[0]
  assert depth == 2 and w_in_even.shape[0] == 1 and w_in_odd.shape[0] == 1
  assert x_sample.shape[1] == 1
  assert x_prompt.shape[1] % PROMPT_TILE == 0 and PROMPT_TILE % CHUNK == 0

  def bf(w):
    return jnp.pad(w.astype(jnp.bfloat16), ((0, 0), (0, _skew_cols(w.shape[1]))))

  row = lambda v: v.reshape(1, -1)

  even_w = (row(norm_mix[0]), bf(w_in_even[0]), _pool_block_diag(w_pool[0]),
            row(pool_scale[0]), conv_w[0], bf(w_out_even[0]), row(norm_ffn[0]),
            bf(ffn_w_gate[0]), bf(ffn_w_up[0]), bf(ffn_w_down[0]))
  odd_mats = (w_in_odd, w_out_odd, ffn_w_gate, ffn_w_up, ffn_w_down)
  odd_mat_layer = (0, 0, 1, 1, 1)

  pool_hist = jnp.swapaxes(state_pool[0], 0, 1)
  conv_hist = jnp.swapaxes(state_conv[0], 0, 1)

  (x1_p, pool_p, conv_p, x1_s, p_s, c_s, w_in_o, w_out_o, wg_o, wu_o,
   wd_o) = _even_layer(x_prompt, x_sample[:, 0, :], _pool_history_sums(pool_hist),
                       conv_hist, even_w,
                       odd_mats, odd_mat_layer, PROMPT_TILE, PAST_LEN)

  odd_w = (row(norm_mix[1]), w_in_o, row(norm_sg[0]), w_s[0], b_s[0].T, w_out_o,
           row(norm_ffn[1]), wg_o, wu_o, wd_o, row(norm_final))
  y_p, y_s, v_s = _odd_layer(x1_p, x1_s, odd_w, PROMPT_TILE)

  pool_s = jnp.concatenate([state_pool[:, :, 1:], p_s[None, :, None, :]], axis=2)
  conv_s = jnp.concatenate([state_conv[:, :, 1:], c_s[None, :, None, :]], axis=2)

  return (y_p, y_s[:, None, :], pool_p[None], pool_s, conv_p[None], conv_s,
          v_s[None, :, None, :])
```

```python
import functools
import math

import jax
import jax.numpy as jnp
from jax import lax
from jax.experimental import pallas as pl
from jax.experimental.pallas import tpu as pltpu

POOL_WINDOWS = (2, 4, 8, 16)
POOL_GROUP = 128
POOL_HIST = max(POOL_WINDOWS) - 1
CONV_WIDTH = 3
CONV_HIST = CONV_WIDTH - 1
CHUNK = 128
N_SG_HEADS = 8
EPS = 1e-6
PAST_LEN = 16384

V7X_SUBLANES = 8
V7X_LANES = 128
V7X_BF16_SUBLANES = 16
V7X_MXU_DIM = 256
V7X_VMEM_LIMIT_BYTES = 60000 * 1024

POOL_PAD = -(-POOL_HIST // V7X_SUBLANES) * V7X_SUBLANES
CONV_PAD = -(-CONV_HIST // V7X_SUBLANES) * V7X_SUBLANES

PROMPT_TILE = 512
FETCH_ROWS = 128


def _bf16_dot(a, b):
  return jnp.dot(a.astype(jnp.bfloat16), b, preferred_element_type=jnp.float32)


def _rmsnorm(x, g):
  y = x * lax.rsqrt(jnp.mean(x * x, axis=-1, keepdims=True) + EPS)
  return y * g


def _gelu_exact(x):
  return 0.5 * x * (1.0 + lax.erf(x * math.sqrt(0.5)))


def _skew_cols(n):
  return V7X_LANES if (n // V7X_LANES) % V7X_SUBLANES == 0 else 0


def _skewed_dot(a, w_ref, n):
  return jnp.dot(a, w_ref[:, :n], preferred_element_type=jnp.float32)


def _pool_project(d, w_bd_ref, scale):
  halves = []
  for i in range(w_bd_ref.shape[0]):
    lo = i * V7X_MXU_DIM
    halves.append(_bf16_dot(d[:, lo:lo + V7X_MXU_DIM], w_bd_ref[i]))
  return jnp.concatenate(halves, axis=-1) * scale


def _ffn_stages(load_x1, w, emit, final_norm):
  st = {}

  def norm():
    st["x"] = load_x1()
    st["h"] = _rmsnorm(st["x"], w["g_ffn"][...]).astype(jnp.bfloat16)

  def gate_up():
    st["gate"] = jnp.dot(st["h"], w["wg"][...], preferred_element_type=jnp.float32)
    st["up"] = jnp.dot(st["h"], w["wu"][...], preferred_element_type=jnp.float32)

  def activate():
    gate = st["gate"]
    st["act"] = (gate * jax.nn.sigmoid(gate) * st["up"]).astype(jnp.bfloat16)

  def down():
    y = st["x"] + _skewed_dot(st["act"], w["wd"], st["x"].shape[1])
    emit(_rmsnorm(y, w["g_final"][...]) if final_norm else y)

  return [norm, gate_up, activate, down]


def _even_mix_stages(load_x, w, emit, mixers, d_pool, d_conv):
  st = {}

  def norm():
    st["x"] = load_x()
    st["h"] = _rmsnorm(st["x"], w["g_mix"][...]).astype(jnp.bfloat16)

  def in_proj():
    st["z"] = _skewed_dot(st["h"], w["w_in"], d_pool + 3 * d_conv)

  def mix():
    z = st["z"]
    p = z[:, :d_pool]
    xb = z[:, d_pool:d_pool + d_conv]
    bg = z[:, d_pool + d_conv:d_pool + 2 * d_conv]
    cg = z[:, d_pool + 2 * d_conv:]
    pooled, conv_y = mixers(p, cg * xb)
    st["pool_d"] = (pooled - p).astype(jnp.bfloat16)
    st["conv_out"] = (bg * conv_y).astype(jnp.bfloat16)

  def project():
    a_out = _pool_project(st["pool_d"], w["w_pool"], w["pool_scale"][...])
    st["mixed"] = jnp.concatenate([a_out.astype(jnp.bfloat16), st["conv_out"]], axis=-1)

  def out_proj():
    emit(st["x"] + _skewed_dot(st["mixed"], w["w_out"], st["x"].shape[1]))

  return [norm, in_proj, mix, project, out_proj]


def _rows_back(x, k):
  return pltpu.roll(x, k, axis=0)


def _even_prompt_mixers(t, p_hist, c_hist, conv_w, tile):
  assert all(w & (w - 1) == 0 and w <= POOL_PAD for w in POOL_WINDOWS)

  def mixers(p, c):
    pos = t * tile + lax.broadcasted_iota(jnp.int32, (POOL_PAD, POOL_GROUP), 0)
    pooled = []
    for g, win in enumerate(POOL_WINDOWS):
      lo = g * POOL_GROUP
      sums = jnp.concatenate([p_hist[:, lo:lo + POOL_GROUP], p[:, lo:lo + POOL_GROUP]],
                             axis=0)
      span = 1
      while span < win:
        sums = sums + _rows_back(sums, span)
        span *= 2
      sums = sums[POOL_PAD:]
      head = sums[:POOL_PAD] / jnp.minimum(pos + 1, win).astype(jnp.float32)
      pooled.append(jnp.concatenate([head, sums[POOL_PAD:] * (1.0 / win)], axis=0))
    p_hist[...] = p[tile - POOL_PAD:, :]
    c_rows = jnp.concatenate([c_hist[...], c], axis=0)
    y = _rows_back(c_rows, 2)[CONV_PAD:] * conv_w[0:1, :]
    y = y + _rows_back(c_rows, 1)[CONV_PAD:] * conv_w[1:2, :]
    y = y + c * conv_w[2:3, :]
    c_hist[...] = c[tile - CONV_PAD:, :]
    return jnp.concatenate(pooled, axis=-1), y
  return mixers


def _pool_history_sums_kernel(pool_hist_ref, sums_ref):
  for g, win in enumerate(POOL_WINDOWS):
    lo = g * POOL_GROUP
    acc = pool_hist_ref[POOL_HIST - 1, :, lo:lo + POOL_GROUP]
    for k in range(2, win):
      acc = acc + pool_hist_ref[POOL_HIST - k, :, lo:lo + POOL_GROUP]
    sums_ref[:, lo:lo + POOL_GROUP] = acc


def _even_sample_mixers(pool_sums_ref, conv_hist_ref, conv_w, p_ref, c_ref, start):
  def mixers(p, c):
    p_ref[...] = p
    c_ref[...] = c
    pooled = []
    for g, win in enumerate(POOL_WINDOWS):
      lo = g * POOL_GROUP
      acc = p[:, lo:lo + POOL_GROUP] + pool_sums_ref[:, lo:lo + POOL_GROUP]
      pooled.append(acc / float(min(start + 1, win)))
    y = conv_hist_ref[0] * conv_w[0:1, :]
    y = y + conv_hist_ref[1] * conv_w[1:2, :]
    y = y + c * conv_w[2:3, :]
    return jnp.concatenate(pooled, axis=-1), y
  return mixers


def _even_mix_head(t, p_hist, c_hist):
  @pl.when(t == 0)
  def _():
    p_hist[...] = jnp.zeros(p_hist.shape, jnp.float32)
    c_hist[...] = jnp.zeros(c_hist.shape, jnp.float32)


def _even_mix_tail(t, pool_state_ref, conv_state_ref, p_hist, c_hist, *, last_t):
  @pl.when(t == last_t)
  def _():
    pool_state_ref[...] = p_hist[POOL_PAD - POOL_HIST:, :]
    conv_state_ref[...] = c_hist[CONV_PAD - CONV_HIST:, :]


EVEN_SMALL_WEIGHT_NAMES = ("g_mix", "w_pool", "pool_scale", "conv_w", "g_ffn")
EVEN_BIG_WEIGHT_NAMES = ("w_in", "w_out", "wg", "wu", "wd")
ODD_WEIGHT_NAMES = ("g_mix", "w_in", "g_v", "w_s", "b_s_t", "w_out",
                    "g_ffn", "wg", "wu", "wd", "g_final")
N_CAST = 5


def _fetch_bf16(srcs, dsts, stage, sems):
  chunks = []
  for src, dst in zip(srcs, dsts):
    n_rows, n_cols = src.shape
    if dst.shape[1] > n_cols:
      dst[:, n_cols:] = jnp.zeros((n_rows, dst.shape[1] - n_cols), dst.dtype)
    for r0 in range(0, n_rows, FETCH_ROWS):
      chunks.append((src, dst, r0, min(FETCH_ROWS, n_rows - r0), n_cols))

  def copy(i):
    src, _, r0, n, n_cols = chunks[i]
    return pltpu.make_async_copy(src.at[pl.ds(r0, n), :],
                                 stage.at[i % 2, pl.ds(0, n), pl.ds(0, n_cols)],
                                 sems.at[i % 2])

  copy(0).start()
  for i, (_, dst, r0, n, n_cols) in enumerate(chunks):
    if i + 1 < len(chunks):
      copy(i + 1).start()
    copy(i).wait()
    dst[pl.ds(r0, n), :n_cols] = stage[i % 2, :n, :n_cols].astype(dst.dtype)


def _run(stages):
  for stage in stages:
    stage()


def _pipelined_steps(s, n_tiles, mix_stages, ffn_stages):
  @pl.when(s == 0)
  def _():
    _run(mix_stages())

  @pl.when((s > 0) & (s < n_tiles))
  def _():
    mix, ffn = mix_stages(), ffn_stages()
    _run([mix[0], mix[1], ffn[0], mix[2], ffn[1], mix[3], ffn[2], ffn[3], mix[4]])

  @pl.when(s == n_tiles)
  def _():
    _run(ffn_stages())


def _set(ref):
  def emit(value):
    ref[...] = value
  return emit


def _even_kernel(*refs, n_tiles, tiles_per_seq, tile, start, d_pool, d_conv, layer):
  x_ref, xs_ref, pool_sums_ref, conv_hist_ref = refs[:4]
  n_small, n_big = len(EVEN_SMALL_WEIGHT_NAMES), len(EVEN_BIG_WEIGHT_NAMES)
  w = dict(zip(EVEN_SMALL_WEIGHT_NAMES, refs[4:4 + n_small]))
  big_hbm = refs[4 + n_small:4 + n_small + n_big]
  cast_in = refs[4 + n_small + n_big:4 + n_small + n_big + N_CAST]
  outs = refs[4 + n_small + n_big + N_CAST:]
  y_ref, pool_state_ref, conv_state_ref, ys_ref, ps_ref, cs_ref = outs[:6]
  cast_out = outs[6:6 + N_CAST]
  scratch = outs[6 + N_CAST:]
  p_hist, c_hist, x1_buf = scratch[:3]
  big_vmem = scratch[3:3 + n_big]
  stage, sems = scratch[3 + n_big:]
  w.update(zip(EVEN_BIG_WEIGHT_NAMES, big_vmem))

  s = pl.program_id(0)
  t = s % tiles_per_seq
  slot = s % 2

  @pl.when(s == 0)
  def _():
    _fetch_bf16([m.at[layer[i]] for i, m in enumerate(big_hbm)], big_vmem, stage, sems)

  @pl.when(s < n_tiles)
  def _():
    _even_mix_head(t, p_hist, c_hist)
    for src, dst in zip(cast_in, cast_out):
      n = src.shape[1]
      dst[:, :n] = src[...].astype(dst.dtype)
      if dst.shape[1] > n:
        dst[:, n:] = jnp.zeros((dst.shape[0], dst.shape[1] - n), dst.dtype)

  def mix_stages():
    return _even_mix_stages(
        lambda: x_ref[...], w, _set(x1_buf.at[slot]),
        _even_prompt_mixers(t, p_hist, c_hist, w["conv_w"], tile), d_pool, d_conv)

  def ffn_stages():
    return _ffn_stages(lambda: x1_buf[1 - slot], w, _set(y_ref), False)

  _pipelined_steps(s, n_tiles, mix_stages, ffn_stages)

  @pl.when(s < n_tiles)
  def _():
    _even_mix_tail(t, pool_state_ref, conv_state_ref, p_hist, c_hist,
                   last_t=tiles_per_seq - 1)

  @pl.when(s == n_tiles + 1)
  def _():
    st = {}
    mixers = _even_sample_mixers(pool_sums_ref, conv_hist_ref, w["conv_w"],
                                 ps_ref, cs_ref, start)
    _run(_even_mix_stages(lambda: xs_ref[...], w, lambda v: st.update(x1=v),
                          mixers, d_pool, d_conv))
    _run(_ffn_stages(lambda: st["x1"], w, _set(ys_ref), False))


def _odd_mix_stages(load_x, w, emit, gating, d_gate):
  st = {}

  def norm():
    st["x"] = load_x()
    st["h"] = _rmsnorm(st["x"], w["g_mix"][...]).astype(jnp.bfloat16)

  def in_proj():
    st["z"] = _skewed_dot(st["h"], w["w_in"], 2 * d_gate)

  def activate():
    z = _gelu_exact(st["z"])
    st["u"] = z[:, :d_gate]
    st["v"] = _rmsnorm(z[:, d_gate:], w["g_v"][...])

  def gate():
    st["gated"] = (st["u"] * gating(st["v"])).astype(jnp.bfloat16)

  def out_proj():
    emit(st["x"] + _skewed_dot(st["gated"], w["w_out"], st["x"].shape[1]))

  return [norm, in_proj, activate, gate, out_proj]


def _odd_prompt_gating(w, tile, d_gate):
  def gating(v):
    vb = v.astype(jnp.bfloat16)
    n_chunks = tile // CHUNK
    head = d_gate // N_SG_HEADS
    row = lax.broadcasted_iota(jnp.int32, (CHUNK, CHUNK), 0)
    col = lax.broadcasted_iota(jnp.int32, (CHUNK, CHUNK), 1)
    causal = row >= col
    per_head = []
    for hd in range(N_SG_HEADS):
      w_h = jnp.where(causal, w["w_s"][hd], 0.0).astype(jnp.bfloat16)
      rhs = jnp.concatenate(
          [vb[c * CHUNK:(c + 1) * CHUNK, hd * head:(hd + 1) * head]
           for c in range(n_chunks)], axis=1)
      o = jnp.dot(w_h, rhs, preferred_element_type=jnp.float32)
      per_head.append(o + w["b_s_t"][:, hd:hd + 1])
    return jnp.concatenate(
        [jnp.concatenate([o[:, c * head:(c + 1) * head] for o in per_head], axis=1)
         for c in range(n_chunks)], axis=0)
  return gating


def _odd_sample_gating(w, v_ref, d_gate):
  def gating(v):
    v_ref[...] = v
    head = d_gate // N_SG_HEADS
    rows = v.shape[0]
    vb = v.astype(jnp.bfloat16).astype(jnp.float32)
    per_head = []
    for hd in range(N_SG_HEADS):
      w00 = w["w_s"][hd, 0:1, 0:1].astype(jnp.bfloat16).astype(jnp.float32)
      b0 = w["b_s_t"][0:1, hd:hd + 1]
      per_head.append(vb[:, hd * head:(hd + 1) * head] * jnp.broadcast_to(w00, (rows, head))
                      + jnp.broadcast_to(b0, (rows, head)))
    return jnp.concatenate(per_head, axis=1)
  return gating


def _odd_kernel(*refs, n_tiles, tile, d_gate):
  x_ref, xs_ref = refs[:2]
  n_w = len(ODD_WEIGHT_NAMES)
  w = dict(zip(ODD_WEIGHT_NAMES, refs[2:2 + n_w]))
  y_ref, ys_ref, vs_ref, x1_buf = refs[2 + n_w:]

  s = pl.program_id(0)
  slot = s % 2

  def mix_stages():
    return _odd_mix_stages(lambda: x_ref[...], w, _set(x1_buf.at[slot]),
                           _odd_prompt_gating(w, tile, d_gate), d_gate)

  def ffn_stages():
    return _ffn_stages(lambda: x1_buf[1 - slot], w, _set(y_ref), True)

  _pipelined_steps(s, n_tiles, mix_stages, ffn_stages)

  @pl.when(s == n_tiles + 1)
  def _():
    st = {}
    _run(_odd_mix_stages(lambda: xs_ref[...], w, lambda v: st.update(x1=v),
                         _odd_sample_gating(w, vs_ref, d_gate), d_gate))
    _run(_ffn_stages(lambda: st["x1"], w, _set(ys_ref), True))


def _resident(arr):
  zeros = (0,) * arr.ndim
  return pl.BlockSpec(arr.shape, lambda s: zeros, pipeline_mode=pl.Buffered(1))


def _whole_out(shape):
  zeros = (0,) * len(shape)
  return pl.BlockSpec(shape, lambda s: zeros)


def _params():
  return pltpu.CompilerParams(dimension_semantics=("arbitrary",),
                              vmem_limit_bytes=V7X_VMEM_LIMIT_BYTES)


def _prompt_specs(x, tile):
  batch, seq, d_model = x.shape
  tiles_per_seq = seq // tile
  n_tiles = batch * tiles_per_seq
  mix_tile = lambda s: jnp.clip(s, 0, n_tiles - 1)
  ffn_tile = lambda s: jnp.clip(s - 1, 0, n_tiles - 1)
  block = lambda which: pl.BlockSpec(
      (None, tile, d_model),
      lambda s: (which(s) // tiles_per_seq, which(s) % tiles_per_seq, 0))
  per_seq = lambda rows, ch: pl.BlockSpec(
      (None, rows, ch), lambda s: (mix_tile(s) // tiles_per_seq, 0, 0))
  return n_tiles, tiles_per_seq, mix_tile, block(mix_tile), block(ffn_tile), per_seq


def _cast_slab_spec(rows, cols, n_steps, step, layer):
  slab = rows // n_steps
  steps_per_slab = 1
  while slab % V7X_BF16_SUBLANES:
    slab *= 2
    steps_per_slab *= 2
  src = pl.BlockSpec((None, slab, cols), lambda s: (layer, step(s) // steps_per_slab, 0))
  dst = pl.BlockSpec((slab, cols + _skew_cols(cols)),
                     lambda s: (step(s) // steps_per_slab, 0))
  return src, dst


def _pool_history_sums(pool_hist):
  rows, d_pool = pool_hist.shape[1:]
  return pl.pallas_call(
      _pool_history_sums_kernel,
      grid=(1,),
      in_specs=[pl.BlockSpec(pool_hist.shape, lambda i: (0, 0, 0))],
      out_specs=pl.BlockSpec((rows, d_pool), lambda i: (0, 0)),
      out_shape=jax.ShapeDtypeStruct((rows, d_pool), pool_hist.dtype),
      name="pool_history_sums",
  )(pool_hist)


def _even_layer(x, xs, pool_sums, conv_hist, small_weights, mats, mat_layer,
                next_mats, next_layer, tile, start):
  n_tiles, tiles_per_seq, step, tok_in, tok_out, per_seq = _prompt_specs(x, tile)
  batch = x.shape[0]
  rows, d_model = xs.shape
  d_pool = pool_sums.shape[-1]
  d_conv = conv_hist.shape[-1]
  cast_specs = [_cast_slab_spec(m.shape[1], m.shape[2], n_tiles, step, next_layer[i])
                for i, m in enumerate(next_mats)]
  f32 = x.dtype
  skewed = lambda m: (m.shape[1], m.shape[2] + _skew_cols(m.shape[2]))
  widest = max(m.shape[2] for m in mats)
  return pl.pallas_call(
      functools.partial(_even_kernel, n_tiles=n_tiles,
                        tiles_per_seq=tiles_per_seq, tile=tile, start=start,
                        d_pool=d_pool, d_conv=d_conv, layer=tuple(mat_layer)),
      grid=(n_tiles + 2,),
      in_specs=([tok_in, _resident(xs), _resident(pool_sums), _resident(conv_hist)]
                + [_resident(w) for w in small_weights]
                + [pl.BlockSpec(memory_space=pl.ANY) for _ in mats]
                + [c[0] for c in cast_specs]),
      out_specs=([tok_out, per_seq(POOL_HIST, d_pool), per_seq(CONV_HIST, d_conv),
                  _whole_out(xs.shape), _whole_out((rows, d_pool)),
                  _whole_out((rows, d_conv))] + [c[1] for c in cast_specs]),
      out_shape=([jax.ShapeDtypeStruct(x.shape, f32),
                  jax.ShapeDtypeStruct((batch, POOL_HIST, d_pool), f32),
                  jax.ShapeDtypeStruct((batch, CONV_HIST, d_conv), f32),
                  jax.ShapeDtypeStruct(xs.shape, f32),
                  jax.ShapeDtypeStruct((rows, d_pool), f32),
                  jax.ShapeDtypeStruct((rows, d_conv), f32)]
                 + [jax.ShapeDtypeStruct(skewed(m), jnp.bfloat16) for m in next_mats]),
      scratch_shapes=([pltpu.VMEM((POOL_PAD, d_pool), jnp.float32),
                       pltpu.VMEM((CONV_PAD, d_conv), jnp.float32),
                       pltpu.VMEM((2, tile, d_model), jnp.float32)]
                      + [pltpu.VMEM(skewed(m), jnp.bfloat16) for m in mats]
                      + [pltpu.VMEM((2, FETCH_ROWS, widest), jnp.float32),
                         pltpu.SemaphoreType.DMA((2,))]),
      compiler_params=_params(),
      name="even_layer",
  )(x, xs, pool_sums, conv_hist, *small_weights, *mats, *next_mats)


def _odd_layer(x, xs, weights, tile):
  n_tiles, _, _, tok_in, tok_out, _ = _prompt_specs(x, tile)
  rows, d_model = xs.shape
  d_gate = weights[ODD_WEIGHT_NAMES.index("g_v")].shape[-1]
  return pl.pallas_call(
      functools.partial(_odd_kernel, n_tiles=n_tiles, tile=tile, d_gate=d_gate),
      grid=(n_tiles + 2,),
      in_specs=[tok_in, _resident(xs)] + [_resident(w) for w in weights],
      out_specs=[tok_out, _whole_out(xs.shape), _whole_out((rows, d_gate))],
      out_shape=[jax.ShapeDtypeStruct(x.shape, x.dtype),
                 jax.ShapeDtypeStruct(xs.shape, x.dtype),
                 jax.ShapeDtypeStruct((rows, d_gate), x.dtype)],
      scratch_shapes=[pltpu.VMEM((2, tile, d_model), jnp.float32)],
      compiler_params=_params(),
      name="odd_layer",
  )(x, xs, *weights)


def _pool_block_diag(w_pool):
  per_tile = V7X_MXU_DIM // POOL_GROUP
  n_tiles = w_pool.shape[0] // per_tile
  out = jnp.zeros((n_tiles, V7X_MXU_DIM, V7X_MXU_DIM), jnp.bfloat16)
  for g in range(w_pool.shape[0]):
    i, j = divmod(g, per_tile)
    lo = j * POOL_GROUP
    out = out.at[i, lo:lo + POOL_GROUP, lo:lo + POOL_GROUP].set(
        w_pool[g].astype(jnp.bfloat16))
  return out


def kernel(x_prompt, x_sample, state_pool, state_conv, norm_mix, norm_ffn, norm_final, w_in_even, w_pool, pool_scale, conv_w, w_out_even, w_in_odd, norm_sg, w_s, b_s, w_out_odd, ffn_w_gate, ffn_w_up, ffn_w_down):
  depth = norm_mix.shape[0]
  assert depth == 2 and w_in_even.shape[0] == 1 and w_in_odd.shape[0] == 1
  assert x_sample.shape[1] == 1
  assert x_prompt.shape[1] % PROMPT_TILE == 0 and PROMPT_TILE % CHUNK == 0

  row = lambda v: v.reshape(1, -1)

  even_small = (row(norm_mix[0]), _pool_block_diag(w_pool[0]), row(pool_scale[0]),
                conv_w[0], row(norm_ffn[0]))
  even_mats = (w_in_even, w_out_even, ffn_w_gate, ffn_w_up, ffn_w_down)
  even_mat_layer = (0, 0, 0, 0, 0)
  odd_mats = (w_in_odd, w_out_odd, ffn_w_gate, ffn_w_up, ffn_w_down)
  odd_mat_layer = (0, 0, 1, 1, 1)

  pool_hist = jnp.swapaxes(state_pool[0], 0, 1)
  conv_hist = jnp.swapaxes(state_conv[0], 0, 1)

  (x1_p, pool_p, conv_p, x1_s, p_s, c_s, w_in_o, w_out_o, wg_o, wu_o,
   wd_o) = _even_layer(x_prompt, x_sample[:, 0, :], _pool_history_sums(pool_hist),
                       conv_hist, even_small, even_mats, even_mat_layer,
                       odd_mats, odd_mat_layer, PROMPT_TILE, PAST_LEN)

  odd_w = (row(norm_mix[1]), w_in_o, row(norm_sg[0]), w_s[0], b_s[0].T, w_out_o,
           row(norm_ffn[1]), wg_o, wu_o, wd_o, row(norm_final))
  y_p, y_s, v_s = _odd_layer(x1_p, x1_s, odd_w, PROMPT_TILE)

  pool_s = jnp.concatenate([state_pool[:, :, 1:], p_s[None, :, None, :]], axis=2)
  conv_s = jnp.concatenate([state_conv[:, :, 1:], c_s[None, :, None, :]], axis=2)

  return (y_p, y_s[:, None, :], pool_p[None], pool_s, conv_p[None], conv_s,
          v_s[None, :, None, :])
```

```python
import functools
import math

import jax
import jax.numpy as jnp
from jax import lax
from jax.experimental import pallas as pl
from jax.experimental.pallas import tpu as pltpu

POOL_WINDOWS = (2, 4, 8, 16)
POOL_GROUP = 128
POOL_HIST = max(POOL_WINDOWS) - 1
CONV_WIDTH = 3
CONV_HIST = CONV_WIDTH - 1
CHUNK = 128
N_SG_HEADS = 8
EPS = 1e-6
PAST_LEN = 16384

V7X_SUBLANES = 8
V7X_LANES = 128
V7X_BF16_SUBLANES = 16
V7X_MXU_DIM = 256
V7X_VMEM_LIMIT_BYTES = 60000 * 1024

POOL_PAD = -(-POOL_HIST // V7X_SUBLANES) * V7X_SUBLANES
CONV_PAD = -(-CONV_HIST // V7X_SUBLANES) * V7X_SUBLANES

PROMPT_TILE = 512
FETCH_ROWS = 128
FETCH_SLOTS = 4


def _bf16_dot(a, b):
  return jnp.dot(a.astype(jnp.bfloat16), b, preferred_element_type=jnp.float32)


def _rmsnorm(x, g):
  y = x * lax.rsqrt(jnp.mean(x * x, axis=-1, keepdims=True) + EPS)
  return y * g


def _gelu_exact(x):
  return 0.5 * x * (1.0 + lax.erf(x * math.sqrt(0.5)))


def _skew_cols(n):
  return V7X_LANES if (n // V7X_LANES) % V7X_SUBLANES == 0 else 0


def _skewed_dot(a, w_ref, n):
  return jnp.dot(a, w_ref[:, :n], preferred_element_type=jnp.float32)


def _pool_project(d, w_bd_ref, scale):
  halves = []
  for i in range(w_bd_ref.shape[0]):
    lo = i * V7X_MXU_DIM
    halves.append(_bf16_dot(d[:, lo:lo + V7X_MXU_DIM], w_bd_ref[i]))
  return jnp.concatenate(halves, axis=-1) * scale


def _ffn_stages(load_x1, w, emit, final_norm):
  st = {}

  def norm():
    st["x"] = load_x1()
    st["h"] = _rmsnorm(st["x"], w["g_ffn"][...]).astype(jnp.bfloat16)

  def gate_up():
    st["gate"] = jnp.dot(st["h"], w["wg"][...], preferred_element_type=jnp.float32)
    st["up"] = jnp.dot(st["h"], w["wu"][...], preferred_element_type=jnp.float32)

  def activate():
    gate = st["gate"]
    st["act"] = (gate * jax.nn.sigmoid(gate) * st["up"]).astype(jnp.bfloat16)

  def down():
    y = st["x"] + _skewed_dot(st["act"], w["wd"], st["x"].shape[1])
    emit(_rmsnorm(y, w["g_final"][...]) if final_norm else y)

  return [norm, gate_up, activate, down]


def _even_mix_stages(load_x, w, emit, mixers, d_pool, d_conv):
  st = {}

  def norm():
    st["x"] = load_x()
    st["h"] = _rmsnorm(st["x"], w["g_mix"][...]).astype(jnp.bfloat16)

  def in_proj():
    st["z"] = _skewed_dot(st["h"], w["w_in"], d_pool + 3 * d_conv)

  def mix():
    z = st["z"]
    p = z[:, :d_pool]
    xb = z[:, d_pool:d_pool + d_conv]
    bg = z[:, d_pool + d_conv:d_pool + 2 * d_conv]
    cg = z[:, d_pool + 2 * d_conv:]
    pooled, conv_y = mixers(p, cg * xb)
    st["pool_d"] = (pooled - p).astype(jnp.bfloat16)
    st["conv_out"] = (bg * conv_y).astype(jnp.bfloat16)

  def project():
    a_out = _pool_project(st["pool_d"], w["w_pool"], w["pool_scale"][...])
    st["mixed"] = jnp.concatenate([a_out.astype(jnp.bfloat16), st["conv_out"]], axis=-1)

  def out_proj():
    emit(st["x"] + _skewed_dot(st["mixed"], w["w_out"], st["x"].shape[1]))

  return [norm, in_proj, mix, project, out_proj]


def _rows_back(x, k):
  return pltpu.roll(x, k, axis=0)


def _even_prompt_mixers(t, p_hist, c_hist, conv_w, tile):
  assert all(w & (w - 1) == 0 and w <= POOL_PAD for w in POOL_WINDOWS)

  def mixers(p, c):
    pos = t * tile + lax.broadcasted_iota(jnp.int32, (POOL_PAD, POOL_GROUP), 0)
    pooled = []
    for g, win in enumerate(POOL_WINDOWS):
      lo = g * POOL_GROUP
      sums = jnp.concatenate([p_hist[:, lo:lo + POOL_GROUP], p[:, lo:lo + POOL_GROUP]],
                             axis=0)
      span = 1
      while span < win:
        sums = sums + _rows_back(sums, span)
        span *= 2
      sums = sums[POOL_PAD:]
      head = sums[:POOL_PAD] / jnp.minimum(pos + 1, win).astype(jnp.float32)
      pooled.append(jnp.concatenate([head, sums[POOL_PAD:] * (1.0 / win)], axis=0))
    p_hist[...] = p[tile - POOL_PAD:, :]
    c_rows = jnp.concatenate([c_hist[...], c], axis=0)
    y = _rows_back(c_rows, 2)[CONV_PAD:] * conv_w[0:1, :]
    y = y + _rows_back(c_rows, 1)[CONV_PAD:] * conv_w[1:2, :]
    y = y + c * conv_w[2:3, :]
    c_hist[...] = c[tile - CONV_PAD:, :]
    return jnp.concatenate(pooled, axis=-1), y
  return mixers


def _pool_history_sums_kernel(pool_hist_ref, sums_ref):
  for g, win in enumerate(POOL_WINDOWS):
    lo = g * POOL_GROUP
    acc = pool_hist_ref[POOL_HIST - 1, :, lo:lo + POOL_GROUP]
    for k in range(2, win):
      acc = acc + pool_hist_ref[POOL_HIST - k, :, lo:lo + POOL_GROUP]
    sums_ref[:, lo:lo + POOL_GROUP] = acc


def _even_sample_mixers(pool_sums_ref, conv_hist_ref, conv_w, p_ref, c_ref, start):
  def mixers(p, c):
    p_ref[...] = p
    c_ref[...] = c
    pooled = []
    for g, win in enumerate(POOL_WINDOWS):
      lo = g * POOL_GROUP
      acc = p[:, lo:lo + POOL_GROUP] + pool_sums_ref[:, lo:lo + POOL_GROUP]
      pooled.append(acc / float(min(start + 1, win)))
    y = conv_hist_ref[0] * conv_w[0:1, :]
    y = y + conv_hist_ref[1] * conv_w[1:2, :]
    y = y + c * conv_w[2:3, :]
    return jnp.concatenate(pooled, axis=-1), y
  return mixers


def _even_mix_head(t, p_hist, c_hist):
  @pl.when(t == 0)
  def _():
    p_hist[...] = jnp.zeros(p_hist.shape, jnp.float32)
    c_hist[...] = jnp.zeros(c_hist.shape, jnp.float32)


def _even_mix_tail(t, pool_state_ref, conv_state_ref, p_hist, c_hist, *, last_t):
  @pl.when(t == last_t)
  def _():
    pool_state_ref[...] = p_hist[POOL_PAD - POOL_HIST:, :]
    conv_state_ref[...] = c_hist[CONV_PAD - CONV_HIST:, :]


EVEN_SMALL_WEIGHT_NAMES = ("g_mix", "w_pool", "pool_scale", "conv_w", "g_ffn")
EVEN_BIG_WEIGHT_NAMES = ("w_in", "w_out", "wg", "wu", "wd")
ODD_WEIGHT_NAMES = ("g_mix", "w_in", "g_v", "w_s", "b_s_t", "w_out",
                    "g_ffn", "wg", "wu", "wd", "g_final")
N_CAST = 5


def _fetch_bf16(srcs, dsts, stage, sems):
  chunks = []
  for src, dst in zip(srcs, dsts):
    n_rows, n_cols = src.shape
    if dst.shape[1] > n_cols:
      dst[:, n_cols:] = jnp.zeros((n_rows, dst.shape[1] - n_cols), dst.dtype)
    for r0 in range(0, n_rows, FETCH_ROWS):
      chunks.append((src, dst, r0, min(FETCH_ROWS, n_rows - r0), n_cols))

  n_slots = stage.shape[0]

  def copy(i):
    src, _, r0, n, n_cols = chunks[i]
    return pltpu.make_async_copy(src.at[pl.ds(r0, n), :],
                                 stage.at[i % n_slots, pl.ds(0, n), pl.ds(0, n_cols)],
                                 sems.at[i % n_slots])

  for i in range(min(n_slots - 1, len(chunks))):
    copy(i).start()
  for i, (_, dst, r0, n, n_cols) in enumerate(chunks):
    if i + n_slots - 1 < len(chunks):
      copy(i + n_slots - 1).start()
    copy(i).wait()
    dst[pl.ds(r0, n), :n_cols] = stage[i % n_slots, :n, :n_cols].astype(dst.dtype)


def _run(stages):
  for stage in stages:
    stage()


def _pipelined_steps(s, n_tiles, mix_stages, ffn_stages):
  @pl.when(s == 0)
  def _():
    _run(mix_stages())

  @pl.when((s > 0) & (s < n_tiles))
  def _():
    mix, ffn = mix_stages(), ffn_stages()
    _run([mix[0], mix[1], ffn[0], mix[2], ffn[1], mix[3], ffn[2], ffn[3], mix[4]])

  @pl.when(s == n_tiles)
  def _():
    _run(ffn_stages())


def _set(ref):
  def emit(value):
    ref[...] = value
  return emit


def _even_kernel(*refs, n_tiles, tiles_per_seq, tile, start, d_pool, d_conv, layer):
  x_ref, xs_ref, pool_sums_ref, conv_hist_ref = refs[:4]
  n_small, n_big = len(EVEN_SMALL_WEIGHT_NAMES), len(EVEN_BIG_WEIGHT_NAMES)
  w = dict(zip(EVEN_SMALL_WEIGHT_NAMES, refs[4:4 + n_small]))
  big_hbm = refs[4 + n_small:4 + n_small + n_big]
  cast_in = refs[4 + n_small + n_big:4 + n_small + n_big + N_CAST]
  outs = refs[4 + n_small + n_big + N_CAST:]
  y_ref, pool_state_ref, conv_state_ref, ys_ref, ps_ref, cs_ref = outs[:6]
  cast_out = outs[6:6 + N_CAST]
  scratch = outs[6 + N_CAST:]
  p_hist, c_hist, x1_buf = scratch[:3]
  big_vmem = scratch[3:3 + n_big]
  stage, sems = scratch[3 + n_big:]
  w.update(zip(EVEN_BIG_WEIGHT_NAMES, big_vmem))

  s = pl.program_id(0)
  t = s % tiles_per_seq
  slot = s % 2

  @pl.when(s == 0)
  def _():
    _fetch_bf16([m.at[layer[i]] for i, m in enumerate(big_hbm)], big_vmem, stage, sems)

  @pl.when(s < n_tiles)
  def _():
    _even_mix_head(t, p_hist, c_hist)
    for src, dst in zip(cast_in, cast_out):
      n = src.shape[1]
      dst[:, :n] = src[...].astype(dst.dtype)
      if dst.shape[1] > n:
        dst[:, n:] = jnp.zeros((dst.shape[0], dst.shape[1] - n), dst.dtype)

  def mix_stages():
    return _even_mix_stages(
        lambda: x_ref[...], w, _set(x1_buf.at[slot]),
        _even_prompt_mixers(t, p_hist, c_hist, w["conv_w"], tile), d_pool, d_conv)

  def ffn_stages():
    return _ffn_stages(lambda: x1_buf[1 - slot], w, _set(y_ref), False)

  _pipelined_steps(s, n_tiles, mix_stages, ffn_stages)

  @pl.when(s < n_tiles)
  def _():
    _even_mix_tail(t, pool_state_ref, conv_state_ref, p_hist, c_hist,
                   last_t=tiles_per_seq - 1)

  @pl.when(s == n_tiles + 1)
  def _():
    st = {}
    mixers = _even_sample_mixers(pool_sums_ref, conv_hist_ref, w["conv_w"],
                                 ps_ref, cs_ref, start)
    _run(_even_mix_stages(lambda: xs_ref[...], w, lambda v: st.update(x1=v),
                          mixers, d_pool, d_conv))
    _run(_ffn_stages(lambda: st["x1"], w, _set(ys_ref), False))


def _odd_mix_stages(load_x, w, emit, gating, d_gate):
  st = {}

  def norm():
    st["x"] = load_x()
    st["h"] = _rmsnorm(st["x"], w["g_mix"][...]).astype(jnp.bfloat16)

  def in_proj():
    st["z"] = _skewed_dot(st["h"], w["w_in"], 2 * d_gate)

  def activate():
    z = _gelu_exact(st["z"])
    st["u"] = z[:, :d_gate]
    st["v"] = _rmsnorm(z[:, d_gate:], w["g_v"][...])

  def gate():
    st["gated"] = (st["u"] * gating(st["v"])).astype(jnp.bfloat16)

  def out_proj():
    emit(st["x"] + _skewed_dot(st["gated"], w["w_out"], st["x"].shape[1]))

  return [norm, in_proj, activate, gate, out_proj]


def _odd_prompt_gating(w, tile, d_gate):
  def gating(v):
    vb = v.astype(jnp.bfloat16)
    n_chunks = tile // CHUNK
    head = d_gate // N_SG_HEADS
    row = lax.broadcasted_iota(jnp.int32, (CHUNK, CHUNK), 0)
    col = lax.broadcasted_iota(jnp.int32, (CHUNK, CHUNK), 1)
    causal = row >= col
    per_head = []
    for hd in range(N_SG_HEADS):
      w_h = jnp.where(causal, w["w_s"][hd], 0.0).astype(jnp.bfloat16)
      rhs = jnp.concatenate(
          [vb[c * CHUNK:(c + 1) * CHUNK, hd * head:(hd + 1) * head]
           for c in range(n_chunks)], axis=1)
      o = jnp.dot(w_h, rhs, preferred_element_type=jnp.float32)
      per_head.append(o + w["b_s_t"][:, hd:hd + 1])
    return jnp.concatenate(
        [jnp.concatenate([o[:, c * head:(c + 1) * head] for o in per_head], axis=1)
         for c in range(n_chunks)], axis=0)
  return gating


def _odd_sample_gating(w, v_ref, d_gate):
  def gating(v):
    v_ref[...] = v
    head = d_gate // N_SG_HEADS
    rows = v.shape[0]
    vb = v.astype(jnp.bfloat16).astype(jnp.float32)
    per_head = []
    for hd in range(N_SG_HEADS):
      w00 = w["w_s"][hd, 0:1, 0:1].astype(jnp.bfloat16).astype(jnp.float32)
      b0 = w["b_s_t"][0:1, hd:hd + 1]
      per_head.append(vb[:, hd * head:(hd + 1) * head] * jnp.broadcast_to(w00, (rows, head))
                      + jnp.broadcast_to(b0, (rows, head)))
    return jnp.concatenate(per_head, axis=1)
  return gating


def _odd_kernel(*refs, n_tiles, tile, d_gate):
  x_ref, xs_ref = refs[:2]
  n_w = len(ODD_WEIGHT_NAMES)
  w = dict(zip(ODD_WEIGHT_NAMES, refs[2:2 + n_w]))
  y_ref, ys_ref, vs_ref, x1_buf = refs[2 + n_w:]

  s = pl.program_id(0)
  slot = s % 2

  def mix_stages():
    return _odd_mix_stages(lambda: x_ref[...], w, _set(x1_buf.at[slot]),
                           _odd_prompt_gating(w, tile, d_gate), d_gate)

  def ffn_stages():
    return _ffn_stages(lambda: x1_buf[1 - slot], w, _set(y_ref), True)

  _pipelined_steps(s, n_tiles, mix_stages, ffn_stages)

  @pl.when(s == n_tiles + 1)
  def _():
    st = {}
    _run(_odd_mix_stages(lambda: xs_ref[...], w, lambda v: st.update(x1=v),
                         _odd_sample_gating(w, vs_ref, d_gate), d_gate))
    _run(_ffn_stages(lambda: st["x1"], w, _set(ys_ref), True))


def _resident(arr):
  zeros = (0,) * arr.ndim
  return pl.BlockSpec(arr.shape, lambda s: zeros, pipeline_mode=pl.Buffered(1))


def _whole_out(shape):
  zeros = (0,) * len(shape)
  return pl.BlockSpec(shape, lambda s: zeros)


def _params():
  return pltpu.CompilerParams(dimension_semantics=("arbitrary",),
                              vmem_limit_bytes=V7X_VMEM_LIMIT_BYTES)


def _prompt_specs(x, tile):
  batch, seq, d_model = x.shape
  tiles_per_seq = seq // tile
  n_tiles = batch * tiles_per_seq
  mix_tile = lambda s: jnp.clip(s, 0, n_tiles - 1)
  ffn_tile = lambda s: jnp.clip(s - 1, 0, n_tiles - 1)
  block = lambda which: pl.BlockSpec(
      (None, tile, d_model),
      lambda s: (which(s) // tiles_per_seq, which(s) % tiles_per_seq, 0))
  per_seq = lambda rows, ch: pl.BlockSpec(
      (None, rows, ch), lambda s: (mix_tile(s) // tiles_per_seq, 0, 0))
  return n_tiles, tiles_per_seq, mix_tile, block(mix_tile), block(ffn_tile), per_seq


def _cast_slab_spec(rows, cols, n_steps, step, layer):
  slab = rows // n_steps
  steps_per_slab = 1
  while slab % V7X_BF16_SUBLANES:
    slab *= 2
    steps_per_slab *= 2
  src = pl.BlockSpec((None, slab, cols), lambda s: (layer, step(s) // steps_per_slab, 0))
  dst = pl.BlockSpec((slab, cols + _skew_cols(cols)),
                     lambda s: (step(s) // steps_per_slab, 0))
  return src, dst


def _pool_history_sums(pool_hist):
  rows, d_pool = pool_hist.shape[1:]
  return pl.pallas_call(
      _pool_history_sums_kernel,
      grid=(1,),
      in_specs=[pl.BlockSpec(pool_hist.shape, lambda i: (0, 0, 0))],
      out_specs=pl.BlockSpec((rows, d_pool), lambda i: (0, 0)),
      out_shape=jax.ShapeDtypeStruct((rows, d_pool), pool_hist.dtype),
      name="pool_history_sums",
  )(pool_hist)


def _even_layer(x, xs, pool_sums, conv_hist, small_weights, mats, mat_layer,
                next_mats, next_layer, tile, start):
  n_tiles, tiles_per_seq, step, tok_in, tok_out, per_seq = _prompt_specs(x, tile)
  batch = x.shape[0]
  rows, d_model = xs.shape
  d_pool = pool_sums.shape[-1]
  d_conv = conv_hist.shape[-1]
  cast_specs = [_cast_slab_spec(m.shape[1], m.shape[2], n_tiles, step, next_layer[i])
                for i, m in enumerate(next_mats)]
  f32 = x.dtype
  skewed = lambda m: (m.shape[1], m.shape[2] + _skew_cols(m.shape[2]))
  widest = max(m.shape[2] for m in mats)
  return pl.pallas_call(
      functools.partial(_even_kernel, n_tiles=n_tiles,
                        tiles_per_seq=tiles_per_seq, tile=tile, start=start,
                        d_pool=d_pool, d_conv=d_conv, layer=tuple(mat_layer)),
      grid=(n_tiles + 2,),
      in_specs=([tok_in, _resident(xs), _resident(pool_sums), _resident(conv_hist)]
                + [_resident(w) for w in small_weights]
                + [pl.BlockSpec(memory_space=pl.ANY) for _ in mats]
                + [c[0] for c in cast_specs]),
      out_specs=([tok_out, per_seq(POOL_HIST, d_pool), per_seq(CONV_HIST, d_conv),
                  _whole_out(xs.shape), _whole_out((rows, d_pool)),
                  _whole_out((rows, d_conv))] + [c[1] for c in cast_specs]),
      out_shape=([jax.ShapeDtypeStruct(x.shape, f32),
                  jax.ShapeDtypeStruct((batch, POOL_HIST, d_pool), f32),
                  jax.ShapeDtypeStruct((batch, CONV_HIST, d_conv), f32),
                  jax.ShapeDtypeStruct(xs.shape, f32),
                  jax.ShapeDtypeStruct((rows, d_pool), f32),
                  jax.ShapeDtypeStruct((rows, d_conv), f32)]
                 + [jax.ShapeDtypeStruct(skewed(m), jnp.bfloat16) for m in next_mats]),
      scratch_shapes=([pltpu.VMEM((POOL_PAD, d_pool), jnp.float32),
                       pltpu.VMEM((CONV_PAD, d_conv), jnp.float32),
                       pltpu.VMEM((2, tile, d_model), jnp.float32)]
                      + [pltpu.VMEM(skewed(m), jnp.bfloat16) for m in mats]
                      + [pltpu.VMEM((FETCH_SLOTS, FETCH_ROWS, widest), jnp.float32),
                         pltpu.SemaphoreType.DMA((FETCH_SLOTS,))]),
      compiler_params=_params(),
      name="even_layer",
  )(x, xs, pool_sums, conv_hist, *small_weights, *mats, *next_mats)


def _odd_layer(x, xs, weights, tile):
  n_tiles, _, _, tok_in, tok_out, _ = _prompt_specs(x, tile)
  rows, d_model = xs.shape
  d_gate = weights[ODD_WEIGHT_NAMES.index("g_v")].shape[-1]
  return pl.pallas_call(
      functools.partial(_odd_kernel, n_tiles=n_tiles, tile=tile, d_gate=d_gate),
      grid=(n_tiles + 2,),
      in_specs=[tok_in, _resident(xs)] + [_resident(w) for w in weights],
      out_specs=[tok_out, _whole_out(xs.shape), _whole_out((rows, d_gate))],
      out_shape=[jax.ShapeDtypeStruct(x.shape, x.dtype),
                 jax.ShapeDtypeStruct(xs.shape, x.dtype),
                 jax.ShapeDtypeStruct((rows, d_gate), x.dtype)],
      scratch_shapes=[pltpu.VMEM((2, tile, d_model), jnp.float32)],
      compiler_params=_params(),
      name="odd_layer",
  )(x, xs, *weights)


def _pool_block_diag(w_pool):
  per_tile = V7X_MXU_DIM // POOL_GROUP
  n_tiles = w_pool.shape[0] // per_tile
  out = jnp.zeros((n_tiles, V7X_MXU_DIM, V7X_MXU_DIM), jnp.bfloat16)
  for g in range(w_pool.shape[0]):
    i, j = divmod(g, per_tile)
    lo = j * POOL_GROUP
    out = out.at[i, lo:lo + POOL_GROUP, lo:lo + POOL_GROUP].set(
        w_pool[g].astype(jnp.bfloat16))
  return out


def kernel(x_prompt, x_sample, state_pool, state_conv, norm_mix, norm_ffn, norm_final, w_in_even, w_pool, pool_scale, conv_w, w_out_even, w_in_odd, norm_sg, w_s, b_s, w_out_odd, ffn_w_gate, ffn_w_up, ffn_w_down):
  depth = norm_mix.shape[0]
  assert depth == 2 and w_in_even.shape[0] == 1 and w_in_odd.shape[0] == 1
  assert x_sample.shape[1] == 1
  assert x_prompt.shape[1] % PROMPT_TILE == 0 and PROMPT_TILE % CHUNK == 0

  row = lambda v: v.reshape(1, -1)

  even_small = (row(norm_mix[0]), _pool_block_diag(w_pool[0]), row(pool_scale[0]),
                conv_w[0], row(norm_ffn[0]))
  even_mats = (w_in_even, w_out_even, ffn_w_gate, ffn_w_up, ffn_w_down)
  even_mat_layer = (0, 0, 0, 0, 0)
  odd_mats = (w_in_odd, w_out_odd, ffn_w_gate, ffn_w_up, ffn_w_down)
  odd_mat_layer = (0, 0, 1, 1, 1)

  pool_hist = jnp.swapaxes(state_pool[0], 0, 1)
  conv_hist = jnp.swapaxes(state_conv[0], 0, 1)

  (x1_p, pool_p, conv_p, x1_s, p_s, c_s, w_in_o, w_out_o, wg_o, wu_o,
   wd_o) = _even_layer(x_prompt, x_sample[:, 0, :], _pool_history_sums(pool_hist),
                       conv_hist, even_small, even_mats, even_mat_layer,
                       odd_mats, odd_mat_layer, PROMPT_TILE, PAST_LEN)

  odd_w = (row(norm_mix[1]), w_in_o, row(norm_sg[0]), w_s[0], b_s[0].T, w_out_o,
           row(norm_ffn[1]), wg_o, wu_o, wd_o, row(norm_final))
  y_p, y_s, v_s = _odd_layer(x1_p, x1_s, odd_w, PROMPT_TILE)

  pool_s = jnp.concatenate([state_pool[:, :, 1:], p_s[None, :, None, :]], axis=2)
  conv_s = jnp.concatenate([state_conv[:, :, 1:], c_s[None, :, None, :]], axis=2)

  return (y_p, y_s[:, None, :], pool_p[None], pool_s, conv_p[None], conv_s,
          v_s[None, :, None, :])
```

```python
import functools
import math

import jax
import jax.numpy as jnp
from jax import lax
from jax.experimental import pallas as pl
from jax.experimental.pallas import tpu as pltpu

POOL_WINDOWS = (2, 4, 8, 16)
POOL_GROUP = 128
POOL_HIST = max(POOL_WINDOWS) - 1
CONV_WIDTH = 3
CONV_HIST = CONV_WIDTH - 1
CHUNK = 128
N_SG_HEADS = 8
EPS = 1e-6
PAST_LEN = 16384

V7X_SUBLANES = 8
V7X_LANES = 128
V7X_BF16_SUBLANES = 16
V7X_MXU_DIM = 256
V7X_VMEM_LIMIT_BYTES = 60000 * 1024

POOL_PAD = -(-POOL_HIST // V7X_SUBLANES) * V7X_SUBLANES
CONV_PAD = -(-CONV_HIST // V7X_SUBLANES) * V7X_SUBLANES

PROMPT_TILE = 512
FETCH_ROWS = 64
FETCH_SLOTS = 8


def _bf16_dot(a, b):
  return jnp.dot(a.astype(jnp.bfloat16), b, preferred_element_type=jnp.float32)


def _rmsnorm(x, g):
  y = x * lax.rsqrt(jnp.mean(x * x, axis=-1, keepdims=True) + EPS)
  return y * g


def _gelu_exact(x):
  return 0.5 * x * (1.0 + lax.erf(x * math.sqrt(0.5)))


def _skew_cols(n):
  return V7X_LANES if (n // V7X_LANES) % V7X_SUBLANES == 0 else 0


def _skewed_dot(a, w_ref, n):
  return jnp.dot(a, w_ref[:, :n], preferred_element_type=jnp.float32)


def _pool_project(d, w_bd_ref, scale):
  halves = []
  for i in range(w_bd_ref.shape[0]):
    lo = i * V7X_MXU_DIM
    halves.append(_bf16_dot(d[:, lo:lo + V7X_MXU_DIM], w_bd_ref[i]))
  return jnp.concatenate(halves, axis=-1) * scale


def _ffn_stages(load_x1, w, emit, final_norm):
  st = {}

  def norm():
    st["x"] = load_x1()
    st["h"] = _rmsnorm(st["x"], w["g_ffn"][...]).astype(jnp.bfloat16)

  def gate_up():
    st["gate"] = jnp.dot(st["h"], w["wg"][...], preferred_element_type=jnp.float32)
    st["up"] = jnp.dot(st["h"], w["wu"][...], preferred_element_type=jnp.float32)

  def activate():
    gate = st["gate"]
    st["act"] = (gate * jax.nn.sigmoid(gate) * st["up"]).astype(jnp.bfloat16)

  def down():
    y = st["x"] + _skewed_dot(st["act"], w["wd"], st["x"].shape[1])
    emit(_rmsnorm(y, w["g_final"][...]) if final_norm else y)

  return [norm, gate_up, activate, down]


def _even_mix_stages(load_x, w, emit, mixers, d_pool, d_conv):
  st = {}

  def norm():
    st["x"] = load_x()
    st["h"] = _rmsnorm(st["x"], w["g_mix"][...]).astype(jnp.bfloat16)

  def in_proj():
    st["z"] = _skewed_dot(st["h"], w["w_in"], d_pool + 3 * d_conv)

  def mix():
    z = st["z"]
    p = z[:, :d_pool]
    xb = z[:, d_pool:d_pool + d_conv]
    bg = z[:, d_pool + d_conv:d_pool + 2 * d_conv]
    cg = z[:, d_pool + 2 * d_conv:]
    pooled, conv_y = mixers(p, cg * xb)
    st["pool_d"] = (pooled - p).astype(jnp.bfloat16)
    st["conv_out"] = (bg * conv_y).astype(jnp.bfloat16)

  def project():
    a_out = _pool_project(st["pool_d"], w["w_pool"], w["pool_scale"][...])
    st["mixed"] = jnp.concatenate([a_out.astype(jnp.bfloat16), st["conv_out"]], axis=-1)

  def out_proj():
    emit(st["x"] + _skewed_dot(st["mixed"], w["w_out"], st["x"].shape[1]))

  return [norm, in_proj, mix, project, out_proj]


def _rows_back(x, k):
  return pltpu.roll(x, k, axis=0)


def _even_prompt_mixers(t, p_hist, c_hist, conv_w, tile):
  assert all(w & (w - 1) == 0 and w <= POOL_PAD for w in POOL_WINDOWS)

  def mixers(p, c):
    pos = t * tile + lax.broadcasted_iota(jnp.int32, (POOL_PAD, POOL_GROUP), 0)
    pooled = []
    for g, win in enumerate(POOL_WINDOWS):
      lo = g * POOL_GROUP
      sums = jnp.concatenate([p_hist[:, lo:lo + POOL_GROUP], p[:, lo:lo + POOL_GROUP]],
                             axis=0)
      span = 1
      while span < win:
        sums = sums + _rows_back(sums, span)
        span *= 2
      sums = sums[POOL_PAD:]
      head = sums[:POOL_PAD] / jnp.minimum(pos + 1, win).astype(jnp.float32)
      pooled.append(jnp.concatenate([head, sums[POOL_PAD:] * (1.0 / win)], axis=0))
    p_hist[...] = p[tile - POOL_PAD:, :]
    c_rows = jnp.concatenate([c_hist[...], c], axis=0)
    y = _rows_back(c_rows, 2)[CONV_PAD:] * conv_w[0:1, :]
    y = y + _rows_back(c_rows, 1)[CONV_PAD:] * conv_w[1:2, :]
    y = y + c * conv_w[2:3, :]
    c_hist[...] = c[tile - CONV_PAD:, :]
    return jnp.concatenate(pooled, axis=-1), y
  return mixers


def _pool_history_sums_kernel(pool_hist_ref, sums_ref):
  for g, win in enumerate(POOL_WINDOWS):
    lo = g * POOL_GROUP
    acc = pool_hist_ref[POOL_HIST - 1, :, lo:lo + POOL_GROUP]
    for k in range(2, win):
      acc = acc + pool_hist_ref[POOL_HIST - k, :, lo:lo + POOL_GROUP]
    sums_ref[:, lo:lo + POOL_GROUP] = acc


def _even_sample_mixers(pool_sums_ref, conv_hist_ref, conv_w, p_ref, c_ref, start):
  def mixers(p, c):
    p_ref[...] = p
    c_ref[...] = c
    pooled = []
    for g, win in enumerate(POOL_WINDOWS):
      lo = g * POOL_GROUP
      acc = p[:, lo:lo + POOL_GROUP] + pool_sums_ref[:, lo:lo + POOL_GROUP]
      pooled.append(acc / float(min(start + 1, win)))
    y = conv_hist_ref[0] * conv_w[0:1, :]
    y = y + conv_hist_ref[1] * conv_w[1:2, :]
    y = y + c * conv_w[2:3, :]
    return jnp.concatenate(pooled, axis=-1), y
  return mixers


def _even_mix_head(t, p_hist, c_hist):
  @pl.when(t == 0)
  def _():
    p_hist[...] = jnp.zeros(p_hist.shape, jnp.float32)
    c_hist[...] = jnp.zeros(c_hist.shape, jnp.float32)


def _even_mix_tail(t, pool_state_ref, conv_state_ref, p_hist, c_hist, *, last_t):
  @pl.when(t == last_t)
  def _():
    pool_state_ref[...] = p_hist[POOL_PAD - POOL_HIST:, :]
    conv_state_ref[...] = c_hist[CONV_PAD - CONV_HIST:, :]


EVEN_SMALL_WEIGHT_NAMES = ("g_mix", "w_pool", "pool_scale", "conv_w", "g_ffn")
EVEN_BIG_WEIGHT_NAMES = ("w_in", "w_out", "wg", "wu", "wd")
ODD_WEIGHT_NAMES = ("g_mix", "w_in", "g_v", "w_s", "b_s_t", "w_out",
                    "g_ffn", "wg", "wu", "wd", "g_final")
N_CAST = 5


def _fetch_bf16(srcs, dsts, stage, sems):
  chunks = []
  for src, dst in zip(srcs, dsts):
    n_rows, n_cols = src.shape
    if dst.shape[1] > n_cols:
      dst[:, n_cols:] = jnp.zeros((n_rows, dst.shape[1] - n_cols), dst.dtype)
    for r0 in range(0, n_rows, FETCH_ROWS):
      chunks.append((src, dst, r0, min(FETCH_ROWS, n_rows - r0), n_cols))

  n_slots = stage.shape[0]

  def copy(i):
    src, _, r0, n, n_cols = chunks[i]
    return pltpu.make_async_copy(src.at[pl.ds(r0, n), :],
                                 stage.at[i % n_slots, pl.ds(0, n), pl.ds(0, n_cols)],
                                 sems.at[i % n_slots])

  for i in range(min(n_slots - 1, len(chunks))):
    copy(i).start(priority=i % 2)
  for i, (_, dst, r0, n, n_cols) in enumerate(chunks):
    if i + n_slots - 1 < len(chunks):
      copy(i + n_slots - 1).start(priority=(i + n_slots - 1) % 2)
    copy(i).wait()
    dst[pl.ds(r0, n), :n_cols] = stage[i % n_slots, :n, :n_cols].astype(dst.dtype)


def _run(stages):
  for stage in stages:
    stage()


def _pipelined_steps(s, n_tiles, mix_stages, ffn_stages):
  @pl.when(s == 0)
  def _():
    _run(mix_stages())

  @pl.when((s > 0) & (s < n_tiles))
  def _():
    mix, ffn = mix_stages(), ffn_stages()
    _run([mix[0], mix[1], ffn[0], mix[2], ffn[1], mix[3], ffn[2], ffn[3], mix[4]])

  @pl.when(s == n_tiles)
  def _():
    _run(ffn_stages())


def _set(ref):
  def emit(value):
    ref[...] = value
  return emit


def _even_kernel(*refs, n_tiles, tiles_per_seq, tile, start, d_pool, d_conv, layer):
  x_ref, xs_ref, pool_sums_ref, conv_hist_ref = refs[:4]
  n_small, n_big = len(EVEN_SMALL_WEIGHT_NAMES), len(EVEN_BIG_WEIGHT_NAMES)
  w = dict(zip(EVEN_SMALL_WEIGHT_NAMES, refs[4:4 + n_small]))
  big_hbm = refs[4 + n_small:4 + n_small + n_big]
  cast_in = refs[4 + n_small + n_big:4 + n_small + n_big + N_CAST]
  outs = refs[4 + n_small + n_big + N_CAST:]
  y_ref, pool_state_ref, conv_state_ref, ys_ref, ps_ref, cs_ref = outs[:6]
  cast_out = outs[6:6 + N_CAST]
  scratch = outs[6 + N_CAST:]
  p_hist, c_hist, x1_buf = scratch[:3]
  big_vmem = scratch[3:3 + n_big]
  stage, sems = scratch[3 + n_big:]
  w.update(zip(EVEN_BIG_WEIGHT_NAMES, big_vmem))

  s = pl.program_id(0)
  t = s % tiles_per_seq
  slot = s % 2

  @pl.when(s == 0)
  def _():
    _fetch_bf16([m.at[layer[i]] for i, m in enumerate(big_hbm)], big_vmem, stage, sems)

  @pl.when(s < n_tiles)
  def _():
    _even_mix_head(t, p_hist, c_hist)
    for src, dst in zip(cast_in, cast_out):
      n = src.shape[1]
      dst[:, :n] = src[...].astype(dst.dtype)
      if dst.shape[1] > n:
        dst[:, n:] = jnp.zeros((dst.shape[0], dst.shape[1] - n), dst.dtype)

  def mix_stages():
    return _even_mix_stages(
        lambda: x_ref[...], w, _set(x1_buf.at[slot]),
        _even_prompt_mixers(t, p_hist, c_hist, w["conv_w"], tile), d_pool, d_conv)

  def ffn_stages():
    return _ffn_stages(lambda: x1_buf[1 - slot], w, _set(y_ref), False)

  _pipelined_steps(s, n_tiles, mix_stages, ffn_stages)

  @pl.when(s < n_tiles)
  def _():
    _even_mix_tail(t, pool_state_ref, conv_state_ref, p_hist, c_hist,
                   last_t=tiles_per_seq - 1)

  @pl.when(s == n_tiles + 1)
  def _():
    st = {}
    mixers = _even_sample_mixers(pool_sums_ref, conv_hist_ref, w["conv_w"],
                                 ps_ref, cs_ref, start)
    _run(_even_mix_stages(lambda: xs_ref[...], w, lambda v: st.update(x1=v),
                          mixers, d_pool, d_conv))
    _run(_ffn_stages(lambda: st["x1"], w, _set(ys_ref), False))


def _odd_mix_stages(load_x, w, emit, gating, d_gate):
  st = {}

  def norm():
    st["x"] = load_x()
    st["h"] = _rmsnorm(st["x"], w["g_mix"][...]).astype(jnp.bfloat16)

  def in_proj():
    st["z"] = _skewed_dot(st["h"], w["w_in"], 2 * d_gate)

  def activate():
    z = _gelu_exact(st["z"])
    st["u"] = z[:, :d_gate]
    st["v"] = _rmsnorm(z[:, d_gate:], w["g_v"][...])

  def gate():
    st["gated"] = (st["u"] * gating(st["v"])).astype(jnp.bfloat16)

  def out_proj():
    emit(st["x"] + _skewed_dot(st["gated"], w["w_out"], st["x"].shape[1]))

  return [norm, in_proj, activate, gate, out_proj]


def _odd_prompt_gating(w, tile, d_gate):
  def gating(v):
    vb = v.astype(jnp.bfloat16)
    n_chunks = tile // CHUNK
    head = d_gate // N_SG_HEADS
    row = lax.broadcasted_iota(jnp.int32, (CHUNK, CHUNK), 0)
    col = lax.broadcasted_iota(jnp.int32, (CHUNK, CHUNK), 1)
    causal = row >= col
    per_head = []
    for hd in range(N_SG_HEADS):
      w_h = jnp.where(causal, w["w_s"][hd], 0.0).astype(jnp.bfloat16)
      rhs = jnp.concatenate(
          [vb[c * CHUNK:(c + 1) * CHUNK, hd * head:(hd + 1) * head]
           for c in range(n_chunks)], axis=1)
      o = jnp.dot(w_h, rhs, preferred_element_type=jnp.float32)
      per_head.append(o + w["b_s_t"][:, hd:hd + 1])
    return jnp.concatenate(
        [jnp.concatenate([o[:, c * head:(c + 1) * head] for o in per_head], axis=1)
         for c in range(n_chunks)], axis=0)
  return gating


def _odd_sample_gating(w, v_ref, d_gate):
  def gating(v):
    v_ref[...] = v
    head = d_gate // N_SG_HEADS
    rows = v.shape[0]
    vb = v.astype(jnp.bfloat16).astype(jnp.float32)
    per_head = []
    for hd in range(N_SG_HEADS):
      w00 = w["w_s"][hd, 0:1, 0:1].astype(jnp.bfloat16).astype(jnp.float32)
      b0 = w["b_s_t"][0:1, hd:hd + 1]
      per_head.append(vb[:, hd * head:(hd + 1) * head] * jnp.broadcast_to(w00, (rows, head))
                      + jnp.broadcast_to(b0, (rows, head)))
    return jnp.concatenate(per_head, axis=1)
  return gating


def _odd_kernel(*refs, n_tiles, tile, d_gate):
  x_ref, xs_ref = refs[:2]
  n_w = len(ODD_WEIGHT_NAMES)
  w = dict(zip(ODD_WEIGHT_NAMES, refs[2:2 + n_w]))
  y_ref, ys_ref, vs_ref, x1_buf = refs[2 + n_w:]

  s = pl.program_id(0)
  slot = s % 2

  def mix_stages():
    return _odd_mix_stages(lambda: x_ref[...], w, _set(x1_buf.at[slot]),
                           _odd_prompt_gating(w, tile, d_gate), d_gate)

  def ffn_stages():
    return _ffn_stages(lambda: x1_buf[1 - slot], w, _set(y_ref), True)

  _pipelined_steps(s, n_tiles, mix_stages, ffn_stages)

  @pl.when(s == n_tiles + 1)
  def _():
    st = {}
    _run(_odd_mix_stages(lambda: xs_ref[...], w, lambda v: st.update(x1=v),
                         _odd_sample_gating(w, vs_ref, d_gate), d_gate))
    _run(_ffn_stages(lambda: st["x1"], w, _set(ys_ref), True))


def _resident(arr):
  zeros = (0,) * arr.ndim
  return pl.BlockSpec(arr.shape, lambda s: zeros, pipeline_mode=pl.Buffered(1))


def _whole_out(shape):
  zeros = (0,) * len(shape)
  return pl.BlockSpec(shape, lambda s: zeros)


def _params():
  return pltpu.CompilerParams(dimension_semantics=("arbitrary",),
                              vmem_limit_bytes=V7X_VMEM_LIMIT_BYTES)


def _prompt_specs(x, tile):
  batch, seq, d_model = x.shape
  tiles_per_seq = seq // tile
  n_tiles = batch * tiles_per_seq
  mix_tile = lambda s: jnp.clip(s, 0, n_tiles - 1)
  ffn_tile = lambda s: jnp.clip(s - 1, 0, n_tiles - 1)
  block = lambda which: pl.BlockSpec(
      (None, tile, d_model),
      lambda s: (which(s) // tiles_per_seq, which(s) % tiles_per_seq, 0))
  per_seq = lambda rows, ch: pl.BlockSpec(
      (None, rows, ch), lambda s: (mix_tile(s) // tiles_per_seq, 0, 0))
  return n_tiles, tiles_per_seq, mix_tile, block(mix_tile), block(ffn_tile), per_seq


def _cast_slab_spec(rows, cols, n_steps, step, layer):
  slab = rows // n_steps
  steps_per_slab = 1
  while slab % V7X_BF16_SUBLANES:
    slab *= 2
    steps_per_slab *= 2
  src = pl.BlockSpec((None, slab, cols), lambda s: (layer, step(s) // steps_per_slab, 0))
  dst = pl.BlockSpec((slab, cols + _skew_cols(cols)),
                     lambda s: (step(s) // steps_per_slab, 0))
  return src, dst


def _pool_history_sums(pool_hist):
  rows, d_pool = pool_hist.shape[1:]
  return pl.pallas_call(
      _pool_history_sums_kernel,
      grid=(1,),
      in_specs=[pl.BlockSpec(pool_hist.shape, lambda i: (0, 0, 0))],
      out_specs=pl.BlockSpec((rows, d_pool), lambda i: (0, 0)),
      out_shape=jax.ShapeDtypeStruct((rows, d_pool), pool_hist.dtype),
      name="pool_history_sums",
  )(pool_hist)


def _even_layer(x, xs, pool_sums, conv_hist, small_weights, mats, mat_layer,
                next_mats, next_layer, tile, start):
  n_tiles, tiles_per_seq, step, tok_in, tok_out, per_seq = _prompt_specs(x, tile)
  batch = x.shape[0]
  rows, d_model = xs.shape
  d_pool = pool_sums.shape[-1]
  d_conv = conv_hist.shape[-1]
  cast_specs = [_cast_slab_spec(m.shape[1], m.shape[2], n_tiles, step, next_layer[i])
                for i, m in enumerate(next_mats)]
  f32 = x.dtype
  skewed = lambda m: (m.shape[1], m.shape[2] + _skew_cols(m.shape[2]))
  widest = max(m.shape[2] for m in mats)
  return pl.pallas_call(
      functools.partial(_even_kernel, n_tiles=n_tiles,
                        tiles_per_seq=tiles_per_seq, tile=tile, start=start,
                        d_pool=d_pool, d_conv=d_conv, layer=tuple(mat_layer)),
      grid=(n_tiles + 2,),
      in_specs=([tok_in, _resident(xs), _resident(pool_sums), _resident(conv_hist)]
                + [_resident(w) for w in small_weights]
                + [pl.BlockSpec(memory_space=pl.ANY) for _ in mats]
                + [c[0] for c in cast_specs]),
      out_specs=([tok_out, per_seq(POOL_HIST, d_pool), per_seq(CONV_HIST, d_conv),
                  _whole_out(xs.shape), _whole_out((rows, d_pool)),
                  _whole_out((rows, d_conv))] + [c[1] for c in cast_specs]),
      out_shape=([jax.ShapeDtypeStruct(x.shape, f32),
                  jax.ShapeDtypeStruct((batch, POOL_HIST, d_pool), f32),
                  jax.ShapeDtypeStruct((batch, CONV_HIST, d_conv), f32),
                  jax.ShapeDtypeStruct(xs.shape, f32),
                  jax.ShapeDtypeStruct((rows, d_pool), f32),
                  jax.ShapeDtypeStruct((rows, d_conv), f32)]
                 + [jax.ShapeDtypeStruct(skewed(m), jnp.bfloat16) for m in next_mats]),
      scratch_shapes=([pltpu.VMEM((POOL_PAD, d_pool), jnp.float32),
                       pltpu.VMEM((CONV_PAD, d_conv), jnp.float32),
                       pltpu.VMEM((2, tile, d_model), jnp.float32)]
                      + [pltpu.VMEM(skewed(m), jnp.bfloat16) for m in mats]
                      + [pltpu.VMEM((FETCH_SLOTS, FETCH_ROWS, widest), jnp.float32),
                         pltpu.SemaphoreType.DMA((FETCH_SLOTS,))]),
      compiler_params=_params(),
      name="even_layer",
  )(x, xs, pool_sums, conv_hist, *small_weights, *mats, *next_mats)


def _odd_layer(x, xs, weights, tile):
  n_tiles, _, _, tok_in, tok_out, _ = _prompt_specs(x, tile)
  rows, d_model = xs.shape
  d_gate = weights[ODD_WEIGHT_NAMES.index("g_v")].shape[-1]
  return pl.pallas_call(
      functools.partial(_odd_kernel, n_tiles=n_tiles, tile=tile, d_gate=d_gate),
      grid=(n_tiles + 2,),
      in_specs=[tok_in, _resident(xs)] + [_resident(w) for w in weights],
      out_specs=[tok_out, _whole_out(xs.shape), _whole_out((rows, d_gate))],
      out_shape=[jax.ShapeDtypeStruct(x.shape, x.dtype),
                 jax.ShapeDtypeStruct(xs.shape, x.dtype),
                 jax.ShapeDtypeStruct((rows, d_gate), x.dtype)],
      scratch_shapes=[pltpu.VMEM((2, tile, d_model), jnp.float32)],
      compiler_params=_params(),
      name="odd_layer",
  )(x, xs, *weights)


def _pool_block_diag(w_pool):
  per_tile = V7X_MXU_DIM // POOL_GROUP
  n_tiles = w_pool.shape[0] // per_tile
  out = jnp.zeros((n_tiles, V7X_MXU_DIM, V7X_MXU_DIM), jnp.bfloat16)
  for g in range(w_pool.shape[0]):
    i, j = divmod(g, per_tile)
    lo = j * POOL_GROUP
    out = out.at[i, lo:lo + POOL_GROUP, lo:lo + POOL_GROUP].set(
        w_pool[g].astype(jnp.bfloat16))
  return out


def kernel(x_prompt, x_sample, state_pool, state_conv, norm_mix, norm_ffn, norm_final, w_in_even, w_pool, pool_scale, conv_w, w_out_even, w_in_odd, norm_sg, w_s, b_s, w_out_odd, ffn_w_gate, ffn_w_up, ffn_w_down):
  depth = norm_mix.shape[0]
  assert depth == 2 and w_in_even.shape[0] == 1 and w_in_odd.shape[0] == 1
  assert x_sample.shape[1] == 1
  assert x_prompt.shape[1] % PROMPT_TILE == 0 and PROMPT_TILE % CHUNK == 0

  row = lambda v: v.reshape(1, -1)

  even_small = (row(norm_mix[0]), _pool_block_diag(w_pool[0]), row(pool_scale[0]),
                conv_w[0], row(norm_ffn[0]))
  even_mats = (w_in_even, w_out_even, ffn_w_gate, ffn_w_up, ffn_w_down)
  even_mat_layer = (0, 0, 0, 0, 0)
  odd_mats = (w_in_odd, w_out_odd, ffn_w_gate, ffn_w_up, ffn_w_down)
  odd_mat_layer = (0, 0, 1, 1, 1)

  pool_hist = jnp.swapaxes(state_pool[0], 0, 1)
  conv_hist = jnp.swapaxes(state_conv[0], 0, 1)

  (x1_p, pool_p, conv_p, x1_s, p_s, c_s, w_in_o, w_out_o, wg_o, wu_o,
   wd_o) = _even_layer(x_prompt, x_sample[:, 0, :], _pool_history_sums(pool_hist),
                       conv_hist, even_small, even_mats, even_mat_layer,
                       odd_mats, odd_mat_layer, PROMPT_TILE, PAST_LEN)

  odd_w = (row(norm_mix[1]), w_in_o, row(norm_sg[0]), w_s[0], b_s[0].T, w_out_o,
           row(norm_ffn[1]), wg_o, wu_o, wd_o, row(norm_final))
  y_p, y_s, v_s = _odd_layer(x1_p, x1_s, odd_w, PROMPT_TILE)

  pool_s = jnp.concatenate([state_pool[:, :, 1:], p_s[None, :, None, :]], axis=2)
  conv_s = jnp.concatenate([state_conv[:, :, 1:], c_s[None, :, None, :]], axis=2)

  return (y_p, y_s[:, None, :], pool_p[None], pool_s, conv_p[None], conv_s,
          v_s[None, :, None, :])
```

```python
import functools
import math

import jax
import jax.numpy as jnp
from jax import lax
from jax.experimental import pallas as pl
from jax.experimental.pallas import tpu as pltpu

POOL_WINDOWS = (2, 4, 8, 16)
POOL_GROUP = 128
POOL_HIST = max(POOL_WINDOWS) - 1
CONV_WIDTH = 3
CONV_HIST = CONV_WIDTH - 1
CHUNK = 128
N_SG_HEADS = 8
EPS = 1e-6
PAST_LEN = 16384

V7X_SUBLANES = 8
V7X_LANES = 128
V7X_BF16_SUBLANES = 16
V7X_MXU_DIM = 256
V7X_VMEM_LIMIT_BYTES = 60000 * 1024

POOL_PAD = -(-POOL_HIST // V7X_SUBLANES) * V7X_SUBLANES
CONV_PAD = -(-CONV_HIST // V7X_SUBLANES) * V7X_SUBLANES

PROMPT_TILE = 512
FETCH_ROWS = 128
FETCH_SLOTS = 4


def _bf16_dot(a, b):
  return jnp.dot(a.astype(jnp.bfloat16), b, preferred_element_type=jnp.float32)


def _rmsnorm(x, g):
  y = x * lax.rsqrt(jnp.mean(x * x, axis=-1, keepdims=True) + EPS)
  return y * g


def _gelu_exact(x):
  return 0.5 * x * (1.0 + lax.erf(x * math.sqrt(0.5)))


def _skew_cols(n):
  return V7X_LANES if (n // V7X_LANES) % V7X_SUBLANES == 0 else 0


def _skewed_dot(a, w_ref, n):
  return jnp.dot(a, w_ref[:, :n], preferred_element_type=jnp.float32)


def _pool_project(d, w_bd_ref, scale):
  halves = []
  for i in range(w_bd_ref.shape[0]):
    lo = i * V7X_MXU_DIM
    halves.append(_bf16_dot(d[:, lo:lo + V7X_MXU_DIM], w_bd_ref[i]))
  return jnp.concatenate(halves, axis=-1) * scale


def _ffn_stages(load_x1, w, emit, final_norm):
  st = {}

  def norm():
    st["x"] = load_x1()
    st["h"] = _rmsnorm(st["x"], w["g_ffn"][...]).astype(jnp.bfloat16)

  def gate_up():
    st["gate"] = jnp.dot(st["h"], w["wg"][...], preferred_element_type=jnp.float32)
    st["up"] = jnp.dot(st["h"], w["wu"][...], preferred_element_type=jnp.float32)

  def activate():
    gate = st["gate"]
    st["act"] = (gate * jax.nn.sigmoid(gate) * st["up"]).astype(jnp.bfloat16)

  def down():
    y = st["x"] + _skewed_dot(st["act"], w["wd"], st["x"].shape[1])
    emit(_rmsnorm(y, w["g_final"][...]) if final_norm else y)

  return [norm, gate_up, activate, down]


def _even_mix_stages(load_x, w, emit, mixers, d_pool, d_conv):
  st = {}

  def norm():
    st["x"] = load_x()
    st["h"] = _rmsnorm(st["x"], w["g_mix"][...]).astype(jnp.bfloat16)

  def in_proj():
    st["z"] = _skewed_dot(st["h"], w["w_in"], d_pool + 3 * d_conv)

  def mix():
    z = st["z"]
    p = z[:, :d_pool]
    xb = z[:, d_pool:d_pool + d_conv]
    bg = z[:, d_pool + d_conv:d_pool + 2 * d_conv]
    cg = z[:, d_pool + 2 * d_conv:]
    pooled, conv_y = mixers(p, cg * xb)
    st["pool_d"] = (pooled - p).astype(jnp.bfloat16)
    st["conv_out"] = (bg * conv_y).astype(jnp.bfloat16)

  def project():
    a_out = _pool_project(st["pool_d"], w["w_pool"], w["pool_scale"][...])
    st["mixed"] = jnp.concatenate([a_out.astype(jnp.bfloat16), st["conv_out"]], axis=-1)

  def out_proj():
    emit(st["x"] + _skewed_dot(st["mixed"], w["w_out"], st["x"].shape[1]))

  return [norm, in_proj, mix, project, out_proj]


def _rows_back(x, k):
  return pltpu.roll(x, k, axis=0)


def _even_prompt_mixers(t, p_hist, c_hist, conv_w, tile):
  assert all(w & (w - 1) == 0 and w <= POOL_PAD for w in POOL_WINDOWS)

  def mixers(p, c):
    pos = t * tile + lax.broadcasted_iota(jnp.int32, (POOL_PAD, POOL_GROUP), 0)
    pooled = []
    for g, win in enumerate(POOL_WINDOWS):
      lo = g * POOL_GROUP
      sums = jnp.concatenate([p_hist[:, lo:lo + POOL_GROUP], p[:, lo:lo + POOL_GROUP]],
                             axis=0)
      span = 1
      while span < win:
        sums = sums + _rows_back(sums, span)
        span *= 2
      sums = sums[POOL_PAD:]
      head = sums[:POOL_PAD] / jnp.minimum(pos + 1, win).astype(jnp.float32)
      pooled.append(jnp.concatenate([head, sums[POOL_PAD:] * (1.0 / win)], axis=0))
    p_hist[...] = p[tile - POOL_PAD:, :]
    c_rows = jnp.concatenate([c_hist[...], c], axis=0)
    y = _rows_back(c_rows, 2)[CONV_PAD:] * conv_w[0:1, :]
    y = y + _rows_back(c_rows, 1)[CONV_PAD:] * conv_w[1:2, :]
    y = y + c * conv_w[2:3, :]
    c_hist[...] = c[tile - CONV_PAD:, :]
    return jnp.concatenate(pooled, axis=-1), y
  return mixers


def _pool_history_sums_kernel(pool_hist_ref, sums_ref):
  for g, win in enumerate(POOL_WINDOWS):
    lo = g * POOL_GROUP
    acc = pool_hist_ref[:, POOL_HIST - 1, lo:lo + POOL_GROUP]
    for k in range(2, win):
      acc = acc + pool_hist_ref[:, POOL_HIST - k, lo:lo + POOL_GROUP]
    sums_ref[:, lo:lo + POOL_GROUP] = acc


def _append_rows_kernel(pool_ref, p_ref, conv_ref, c_ref, new_pool_ref, new_conv_ref):
  for old, row, new in ((pool_ref, p_ref, new_pool_ref), (conv_ref, c_ref, new_conv_ref)):
    n = old.shape[1]
    new[:, 0:n - 1, :] = old[:, 1:n, :]
    new[:, n - 1, :] = row[...]


def _even_sample_mixers(pool_sums_ref, conv_hist_ref, conv_w, p_ref, c_ref, start):
  def mixers(p, c):
    p_ref[...] = p
    c_ref[...] = c
    pooled = []
    for g, win in enumerate(POOL_WINDOWS):
      lo = g * POOL_GROUP
      acc = p[:, lo:lo + POOL_GROUP] + pool_sums_ref[:, lo:lo + POOL_GROUP]
      pooled.append(acc / float(min(start + 1, win)))
    y = conv_hist_ref[:, 0, :] * conv_w[0:1, :]
    y = y + conv_hist_ref[:, 1, :] * conv_w[1:2, :]
    y = y + c * conv_w[2:3, :]
    return jnp.concatenate(pooled, axis=-1), y
  return mixers


def _even_mix_head(t, p_hist, c_hist):
  @pl.when(t == 0)
  def _():
    p_hist[...] = jnp.zeros(p_hist.shape, jnp.float32)
    c_hist[...] = jnp.zeros(c_hist.shape, jnp.float32)


def _even_mix_tail(t, pool_state_ref, conv_state_ref, p_hist, c_hist, *, last_t):
  @pl.when(t == last_t)
  def _():
    pool_state_ref[...] = p_hist[POOL_PAD - POOL_HIST:, :]
    conv_state_ref[...] = c_hist[CONV_PAD - CONV_HIST:, :]


EVEN_SMALL_WEIGHT_NAMES = ("g_mix", "w_pool", "pool_scale", "conv_w", "g_ffn")
EVEN_BIG_WEIGHT_NAMES = ("w_in", "w_out", "wg", "wu", "wd")
ODD_WEIGHT_NAMES = ("g_mix", "w_in", "g_v", "w_s", "b_s_t", "w_out",
                    "g_ffn", "wg", "wu", "wd", "g_final")
N_CAST = 5


def _fetch_bf16(srcs, dsts, stage, sems):
  chunks = []
  for src, dst in zip(srcs, dsts):
    n_rows, n_cols = src.shape
    if dst.shape[1] > n_cols:
      dst[:, n_cols:] = jnp.zeros((n_rows, dst.shape[1] - n_cols), dst.dtype)
    for r0 in range(0, n_rows, FETCH_ROWS):
      chunks.append((src, dst, r0, min(FETCH_ROWS, n_rows - r0), n_cols))

  n_slots = stage.shape[0]

  def copy(i):
    src, _, r0, n, n_cols = chunks[i]
    return pltpu.make_async_copy(src.at[pl.ds(r0, n), :],
                                 stage.at[i % n_slots, pl.ds(0, n), pl.ds(0, n_cols)],
                                 sems.at[i % n_slots])

  for i in range(min(n_slots - 1, len(chunks))):
    copy(i).start()
  for i, (_, dst, r0, n, n_cols) in enumerate(chunks):
    if i + n_slots - 1 < len(chunks):
      copy(i + n_slots - 1).start()
    copy(i).wait()
    dst[pl.ds(r0, n), :n_cols] = stage[i % n_slots, :n, :n_cols].astype(dst.dtype)


def _run(stages):
  for stage in stages:
    stage()


def _pipelined_steps(s, n_tiles, mix_stages, ffn_stages):
  @pl.when(s == 0)
  def _():
    _run(mix_stages())

  @pl.when((s > 0) & (s < n_tiles))
  def _():
    mix, ffn = mix_stages(), ffn_stages()
    _run([mix[0], mix[1], ffn[0], mix[2], ffn[1], mix[3], ffn[2], ffn[3], mix[4]])

  @pl.when(s == n_tiles)
  def _():
    _run(ffn_stages())


def _set(ref):
  def emit(value):
    ref[...] = value
  return emit


def _even_kernel(*refs, n_tiles, tiles_per_seq, tile, start, d_pool, d_conv, layer):
  x_ref, xs_ref, pool_sums_ref, conv_hist_ref = refs[:4]
  n_small, n_big = len(EVEN_SMALL_WEIGHT_NAMES), len(EVEN_BIG_WEIGHT_NAMES)
  w = dict(zip(EVEN_SMALL_WEIGHT_NAMES, refs[4:4 + n_small]))
  big_hbm = refs[4 + n_small:4 + n_small + n_big]
  cast_in = refs[4 + n_small + n_big:4 + n_small + n_big + N_CAST]
  outs = refs[4 + n_small + n_big + N_CAST:]
  y_ref, pool_state_ref, conv_state_ref, ys_ref, ps_ref, cs_ref = outs[:6]
  cast_out = outs[6:6 + N_CAST]
  scratch = outs[6 + N_CAST:]
  p_hist, c_hist, x1_buf = scratch[:3]
  big_vmem = scratch[3:3 + n_big]
  stage, sems = scratch[3 + n_big:]
  w.update(zip(EVEN_BIG_WEIGHT_NAMES, big_vmem))

  s = pl.program_id(0)
  t = s % tiles_per_seq
  slot = s % 2

  @pl.when(s == 0)
  def _():
    _fetch_bf16([m.at[layer[i]] for i, m in enumerate(big_hbm)], big_vmem, stage, sems)

  @pl.when(s < n_tiles)
  def _():
    _even_mix_head(t, p_hist, c_hist)
    for src, dst in zip(cast_in, cast_out):
      n = src.shape[1]
      dst[:, :n] = src[...].astype(dst.dtype)
      if dst.shape[1] > n:
        dst[:, n:] = jnp.zeros((dst.shape[0], dst.shape[1] - n), dst.dtype)

  def mix_stages():
    return _even_mix_stages(
        lambda: x_ref[...], w, _set(x1_buf.at[slot]),
        _even_prompt_mixers(t, p_hist, c_hist, w["conv_w"], tile), d_pool, d_conv)

  def ffn_stages():
    return _ffn_stages(lambda: x1_buf[1 - slot], w, _set(y_ref), False)

  _pipelined_steps(s, n_tiles, mix_stages, ffn_stages)

  @pl.when(s < n_tiles)
  def _():
    _even_mix_tail(t, pool_state_ref, conv_state_ref, p_hist, c_hist,
                   last_t=tiles_per_seq - 1)

  @pl.when(s == n_tiles + 1)
  def _():
    st = {}
    mixers = _even_sample_mixers(pool_sums_ref, conv_hist_ref, w["conv_w"],
                                 ps_ref, cs_ref, start)
    _run(_even_mix_stages(lambda: xs_ref[...], w, lambda v: st.update(x1=v),
                          mixers, d_pool, d_conv))
    _run(_ffn_stages(lambda: st["x1"], w, _set(ys_ref), False))


def _odd_mix_stages(load_x, w, emit, gating, d_gate):
  st = {}

  def norm():
    st["x"] = load_x()
    st["h"] = _rmsnorm(st["x"], w["g_mix"][...]).astype(jnp.bfloat16)

  def in_proj():
    st["z"] = _skewed_dot(st["h"], w["w_in"], 2 * d_gate)

  def activate():
    z = _gelu_exact(st["z"])
    st["u"] = z[:, :d_gate]
    st["v"] = _rmsnorm(z[:, d_gate:], w["g_v"][...])

  def gate():
    st["gated"] = (st["u"] * gating(st["v"])).astype(jnp.bfloat16)

  def out_proj():
    emit(st["x"] + _skewed_dot(st["gated"], w["w_out"], st["x"].shape[1]))

  return [norm, in_proj, activate, gate, out_proj]


def _odd_prompt_gating(w, tile, d_gate):
  def gating(v):
    vb = v.astype(jnp.bfloat16)
    n_chunks = tile // CHUNK
    head = d_gate // N_SG_HEADS
    row = lax.broadcasted_iota(jnp.int32, (CHUNK, CHUNK), 0)
    col = lax.broadcasted_iota(jnp.int32, (CHUNK, CHUNK), 1)
    causal = row >= col
    per_head = []
    for hd in range(N_SG_HEADS):
      w_h = jnp.where(causal, w["w_s"][hd], 0.0).astype(jnp.bfloat16)
      rhs = jnp.concatenate(
          [vb[c * CHUNK:(c + 1) * CHUNK, hd * head:(hd + 1) * head]
           for c in range(n_chunks)], axis=1)
      o = jnp.dot(w_h, rhs, preferred_element_type=jnp.float32)
      per_head.append(o + w["b_s_t"][:, hd:hd + 1])
    return jnp.concatenate(
        [jnp.concatenate([o[:, c * head:(c + 1) * head] for o in per_head], axis=1)
         for c in range(n_chunks)], axis=0)
  return gating


def _odd_sample_gating(w, v_ref, d_gate):
  def gating(v):
    v_ref[...] = v
    head = d_gate // N_SG_HEADS
    rows = v.shape[0]
    vb = v.astype(jnp.bfloat16).astype(jnp.float32)
    per_head = []
    for hd in range(N_SG_HEADS):
      w00 = w["w_s"][hd, 0:1, 0:1].astype(jnp.bfloat16).astype(jnp.float32)
      b0 = w["b_s_t"][0:1, hd:hd + 1]
      per_head.append(vb[:, hd * head:(hd + 1) * head] * jnp.broadcast_to(w00, (rows, head))
                      + jnp.broadcast_to(b0, (rows, head)))
    return jnp.concatenate(per_head, axis=1)
  return gating


def _odd_kernel(*refs, n_tiles, tile, d_gate):
  x_ref, xs_ref = refs[:2]
  n_w = len(ODD_WEIGHT_NAMES)
  w = dict(zip(ODD_WEIGHT_NAMES, refs[2:2 + n_w]))
  y_ref, ys_ref, vs_ref, x1_buf = refs[2 + n_w:]

  s = pl.program_id(0)
  slot = s % 2

  def mix_stages():
    return _odd_mix_stages(lambda: x_ref[...], w, _set(x1_buf.at[slot]),
                           _odd_prompt_gating(w, tile, d_gate), d_gate)

  def ffn_stages():
    return _ffn_stages(lambda: x1_buf[1 - slot], w, _set(y_ref), True)

  _pipelined_steps(s, n_tiles, mix_stages, ffn_stages)

  @pl.when(s == n_tiles + 1)
  def _():
    st = {}
    _run(_odd_mix_stages(lambda: xs_ref[...], w, lambda v: st.update(x1=v),
                         _odd_sample_gating(w, vs_ref, d_gate), d_gate))
    _run(_ffn_stages(lambda: st["x1"], w, _set(ys_ref), True))


def _resident(arr):
  zeros = (0,) * arr.ndim
  return pl.BlockSpec(arr.shape, lambda s: zeros, pipeline_mode=pl.Buffered(1))


def _whole_out(shape):
  zeros = (0,) * len(shape)
  return pl.BlockSpec(shape, lambda s: zeros)


def _params():
  return pltpu.CompilerParams(dimension_semantics=("arbitrary",),
                              vmem_limit_bytes=V7X_VMEM_LIMIT_BYTES)


def _prompt_specs(x, tile):
  batch, seq, d_model = x.shape
  tiles_per_seq = seq // tile
  n_tiles = batch * tiles_per_seq
  mix_tile = lambda s: jnp.clip(s, 0, n_tiles - 1)
  ffn_tile = lambda s: jnp.clip(s - 1, 0, n_tiles - 1)
  block = lambda which: pl.BlockSpec(
      (None, tile, d_model),
      lambda s: (which(s) // tiles_per_seq, which(s) % tiles_per_seq, 0))
  per_seq = lambda rows, ch: pl.BlockSpec(
      (None, rows, ch), lambda s: (mix_tile(s) // tiles_per_seq, 0, 0))
  return n_tiles, tiles_per_seq, mix_tile, block(mix_tile), block(ffn_tile), per_seq


def _cast_slab_spec(rows, cols, n_steps, step, layer):
  slab = rows // n_steps
  steps_per_slab = 1
  while slab % V7X_BF16_SUBLANES:
    slab *= 2
    steps_per_slab *= 2
  src = pl.BlockSpec((None, slab, cols), lambda s: (layer, step(s) // steps_per_slab, 0))
  dst = pl.BlockSpec((slab, cols + _skew_cols(cols)),
                     lambda s: (step(s) // steps_per_slab, 0))
  return src, dst


def _pool_history_sums(pool_hist):
  rows, _, d_pool = pool_hist.shape
  return pl.pallas_call(
      _pool_history_sums_kernel,
      grid=(1,),
      in_specs=[pl.BlockSpec(pool_hist.shape, lambda i: (0, 0, 0))],
      out_specs=pl.BlockSpec((rows, d_pool), lambda i: (0, 0)),
      out_shape=jax.ShapeDtypeStruct((rows, d_pool), pool_hist.dtype),
      name="pool_history_sums",
  )(pool_hist)


def _append_rows(pool_hist, p_new, conv_hist, c_new):
  whole = lambda a: pl.BlockSpec(a.shape, lambda i: (0,) * a.ndim)
  operands = (pool_hist, p_new, conv_hist, c_new)
  return pl.pallas_call(
      _append_rows_kernel,
      grid=(1,),
      in_specs=[whole(a) for a in operands],
      out_specs=[whole(pool_hist), whole(conv_hist)],
      out_shape=[jax.ShapeDtypeStruct(pool_hist.shape, pool_hist.dtype),
                 jax.ShapeDtypeStruct(conv_hist.shape, conv_hist.dtype)],
      name="append_state_rows",
  )(*operands)


def _even_layer(x, xs, pool_sums, conv_hist, small_weights, mats, mat_layer,
                next_mats, next_layer, tile, start):
  n_tiles, tiles_per_seq, step, tok_in, tok_out, per_seq = _prompt_specs(x, tile)
  batch = x.shape[0]
  rows, d_model = xs.shape
  d_pool = pool_sums.shape[-1]
  d_conv = conv_hist.shape[-1]
  cast_specs = [_cast_slab_spec(m.shape[1], m.shape[2], n_tiles, step, next_layer[i])
                for i, m in enumerate(next_mats)]
  f32 = x.dtype
  skewed = lambda m: (m.shape[1], m.shape[2] + _skew_cols(m.shape[2]))
  widest = max(m.shape[2] for m in mats)
  return pl.pallas_call(
      functools.partial(_even_kernel, n_tiles=n_tiles,
                        tiles_per_seq=tiles_per_seq, tile=tile, start=start,
                        d_pool=d_pool, d_conv=d_conv, layer=tuple(mat_layer)),
      grid=(n_tiles + 2,),
      in_specs=([tok_in, _resident(xs), _resident(pool_sums), _resident(conv_hist)]
                + [_resident(w) for w in small_weights]
                + [pl.BlockSpec(memory_space=pl.ANY) for _ in mats]
                + [c[0] for c in cast_specs]),
      out_specs=([tok_out, per_seq(POOL_HIST, d_pool), per_seq(CONV_HIST, d_conv),
                  _whole_out(xs.shape), _whole_out((rows, d_pool)),
                  _whole_out((rows, d_conv))] + [c[1] for c in cast_specs]),
      out_shape=([jax.ShapeDtypeStruct(x.shape, f32),
                  jax.ShapeDtypeStruct((batch, POOL_HIST, d_pool), f32),
                  jax.ShapeDtypeStruct((batch, CONV_HIST, d_conv), f32),
                  jax.ShapeDtypeStruct(xs.shape, f32),
                  jax.ShapeDtypeStruct((rows, d_pool), f32),
                  jax.ShapeDtypeStruct((rows, d_conv), f32)]
                 + [jax.ShapeDtypeStruct(skewed(m), jnp.bfloat16) for m in next_mats]),
      scratch_shapes=([pltpu.VMEM((POOL_PAD, d_pool), jnp.float32),
                       pltpu.VMEM((CONV_PAD, d_conv), jnp.float32),
                       pltpu.VMEM((2, tile, d_model), jnp.float32)]
                      + [pltpu.VMEM(skewed(m), jnp.bfloat16) for m in mats]
                      + [pltpu.VMEM((FETCH_SLOTS, FETCH_ROWS, widest), jnp.float32),
                         pltpu.SemaphoreType.DMA((FETCH_SLOTS,))]),
      compiler_params=_params(),
      name="even_layer",
  )(x, xs, pool_sums, conv_hist, *small_weights, *mats, *next_mats)


def _odd_layer(x, xs, weights, tile):
  n_tiles, _, _, tok_in, tok_out, _ = _prompt_specs(x, tile)
  rows, d_model = xs.shape
  d_gate = weights[ODD_WEIGHT_NAMES.index("g_v")].shape[-1]
  return pl.pallas_call(
      functools.partial(_odd_kernel, n_tiles=n_tiles, tile=tile, d_gate=d_gate),
      grid=(n_tiles + 2,),
      in_specs=[tok_in, _resident(xs)] + [_resident(w) for w in weights],
      out_specs=[tok_out, _whole_out(xs.shape), _whole_out((rows, d_gate))],
      out_shape=[jax.ShapeDtypeStruct(x.shape, x.dtype),
                 jax.ShapeDtypeStruct(xs.shape, x.dtype),
                 jax.ShapeDtypeStruct((rows, d_gate), x.dtype)],
      scratch_shapes=[pltpu.VMEM((2, tile, d_model), jnp.float32)],
      compiler_params=_params(),
      name="odd_layer",
  )(x, xs, *weights)


def _pool_block_diag(w_pool):
  per_tile = V7X_MXU_DIM // POOL_GROUP
  n_tiles = w_pool.shape[0] // per_tile
  out = jnp.zeros((n_tiles, V7X_MXU_DIM, V7X_MXU_DIM), jnp.bfloat16)
  for g in range(w_pool.shape[0]):
    i, j = divmod(g, per_tile)
    lo = j * POOL_GROUP
    out = out.at[i, lo:lo + POOL_GROUP, lo:lo + POOL_GROUP].set(
        w_pool[g].astype(jnp.bfloat16))
  return out


def kernel(x_prompt, x_sample, state_pool, state_conv, norm_mix, norm_ffn, norm_final, w_in_even, w_pool, pool_scale, conv_w, w_out_even, w_in_odd, norm_sg, w_s, b_s, w_out_odd, ffn_w_gate, ffn_w_up, ffn_w_down):
  depth = norm_mix.shape[0]
  assert depth == 2 and w_in_even.shape[0] == 1 and w_in_odd.shape[0] == 1
  assert x_sample.shape[1] == 1
  assert x_prompt.shape[1] % PROMPT_TILE == 0 and PROMPT_TILE % CHUNK == 0

  row = lambda v: v.reshape(1, -1)

  even_small = (row(norm_mix[0]), _pool_block_diag(w_pool[0]), row(pool_scale[0]),
                conv_w[0], row(norm_ffn[0]))
  even_mats = (w_in_even, w_out_even, ffn_w_gate, ffn_w_up, ffn_w_down)
  even_mat_layer = (0, 0, 0, 0, 0)
  odd_mats = (w_in_odd, w_out_odd, ffn_w_gate, ffn_w_up, ffn_w_down)
  odd_mat_layer = (0, 0, 1, 1, 1)

  pool_hist, conv_hist = state_pool[0], state_conv[0]

  (x1_p, pool_p, conv_p, x1_s, p_s, c_s, w_in_o, w_out_o, wg_o, wu_o,
   wd_o) = _even_layer(x_prompt, x_sample[:, 0, :], _pool_history_sums(pool_hist),
                       conv_hist, even_small, even_mats, even_mat_layer,
                       odd_mats, odd_mat_layer, PROMPT_TILE, PAST_LEN)

  odd_w = (row(norm_mix[1]), w_in_o, row(norm_sg[0]), w_s[0], b_s[0].T, w_out_o,
           row(norm_ffn[1]), wg_o, wu_o, wd_o, row(norm_final))
  y_p, y_s, v_s = _odd_layer(x1_p, x1_s, odd_w, PROMPT_TILE)

  pool_s, conv_s = _append_rows(pool_hist, p_s, conv_hist, c_s)

  return (y_p, y_s[:, None, :], pool_p[None], pool_s[None], conv_p[None], conv_s[None],
          v_s[None, :, None, :])
```

```python
import functools
import math

import jax
import jax.numpy as jnp
from jax import lax
from jax.experimental import pallas as pl
from jax.experimental.pallas import tpu as pltpu

POOL_WINDOWS = (2, 4, 8, 16)
POOL_GROUP = 128
POOL_HIST = max(POOL_WINDOWS) - 1
CONV_WIDTH = 3
CONV_HIST = CONV_WIDTH - 1
CHUNK = 128
N_SG_HEADS = 8
EPS = 1e-6
PAST_LEN = 16384

V7X_SUBLANES = 8
V7X_LANES = 128
V7X_BF16_SUBLANES = 16
V7X_MXU_DIM = 256
V7X_VMEM_LIMIT_BYTES = 60000 * 1024

POOL_PAD = -(-POOL_HIST // V7X_SUBLANES) * V7X_SUBLANES
CONV_PAD = -(-CONV_HIST // V7X_SUBLANES) * V7X_SUBLANES

PROMPT_TILE = 512
FETCH_ROWS = 128
FETCH_SLOTS = 4


def _bf16_dot(a, b):
  return jnp.dot(a.astype(jnp.bfloat16), b, preferred_element_type=jnp.float32)


def _rmsnorm(x, g):
  y = x * lax.rsqrt(jnp.mean(x * x, axis=-1, keepdims=True) + EPS)
  return y * g


def _gelu_exact(x):
  return 0.5 * x * (1.0 + lax.erf(x * math.sqrt(0.5)))


def _skew_cols(n):
  return V7X_LANES if (n // V7X_LANES) % V7X_SUBLANES == 0 else 0


def _skewed_dot(a, w_ref, n):
  return jnp.dot(a, w_ref[:, :n], preferred_element_type=jnp.float32)


def _pool_project(d, w_bd_ref, scale):
  halves = []
  for i in range(w_bd_ref.shape[0]):
    lo = i * V7X_MXU_DIM
    halves.append(_bf16_dot(d[:, lo:lo + V7X_MXU_DIM], w_bd_ref[i]))
  return jnp.concatenate(halves, axis=-1) * scale


def _ffn_stages(load_x1, w, emit, final_norm):
  st = {}

  def norm():
    st["x"] = load_x1()
    st["h"] = _rmsnorm(st["x"], w["g_ffn"][...]).astype(jnp.bfloat16)

  def gate_up():
    st["gate"] = jnp.dot(st["h"], w["wg"][...], preferred_element_type=jnp.float32)
    st["up"] = jnp.dot(st["h"], w["wu"][...], preferred_element_type=jnp.float32)

  def activate():
    gate = st["gate"]
    st["act"] = (gate * jax.nn.sigmoid(gate) * st["up"]).astype(jnp.bfloat16)

  def down():
    y = st["x"] + _skewed_dot(st["act"], w["wd"], st["x"].shape[1])
    emit(_rmsnorm(y, w["g_final"][...]) if final_norm else y)

  return [norm, gate_up, activate, down]


def _even_mix_stages(load_x, w, emit, mixers, d_pool, d_conv):
  st = {}

  def norm():
    st["x"] = load_x()
    st["h"] = _rmsnorm(st["x"], w["g_mix"][...]).astype(jnp.bfloat16)

  def in_proj():
    st["z"] = _skewed_dot(st["h"], w["w_in"], d_pool + 3 * d_conv)

  def mix():
    z = st["z"]
    p = z[:, :d_pool]
    xb = z[:, d_pool:d_pool + d_conv]
    bg = z[:, d_pool + d_conv:d_pool + 2 * d_conv]
    cg = z[:, d_pool + 2 * d_conv:]
    pooled, conv_y = mixers(p, cg * xb)
    st["pool_d"] = (pooled - p).astype(jnp.bfloat16)
    st["conv_out"] = (bg * conv_y).astype(jnp.bfloat16)

  def project():
    a_out = _pool_project(st["pool_d"], w["w_pool"], w["pool_scale"][...])
    st["mixed"] = jnp.concatenate([a_out.astype(jnp.bfloat16), st["conv_out"]], axis=-1)

  def out_proj():
    emit(st["x"] + _skewed_dot(st["mixed"], w["w_out"], st["x"].shape[1]))

  return [norm, in_proj, mix, project, out_proj]


def _rows_back(x, k):
  return pltpu.roll(x, k, axis=0)


def _even_prompt_mixers(t, p_hist, c_hist, conv_w, tile):
  assert all(w & (w - 1) == 0 and w <= POOL_PAD for w in POOL_WINDOWS)

  def mixers(p, c):
    pos = t * tile + lax.broadcasted_iota(jnp.int32, (POOL_PAD, POOL_GROUP), 0)
    pooled = []
    for g, win in enumerate(POOL_WINDOWS):
      lo = g * POOL_GROUP
      sums = jnp.concatenate([p_hist[:, lo:lo + POOL_GROUP], p[:, lo:lo + POOL_GROUP]],
                             axis=0)
      span = 1
      while span < win:
        sums = sums + _rows_back(sums, span)
        span *= 2
      sums = sums[POOL_PAD:]
      head = sums[:POOL_PAD] / jnp.minimum(pos + 1, win).astype(jnp.float32)
      pooled.append(jnp.concatenate([head, sums[POOL_PAD:] * (1.0 / win)], axis=0))
    p_hist[...] = p[tile - POOL_PAD:, :]
    c_rows = jnp.concatenate([c_hist[...], c], axis=0)
    y = _rows_back(c_rows, 2)[CONV_PAD:] * conv_w[0:1, :]
    y = y + _rows_back(c_rows, 1)[CONV_PAD:] * conv_w[1:2, :]
    y = y + c * conv_w[2:3, :]
    c_hist[...] = c[tile - CONV_PAD:, :]
    return jnp.concatenate(pooled, axis=-1), y
  return mixers


def _pool_history_sums_kernel(pool_hist_ref, sums_ref):
  for g, win in enumerate(POOL_WINDOWS):
    lo = g * POOL_GROUP
    acc = pool_hist_ref[POOL_HIST - 1, :, lo:lo + POOL_GROUP]
    for k in range(2, win):
      acc = acc + pool_hist_ref[POOL_HIST - k, :, lo:lo + POOL_GROUP]
    sums_ref[:, lo:lo + POOL_GROUP] = acc


def _even_sample_mixers(pool_sums_ref, conv_hist_ref, conv_w, p_ref, c_ref, start):
  def mixers(p, c):
    p_ref[...] = p
    c_ref[...] = c
    pooled = []
    for g, win in enumerate(POOL_WINDOWS):
      lo = g * POOL_GROUP
      acc = p[:, lo:lo + POOL_GROUP] + pool_sums_ref[:, lo:lo + POOL_GROUP]
      pooled.append(acc / float(min(start + 1, win)))
    y = conv_hist_ref[0] * conv_w[0:1, :]
    y = y + conv_hist_ref[1] * conv_w[1:2, :]
    y = y + c * conv_w[2:3, :]
    return jnp.concatenate(pooled, axis=-1), y
  return mixers


def _even_mix_head(t, p_hist, c_hist):
  @pl.when(t == 0)
  def _():
    p_hist[...] = jnp.zeros(p_hist.shape, jnp.float32)
    c_hist[...] = jnp.zeros(c_hist.shape, jnp.float32)


def _even_mix_tail(t, pool_state_ref, conv_state_ref, p_hist, c_hist, *, last_t):
  @pl.when(t == last_t)
  def _():
    pool_state_ref[...] = p_hist[POOL_PAD - POOL_HIST:, :]
    conv_state_ref[...] = c_hist[CONV_PAD - CONV_HIST:, :]


EVEN_SMALL_WEIGHT_NAMES = ("g_mix", "w_pool", "pool_scale", "conv_w", "g_ffn")
EVEN_BIG_WEIGHT_NAMES = ("w_in", "w_out", "wg", "wu", "wd")
ODD_WEIGHT_NAMES = ("g_mix", "w_in", "g_v", "w_s", "b_s_t", "w_out",
                    "g_ffn", "wg", "wu", "wd", "g_final")
N_CAST = 5


def _fetch_bf16(srcs, dsts, stage, sems):
  chunks = []
  for src, dst in zip(srcs, dsts):
    n_rows, n_cols = src.shape
    if dst.shape[1] > n_cols:
      dst[:, n_cols:] = jnp.zeros((n_rows, dst.shape[1] - n_cols), dst.dtype)
    for r0 in range(0, n_rows, FETCH_ROWS):
      chunks.append((src, dst, r0, min(FETCH_ROWS, n_rows - r0), n_cols))

  n_slots = stage.shape[0]

  def copy(i):
    src, _, r0, n, n_cols = chunks[i]
    return pltpu.make_async_copy(src.at[pl.ds(r0, n), :],
                                 stage.at[i % n_slots, pl.ds(0, n), pl.ds(0, n_cols)],
                                 sems.at[i % n_slots])

  for i in range(min(n_slots - 1, len(chunks))):
    copy(i).start()
  for i, (_, dst, r0, n, n_cols) in enumerate(chunks):
    if i + n_slots - 1 < len(chunks):
      copy(i + n_slots - 1).start()
    copy(i).wait()
    dst[pl.ds(r0, n), :n_cols] = stage[i % n_slots, :n, :n_cols].astype(dst.dtype)


def _run(stages):
  for stage in stages:
    stage()


def _pipelined_steps(s, n_tiles, mix_stages, ffn_stages):
  @pl.when(s == 0)
  def _():
    _run(mix_stages())

  @pl.when((s > 0) & (s < n_tiles))
  def _():
    mix, ffn = mix_stages(), ffn_stages()
    _run([mix[0], mix[1], ffn[0], mix[2], ffn[1], mix[3], ffn[2], ffn[3], mix[4]])

  @pl.when(s == n_tiles)
  def _():
    _run(ffn_stages())


def _set(ref):
  def emit(value):
    ref[...] = value
  return emit


def _set_rows(ref):
  def emit(value):
    ref[:, 0, :] = value
  return emit


def _even_kernel(*refs, n_tiles, tiles_per_seq, tile, start, d_pool, d_conv, layer):
  x_ref, xs_ref, pool_sums_ref, conv_hist_ref = refs[:4]
  n_small, n_big = len(EVEN_SMALL_WEIGHT_NAMES), len(EVEN_BIG_WEIGHT_NAMES)
  w = dict(zip(EVEN_SMALL_WEIGHT_NAMES, refs[4:4 + n_small]))
  big_hbm = refs[4 + n_small:4 + n_small + n_big]
  cast_in = refs[4 + n_small + n_big:4 + n_small + n_big + N_CAST]
  outs = refs[4 + n_small + n_big + N_CAST:]
  y_ref, pool_state_ref, conv_state_ref, ys_ref, ps_ref, cs_ref = outs[:6]
  cast_out = outs[6:6 + N_CAST]
  scratch = outs[6 + N_CAST:]
  p_hist, c_hist, x1_buf = scratch[:3]
  big_vmem = scratch[3:3 + n_big]
  stage, sems = scratch[3 + n_big:]
  w.update(zip(EVEN_BIG_WEIGHT_NAMES, big_vmem))

  s = pl.program_id(0)
  t = s % tiles_per_seq
  slot = s % 2

  @pl.when(s == 0)
  def _():
    _fetch_bf16([m.at[layer[i]] for i, m in enumerate(big_hbm)], big_vmem, stage, sems)

  @pl.when(s < n_tiles)
  def _():
    _even_mix_head(t, p_hist, c_hist)
    for src, dst in zip(cast_in, cast_out):
      n = src.shape[1]
      dst[:, :n] = src[...].astype(dst.dtype)
      if dst.shape[1] > n:
        dst[:, n:] = jnp.zeros((dst.shape[0], dst.shape[1] - n), dst.dtype)

  def mix_stages():
    return _even_mix_stages(
        lambda: x_ref[...], w, _set(x1_buf.at[slot]),
        _even_prompt_mixers(t, p_hist, c_hist, w["conv_w"], tile), d_pool, d_conv)

  def ffn_stages():
    return _ffn_stages(lambda: x1_buf[1 - slot], w, _set(y_ref), False)

  _pipelined_steps(s, n_tiles, mix_stages, ffn_stages)

  @pl.when(s < n_tiles)
  def _():
    _even_mix_tail(t, pool_state_ref, conv_state_ref, p_hist, c_hist,
                   last_t=tiles_per_seq - 1)

  @pl.when(s == n_tiles + 1)
  def _():
    st = {}
    mixers = _even_sample_mixers(pool_sums_ref, conv_hist_ref, w["conv_w"],
                                 ps_ref, cs_ref, start)
    _run(_even_mix_stages(lambda: xs_ref[:, 0, :], w, lambda v: st.update(x1=v),
                          mixers, d_pool, d_conv))
    _run(_ffn_stages(lambda: st["x1"], w, _set(ys_ref), False))


def _odd_mix_stages(load_x, w, emit, gating, d_gate):
  st = {}

  def norm():
    st["x"] = load_x()
    st["h"] = _rmsnorm(st["x"], w["g_mix"][...]).astype(jnp.bfloat16)

  def in_proj():
    st["z"] = _skewed_dot(st["h"], w["w_in"], 2 * d_gate)

  def activate():
    z = _gelu_exact(st["z"])
    st["u"] = z[:, :d_gate]
    st["v"] = _rmsnorm(z[:, d_gate:], w["g_v"][...])

  def gate():
    st["gated"] = (st["u"] * gating(st["v"])).astype(jnp.bfloat16)

  def out_proj():
    emit(st["x"] + _skewed_dot(st["gated"], w["w_out"], st["x"].shape[1]))

  return [norm, in_proj, activate, gate, out_proj]


def _odd_prompt_gating(w, tile, d_gate):
  def gating(v):
    vb = v.astype(jnp.bfloat16)
    n_chunks = tile // CHUNK
    head = d_gate // N_SG_HEADS
    row = lax.broadcasted_iota(jnp.int32, (CHUNK, CHUNK), 0)
    col = lax.broadcasted_iota(jnp.int32, (CHUNK, CHUNK), 1)
    causal = row >= col
    per_head = []
    for hd in range(N_SG_HEADS):
      w_h = jnp.where(causal, w["w_s"][hd], 0.0).astype(jnp.bfloat16)
      rhs = jnp.concatenate(
          [vb[c * CHUNK:(c + 1) * CHUNK, hd * head:(hd + 1) * head]
           for c in range(n_chunks)], axis=1)
      o = jnp.dot(w_h, rhs, preferred_element_type=jnp.float32)
      per_head.append(o + w["b_s_t"][:, hd:hd + 1])
    return jnp.concatenate(
        [jnp.concatenate([o[:, c * head:(c + 1) * head] for o in per_head], axis=1)
         for c in range(n_chunks)], axis=0)
  return gating


def _odd_sample_gating(w, v_ref, d_gate):
  def gating(v):
    v_ref[:, 0, :] = v
    head = d_gate // N_SG_HEADS
    rows = v.shape[0]
    vb = v.astype(jnp.bfloat16).astype(jnp.float32)
    per_head = []
    for hd in range(N_SG_HEADS):
      w00 = w["w_s"][hd, 0:1, 0:1].astype(jnp.bfloat16).astype(jnp.float32)
      b0 = w["b_s_t"][0:1, hd:hd + 1]
      per_head.append(vb[:, hd * head:(hd + 1) * head] * jnp.broadcast_to(w00, (rows, head))
                      + jnp.broadcast_to(b0, (rows, head)))
    return jnp.concatenate(per_head, axis=1)
  return gating


def _odd_kernel(*refs, n_tiles, tile, d_gate):
  x_ref, xs_ref = refs[:2]
  n_w = len(ODD_WEIGHT_NAMES)
  w = dict(zip(ODD_WEIGHT_NAMES, refs[2:2 + n_w]))
  y_ref, ys_ref, vs_ref, x1_buf = refs[2 + n_w:]

  s = pl.program_id(0)
  slot = s % 2

  def mix_stages():
    return _odd_mix_stages(lambda: x_ref[...], w, _set(x1_buf.at[slot]),
                           _odd_prompt_gating(w, tile, d_gate), d_gate)

  def ffn_stages():
    return _ffn_stages(lambda: x1_buf[1 - slot], w, _set(y_ref), True)

  _pipelined_steps(s, n_tiles, mix_stages, ffn_stages)

  @pl.when(s == n_tiles + 1)
  def _():
    st = {}
    _run(_odd_mix_stages(lambda: xs_ref[...], w, lambda v: st.update(x1=v),
                         _odd_sample_gating(w, vs_ref, d_gate), d_gate))
    _run(_ffn_stages(lambda: st["x1"], w, _set_rows(ys_ref), True))


def _resident(arr):
  zeros = (0,) * arr.ndim
  return pl.BlockSpec(arr.shape, lambda s: zeros, pipeline_mode=pl.Buffered(1))


def _whole_out(shape):
  zeros = (0,) * len(shape)
  return pl.BlockSpec(shape, lambda s: zeros)


def _params():
  return pltpu.CompilerParams(dimension_semantics=("arbitrary",),
                              vmem_limit_bytes=V7X_VMEM_LIMIT_BYTES)


def _prompt_specs(x, tile):
  batch, seq, d_model = x.shape
  tiles_per_seq = seq // tile
  n_tiles = batch * tiles_per_seq
  mix_tile = lambda s: jnp.clip(s, 0, n_tiles - 1)
  ffn_tile = lambda s: jnp.clip(s - 1, 0, n_tiles - 1)
  block = lambda which: pl.BlockSpec(
      (None, tile, d_model),
      lambda s: (which(s) // tiles_per_seq, which(s) % tiles_per_seq, 0))
  per_seq = lambda rows, ch: pl.BlockSpec(
      (None, rows, ch), lambda s: (mix_tile(s) // tiles_per_seq, 0, 0))
  return n_tiles, tiles_per_seq, mix_tile, block(mix_tile), block(ffn_tile), per_seq


def _cast_slab_spec(rows, cols, n_steps, step, layer):
  slab = rows // n_steps
  steps_per_slab = 1
  while slab % V7X_BF16_SUBLANES:
    slab *= 2
    steps_per_slab *= 2
  src = pl.BlockSpec((None, slab, cols), lambda s: (layer, step(s) // steps_per_slab, 0))
  dst = pl.BlockSpec((slab, cols + _skew_cols(cols)),
                     lambda s: (step(s) // steps_per_slab, 0))
  return src, dst


def _pool_history_sums(pool_hist):
  rows, d_pool = pool_hist.shape[1:]
  return pl.pallas_call(
      _pool_history_sums_kernel,
      grid=(1,),
      in_specs=[pl.BlockSpec(pool_hist.shape, lambda i: (0, 0, 0))],
      out_specs=pl.BlockSpec((rows, d_pool), lambda i: (0, 0)),
      out_shape=jax.ShapeDtypeStruct((rows, d_pool), pool_hist.dtype),
      name="pool_history_sums",
  )(pool_hist)


def _even_layer(x, xs, pool_sums, conv_hist, small_weights, mats, mat_layer,
                next_mats, next_layer, tile, start):
  n_tiles, tiles_per_seq, step, tok_in, tok_out, per_seq = _prompt_specs(x, tile)
  batch = x.shape[0]
  rows, _, d_model = xs.shape
  d_pool = pool_sums.shape[-1]
  d_conv = conv_hist.shape[-1]
  cast_specs = [_cast_slab_spec(m.shape[1], m.shape[2], n_tiles, step, next_layer[i])
                for i, m in enumerate(next_mats)]
  f32 = x.dtype
  skewed = lambda m: (m.shape[1], m.shape[2] + _skew_cols(m.shape[2]))
  widest = max(m.shape[2] for m in mats)
  return pl.pallas_call(
      functools.partial(_even_kernel, n_tiles=n_tiles,
                        tiles_per_seq=tiles_per_seq, tile=tile, start=start,
                        d_pool=d_pool, d_conv=d_conv, layer=tuple(mat_layer)),
      grid=(n_tiles + 2,),
      in_specs=([tok_in, _resident(xs), _resident(pool_sums), _resident(conv_hist)]
                + [_resident(w) for w in small_weights]
                + [pl.BlockSpec(memory_space=pl.ANY) for _ in mats]
                + [c[0] for c in cast_specs]),
      out_specs=([tok_out, per_seq(POOL_HIST, d_pool), per_seq(CONV_HIST, d_conv),
                  _whole_out((rows, d_model)), _whole_out((rows, d_pool)),
                  _whole_out((rows, d_conv))] + [c[1] for c in cast_specs]),
      out_shape=([jax.ShapeDtypeStruct(x.shape, f32),
                  jax.ShapeDtypeStruct((batch, POOL_HIST, d_pool), f32),
                  jax.ShapeDtypeStruct((batch, CONV_HIST, d_conv), f32),
                  jax.ShapeDtypeStruct((rows, d_model), f32),
                  jax.ShapeDtypeStruct((rows, d_pool), f32),
                  jax.ShapeDtypeStruct((rows, d_conv), f32)]
                 + [jax.ShapeDtypeStruct(skewed(m), jnp.bfloat16) for m in next_mats]),
      scratch_shapes=([pltpu.VMEM((POOL_PAD, d_pool), jnp.float32),
                       pltpu.VMEM((CONV_PAD, d_conv), jnp.float32),
                       pltpu.VMEM((2, tile, d_model), jnp.float32)]
                      + [pltpu.VMEM(skewed(m), jnp.bfloat16) for m in mats]
                      + [pltpu.VMEM((FETCH_SLOTS, FETCH_ROWS, widest), jnp.float32),
                         pltpu.SemaphoreType.DMA((FETCH_SLOTS,))]),
      compiler_params=_params(),
      name="even_layer",
  )(x, xs, pool_sums, conv_hist, *small_weights, *mats, *next_mats)


def _odd_layer(x, xs, weights, tile):
  n_tiles, _, _, tok_in, tok_out, _ = _prompt_specs(x, tile)
  rows, d_model = xs.shape
  d_gate = weights[ODD_WEIGHT_NAMES.index("g_v")].shape[-1]
  return pl.pallas_call(
      functools.partial(_odd_kernel, n_tiles=n_tiles, tile=tile, d_gate=d_gate),
      grid=(n_tiles + 2,),
      in_specs=[tok_in, _resident(xs)] + [_resident(w) for w in weights],
      out_specs=[tok_out, _whole_out((rows, 1, d_model)), _whole_out((rows, 1, d_gate))],
      out_shape=[jax.ShapeDtypeStruct(x.shape, x.dtype),
                 jax.ShapeDtypeStruct((rows, 1, d_model), x.dtype),
                 jax.ShapeDtypeStruct((rows, 1, d_gate), x.dtype)],
      scratch_shapes=[pltpu.VMEM((2, tile, d_model), jnp.float32)],
      compiler_params=_params(),
      name="odd_layer",
  )(x, xs, *weights)


def _pool_block_diag(w_pool):
  per_tile = V7X_MXU_DIM // POOL_GROUP
  n_tiles = w_pool.shape[0] // per_tile
  out = jnp.zeros((n_tiles, V7X_MXU_DIM, V7X_MXU_DIM), jnp.bfloat16)
  for g in range(w_pool.shape[0]):
    i, j = divmod(g, per_tile)
    lo = j * POOL_GROUP
    out = out.at[i, lo:lo + POOL_GROUP, lo:lo + POOL_GROUP].set(
        w_pool[g].astype(jnp.bfloat16))
  return out


def kernel(x_prompt, x_sample, state_pool, state_conv, norm_mix, norm_ffn, norm_final, w_in_even, w_pool, pool_scale, conv_w, w_out_even, w_in_odd, norm_sg, w_s, b_s, w_out_odd, ffn_w_gate, ffn_w_up, ffn_w_down):
  depth = norm_mix.shape[0]
  assert depth == 2 and w_in_even.shape[0] == 1 and w_in_odd.shape[0] == 1
  assert x_sample.shape[1] == 1
  assert x_prompt.shape[1] % PROMPT_TILE == 0 and PROMPT_TILE % CHUNK == 0

  row = lambda v: v.reshape(1, -1)

  even_small = (row(norm_mix[0]), _pool_block_diag(w_pool[0]), row(pool_scale[0]),
                conv_w[0], row(norm_ffn[0]))
  even_mats = (w_in_even, w_out_even, ffn_w_gate, ffn_w_up, ffn_w_down)
  even_mat_layer = (0, 0, 0, 0, 0)
  odd_mats = (w_in_odd, w_out_odd, ffn_w_gate, ffn_w_up, ffn_w_down)
  odd_mat_layer = (0, 0, 1, 1, 1)

  pool_hist = jnp.swapaxes(state_pool[0], 0, 1)
  conv_hist = jnp.swapaxes(state_conv[0], 0, 1)

  (x1_p, pool_p, conv_p, x1_s, p_s, c_s, w_in_o, w_out_o, wg_o, wu_o,
   wd_o) = _even_layer(x_prompt, x_sample, _pool_history_sums(pool_hist),
                       conv_hist, even_small, even_mats, even_mat_layer,
                       odd_mats, odd_mat_layer, PROMPT_TILE, PAST_LEN)

  odd_w = (row(norm_mix[1]), w_in_o, row(norm_sg[0]), w_s[0], b_s[0].T, w_out_o,
           row(norm_ffn[1]), wg_o, wu_o, wd_o, row(norm_final))
  y_p, y_s, v_s = _odd_layer(x1_p, x1_s, odd_w, PROMPT_TILE)

  pool_s = jnp.concatenate([state_pool[:, :, 1:], p_s[None, :, None, :]], axis=2)
  conv_s = jnp.concatenate([state_conv[:, :, 1:], c_s[None, :, None, :]], axis=2)

  return (y_p, y_s, pool_p[None], pool_s, conv_p[None], conv_s, v_s[None])
```

```python
import functools
import math

import jax
import jax.numpy as jnp
from jax import lax
from jax.experimental import pallas as pl
from jax.experimental.pallas import tpu as pltpu

POOL_WINDOWS = (2, 4, 8, 16)
POOL_GROUP = 128
POOL_HIST = max(POOL_WINDOWS) - 1
CONV_WIDTH = 3
CONV_HIST = CONV_WIDTH - 1
CHUNK = 128
N_SG_HEADS = 8
EPS = 1e-6
PAST_LEN = 16384

V7X_SUBLANES = 8
V7X_LANES = 128
V7X_BF16_SUBLANES = 16
V7X_MXU_DIM = 256
V7X_VMEM_LIMIT_BYTES = 60000 * 1024

POOL_PAD = -(-POOL_HIST // V7X_SUBLANES) * V7X_SUBLANES
CONV_PAD = -(-CONV_HIST // V7X_SUBLANES) * V7X_SUBLANES

PROMPT_TILE = 512
FETCH_ROWS = 128
FETCH_SLOTS = 4


def _bf16_dot(a, b):
  return jnp.dot(a.astype(jnp.bfloat16), b, preferred_element_type=jnp.float32)


def _rmsnorm(x, g):
  y = x * lax.rsqrt(jnp.mean(x * x, axis=-1, keepdims=True) + EPS)
  return y * g


def _gelu_exact(x):
  return 0.5 * x * (1.0 + lax.erf(x * math.sqrt(0.5)))


def _skew_cols(n):
  return V7X_LANES if (n // V7X_LANES) % V7X_SUBLANES == 0 else 0


def _skewed_dot(a, w_ref, n):
  return jnp.dot(a, w_ref[:, :n], preferred_element_type=jnp.float32)


def _pool_project(d, w_bd_ref, scale):
  halves = []
  for i in range(w_bd_ref.shape[0]):
    lo = i * V7X_MXU_DIM
    halves.append(_bf16_dot(d[:, lo:lo + V7X_MXU_DIM], w_bd_ref[i]))
  return jnp.concatenate(halves, axis=-1) * scale


def _ffn_stages(load_x1, w, emit, final_norm):
  st = {}

  def norm():
    st["x"] = load_x1()
    st["h"] = _rmsnorm(st["x"], w["g_ffn"][...]).astype(jnp.bfloat16)

  def gate_up():
    st["gate"] = jnp.dot(st["h"], w["wg"][...], preferred_element_type=jnp.float32)
    st["up"] = jnp.dot(st["h"], w["wu"][...], preferred_element_type=jnp.float32)

  def activate():
    gate = st["gate"]
    st["act"] = (gate * jax.nn.sigmoid(gate) * st["up"]).astype(jnp.bfloat16)

  def down():
    y = st["x"] + _skewed_dot(st["act"], w["wd"], st["x"].shape[1])
    emit(_rmsnorm(y, w["g_final"][...]) if final_norm else y)

  return [norm, gate_up, activate, down]


def _even_mix_stages(load_x, w, emit, mixers, d_pool, d_conv):
  st = {}

  def norm():
    st["x"] = load_x()
    st["h"] = _rmsnorm(st["x"], w["g_mix"][...]).astype(jnp.bfloat16)

  def in_proj():
    st["z"] = _skewed_dot(st["h"], w["w_in"], d_pool + 3 * d_conv)

  def mix():
    z = st["z"]
    p = z[:, :d_pool]
    xb = z[:, d_pool:d_pool + d_conv]
    bg = z[:, d_pool + d_conv:d_pool + 2 * d_conv]
    cg = z[:, d_pool + 2 * d_conv:]
    pooled, conv_y = mixers(p, cg * xb)
    st["pool_d"] = (pooled - p).astype(jnp.bfloat16)
    st["conv_out"] = (bg * conv_y).astype(jnp.bfloat16)

  def project():
    a_out = _pool_project(st["pool_d"], w["w_pool"], w["pool_scale"][...])
    st["mixed"] = jnp.concatenate([a_out.astype(jnp.bfloat16), st["conv_out"]], axis=-1)

  def out_proj():
    emit(st["x"] + _skewed_dot(st["mixed"], w["w_out"], st["x"].shape[1]))

  return [norm, in_proj, mix, project, out_proj]


def _rows_back(x, k):
  return pltpu.roll(x, k, axis=0)


def _even_prompt_mixers(t, p_hist, c_hist, conv_w, tile):
  assert all(w & (w - 1) == 0 and w <= POOL_PAD for w in POOL_WINDOWS)

  def mixers(p, c):
    pos = t * tile + lax.broadcasted_iota(jnp.int32, (POOL_PAD, POOL_GROUP), 0)
    pooled = []
    for g, win in enumerate(POOL_WINDOWS):
      lo = g * POOL_GROUP
      sums = jnp.concatenate([p_hist[:, lo:lo + POOL_GROUP], p[:, lo:lo + POOL_GROUP]],
                             axis=0)
      span = 1
      while span < win:
        sums = sums + _rows_back(sums, span)
        span *= 2
      sums = sums[POOL_PAD:]
      head = sums[:POOL_PAD] / jnp.minimum(pos + 1, win).astype(jnp.float32)
      pooled.append(jnp.concatenate([head, sums[POOL_PAD:] * (1.0 / win)], axis=0))
    p_hist[...] = p[tile - POOL_PAD:, :]
    c_rows = jnp.concatenate([c_hist[...], c], axis=0)
    y = _rows_back(c_rows, 2)[CONV_PAD:] * conv_w[0:1, :]
    y = y + _rows_back(c_rows, 1)[CONV_PAD:] * conv_w[1:2, :]
    y = y + c * conv_w[2:3, :]
    c_hist[...] = c[tile - CONV_PAD:, :]
    return jnp.concatenate(pooled, axis=-1), y
  return mixers


def _pool_history_sums_kernel(pool_hist_ref, sums_ref):
  for g, win in enumerate(POOL_WINDOWS):
    lo = g * POOL_GROUP
    acc = pool_hist_ref[POOL_HIST - 1, :, lo:lo + POOL_GROUP]
    for k in range(2, win):
      acc = acc + pool_hist_ref[POOL_HIST - k, :, lo:lo + POOL_GROUP]
    sums_ref[:, lo:lo + POOL_GROUP] = acc


def _even_sample_mixers(pool_sums_ref, conv_hist_ref, conv_w, p_ref, c_ref, start):
  def mixers(p, c):
    p_ref[...] = p
    c_ref[...] = c
    pooled = []
    for g, win in enumerate(POOL_WINDOWS):
      lo = g * POOL_GROUP
      acc = p[:, lo:lo + POOL_GROUP] + pool_sums_ref[:, lo:lo + POOL_GROUP]
      pooled.append(acc / float(min(start + 1, win)))
    y = conv_hist_ref[0] * conv_w[0:1, :]
    y = y + conv_hist_ref[1] * conv_w[1:2, :]
    y = y + c * conv_w[2:3, :]
    return jnp.concatenate(pooled, axis=-1), y
  return mixers


def _even_mix_head(t, p_hist, c_hist):
  @pl.when(t == 0)
  def _():
    p_hist[...] = jnp.zeros(p_hist.shape, jnp.float32)
    c_hist[...] = jnp.zeros(c_hist.shape, jnp.float32)


def _even_mix_tail(t, pool_state_ref, conv_state_ref, p_hist, c_hist, *, last_t):
  @pl.when(t == last_t)
  def _():
    pool_state_ref[...] = p_hist[POOL_PAD - POOL_HIST:, :]
    conv_state_ref[...] = c_hist[CONV_PAD - CONV_HIST:, :]


EVEN_SMALL_WEIGHT_NAMES = ("g_mix", "w_pool", "pool_scale", "conv_w", "g_ffn")
EVEN_BIG_WEIGHT_NAMES = ("w_in", "w_out", "wg", "wu", "wd")
ODD_WEIGHT_NAMES = ("g_mix", "w_in", "g_v", "w_s", "b_s_t", "w_out",
                    "g_ffn", "wg", "wu", "wd", "g_final")
N_CAST = 5


def _fetch_bf16(srcs, dsts, stage, sems):
  chunks = []
  for src, dst in zip(srcs, dsts):
    n_rows, n_cols = src.shape
    if dst.shape[1] > n_cols:
      dst[:, n_cols:] = jnp.zeros((n_rows, dst.shape[1] - n_cols), dst.dtype)
    for r0 in range(0, n_rows, FETCH_ROWS):
      chunks.append((src, dst, r0, min(FETCH_ROWS, n_rows - r0), n_cols))

  n_slots = stage.shape[0]

  def copy(i):
    src, _, r0, n, n_cols = chunks[i]
    return pltpu.make_async_copy(src.at[pl.ds(r0, n), :],
                                 stage.at[i % n_slots, pl.ds(0, n), pl.ds(0, n_cols)],
                                 sems.at[i % n_slots])

  for i in range(min(n_slots - 1, len(chunks))):
    copy(i).start()
  for i, (_, dst, r0, n, n_cols) in enumerate(chunks):
    if i + n_slots - 1 < len(chunks):
      copy(i + n_slots - 1).start()
    copy(i).wait()
    dst[pl.ds(r0, n), :n_cols] = stage[i % n_slots, :n, :n_cols].astype(dst.dtype)


def _run(stages):
  for stage in stages:
    stage()


def _pipelined_steps(s, n_tiles, mix_stages, ffn_stages):
  @pl.when(s == 0)
  def _():
    _run(mix_stages())

  @pl.when((s > 0) & (s < n_tiles))
  def _():
    mix, ffn = mix_stages(), ffn_stages()
    _run([ffn[0], ffn[1], mix[0], mix[1], mix[2], ffn[2], ffn[3], mix[3], mix[4]])

  @pl.when(s == n_tiles)
  def _():
    _run(ffn_stages())


def _set(ref):
  def emit(value):
    ref[...] = value
  return emit


def _set_rows(ref):
  def emit(value):
    ref[:, 0, :] = value
  return emit


def _even_kernel(*refs, n_tiles, tiles_per_seq, tile, start, d_pool, d_conv, layer):
  x_ref, xs_ref, pool_sums_ref, conv_hist_ref = refs[:4]
  n_small, n_big = len(EVEN_SMALL_WEIGHT_NAMES), len(EVEN_BIG_WEIGHT_NAMES)
  w = dict(zip(EVEN_SMALL_WEIGHT_NAMES, refs[4:4 + n_small]))
  big_hbm = refs[4 + n_small:4 + n_small + n_big]
  cast_in = refs[4 + n_small + n_big:4 + n_small + n_big + N_CAST]
  outs = refs[4 + n_small + n_big + N_CAST:]
  y_ref, pool_state_ref, conv_state_ref, ys_ref, ps_ref, cs_ref = outs[:6]
  cast_out = outs[6:6 + N_CAST]
  scratch = outs[6 + N_CAST:]
  p_hist, c_hist, x1_buf = scratch[:3]
  big_vmem = scratch[3:3 + n_big]
  stage, sems = scratch[3 + n_big:]
  w.update(zip(EVEN_BIG_WEIGHT_NAMES, big_vmem))

  s = pl.program_id(0)
  t = s % tiles_per_seq
  slot = s % 2

  @pl.when(s == 0)
  def _():
    _fetch_bf16([m.at[layer[i]] for i, m in enumerate(big_hbm)], big_vmem, stage, sems)

  @pl.when(s < n_tiles)
  def _():
    _even_mix_head(t, p_hist, c_hist)
    for src, dst in zip(cast_in, cast_out):
      n = src.shape[1]
      dst[:, :n] = src[...].astype(dst.dtype)
      if dst.shape[1] > n:
        dst[:, n:] = jnp.zeros((dst.shape[0], dst.shape[1] - n), dst.dtype)

  def mix_stages():
    return _even_mix_stages(
        lambda: x_ref[...], w, _set(x1_buf.at[slot]),
        _even_prompt_mixers(t, p_hist, c_hist, w["conv_w"], tile), d_pool, d_conv)

  def ffn_stages():
    return _ffn_stages(lambda: x1_buf[1 - slot], w, _set(y_ref), False)

  _pipelined_steps(s, n_tiles, mix_stages, ffn_stages)

  @pl.when(s < n_tiles)
  def _():
    _even_mix_tail(t, pool_state_ref, conv_state_ref, p_hist, c_hist,
                   last_t=tiles_per_seq - 1)

  @pl.when(s == n_tiles + 1)
  def _():
    st = {}
    mixers = _even_sample_mixers(pool_sums_ref, conv_hist_ref, w["conv_w"],
                                 ps_ref, cs_ref, start)
    _run(_even_mix_stages(lambda: xs_ref[:, 0, :], w, lambda v: st.update(x1=v),
                          mixers, d_pool, d_conv))
    _run(_ffn_stages(lambda: st["x1"], w, _set(ys_ref), False))


def _odd_mix_stages(load_x, w, emit, gating, d_gate):
  st = {}

  def norm():
    st["x"] = load_x()
    st["h"] = _rmsnorm(st["x"], w["g_mix"][...]).astype(jnp.bfloat16)

  def in_proj():
    st["z"] = _skewed_dot(st["h"], w["w_in"], 2 * d_gate)

  def activate():
    z = _gelu_exact(st["z"])
    st["u"] = z[:, :d_gate]
    st["v"] = _rmsnorm(z[:, d_gate:], w["g_v"][...])

  def gate():
    st["gated"] = (st["u"] * gating(st["v"])).astype(jnp.bfloat16)

  def out_proj():
    emit(st["x"] + _skewed_dot(st["gated"], w["w_out"], st["x"].shape[1]))

  return [norm, in_proj, activate, gate, out_proj]


def _odd_prompt_gating(w, tile, d_gate):
  def gating(v):
    vb = v.astype(jnp.bfloat16)
    n_chunks = tile // CHUNK
    head = d_gate // N_SG_HEADS
    row = lax.broadcasted_iota(jnp.int32, (CHUNK, CHUNK), 0)
    col = lax.broadcasted_iota(jnp.int32, (CHUNK, CHUNK), 1)
    causal = row >= col
    per_head = []
    for hd in range(N_SG_HEADS):
      w_h = jnp.where(causal, w["w_s"][hd], 0.0).astype(jnp.bfloat16)
      rhs = jnp.concatenate(
          [vb[c * CHUNK:(c + 1) * CHUNK, hd * head:(hd + 1) * head]
           for c in range(n_chunks)], axis=1)
      o = jnp.dot(w_h, rhs, preferred_element_type=jnp.float32)
      per_head.append(o + w["b_s_t"][:, hd:hd + 1])
    return jnp.concatenate(
        [jnp.concatenate([o[:, c * head:(c + 1) * head] for o in per_head], axis=1)
         for c in range(n_chunks)], axis=0)
  return gating


def _odd_sample_gating(w, v_ref, d_gate):
  def gating(v):
    v_ref[:, 0, :] = v
    head = d_gate // N_SG_HEADS
    rows = v.shape[0]
    vb = v.astype(jnp.bfloat16).astype(jnp.float32)
    per_head = []
    for hd in range(N_SG_HEADS):
      w00 = w["w_s"][hd, 0:1, 0:1].astype(jnp.bfloat16).astype(jnp.float32)
      b0 = w["b_s_t"][0:1, hd:hd + 1]
      per_head.append(vb[:, hd * head:(hd + 1) * head] * jnp.broadcast_to(w00, (rows, head))
                      + jnp.broadcast_to(b0, (rows, head)))
    return jnp.concatenate(per_head, axis=1)
  return gating


def _odd_kernel(*refs, n_tiles, tile, d_gate):
  x_ref, xs_ref = refs[:2]
  n_w = len(ODD_WEIGHT_NAMES)
  w = dict(zip(ODD_WEIGHT_NAMES, refs[2:2 + n_w]))
  y_ref, ys_ref, vs_ref, x1_buf = refs[2 + n_w:]

  s = pl.program_id(0)
  slot = s % 2

  def mix_stages():
    return _odd_mix_stages(lambda: x_ref[...], w, _set(x1_buf.at[slot]),
                           _odd_prompt_gating(w, tile, d_gate), d_gate)

  def ffn_stages():
    return _ffn_stages(lambda: x1_buf[1 - slot], w, _set(y_ref), True)

  _pipelined_steps(s, n_tiles, mix_stages, ffn_stages)

  @pl.when(s == n_tiles + 1)
  def _():
    st = {}
    _run(_odd_mix_stages(lambda: xs_ref[...], w, lambda v: st.update(x1=v),
                         _odd_sample_gating(w, vs_ref, d_gate), d_gate))
    _run(_ffn_stages(lambda: st["x1"], w, _set_rows(ys_ref), True))


def _resident(arr):
  zeros = (0,) * arr.ndim
  return pl.BlockSpec(arr.shape, lambda s: zeros, pipeline_mode=pl.Buffered(1))


def _whole_out(shape):
  zeros = (0,) * len(shape)
  return pl.BlockSpec(shape, lambda s: zeros)


def _params():
  return pltpu.CompilerParams(dimension_semantics=("arbitrary",),
                              vmem_limit_bytes=V7X_VMEM_LIMIT_BYTES)


def _prompt_specs(x, tile):
  batch, seq, d_model = x.shape
  tiles_per_seq = seq // tile
  n_tiles = batch * tiles_per_seq
  mix_tile = lambda s: jnp.clip(s, 0, n_tiles - 1)
  ffn_tile = lambda s: jnp.clip(s - 1, 0, n_tiles - 1)
  block = lambda which: pl.BlockSpec(
      (None, tile, d_model),
      lambda s: (which(s) // tiles_per_seq, which(s) % tiles_per_seq, 0))
  per_seq = lambda rows, ch: pl.BlockSpec(
      (None, rows, ch), lambda s: (mix_tile(s) // tiles_per_seq, 0, 0))
  return n_tiles, tiles_per_seq, mix_tile, block(mix_tile), block(ffn_tile), per_seq


def _cast_slab_spec(rows, cols, n_steps, step, layer):
  slab = rows // n_steps
  steps_per_slab = 1
  while slab % V7X_BF16_SUBLANES:
    slab *= 2
    steps_per_slab *= 2
  src = pl.BlockSpec((None, slab, cols), lambda s: (layer, step(s) // steps_per_slab, 0))
  dst = pl.BlockSpec((slab, cols + _skew_cols(cols)),
                     lambda s: (step(s) // steps_per_slab, 0))
  return src, dst


def _pool_history_sums(pool_hist):
  rows, d_pool = pool_hist.shape[1:]
  return pl.pallas_call(
      _pool_history_sums_kernel,
      grid=(1,),
      in_specs=[pl.BlockSpec(pool_hist.shape, lambda i: (0, 0, 0))],
      out_specs=pl.BlockSpec((rows, d_pool), lambda i: (0, 0)),
      out_shape=jax.ShapeDtypeStruct((rows, d_pool), pool_hist.dtype),
      name="pool_history_sums",
  )(pool_hist)


def _even_layer(x, xs, pool_sums, conv_hist, small_weights, mats, mat_layer,
                next_mats, next_layer, tile, start):
  n_tiles, tiles_per_seq, step, tok_in, tok_out, per_seq = _prompt_specs(x, tile)
  batch = x.shape[0]
  rows, _, d_model = xs.shape
  d_pool = pool_sums.shape[-1]
  d_conv = conv_hist.shape[-1]
  cast_specs = [_cast_slab_spec(m.shape[1], m.shape[2], n_tiles, step, next_layer[i])
                for i, m in enumerate(next_mats)]
  f32 = x.dtype
  skewed = lambda m: (m.shape[1], m.shape[2] + _skew_cols(m.shape[2]))
  widest = max(m.shape[2] for m in mats)
  return pl.pallas_call(
      functools.partial(_even_kernel, n_tiles=n_tiles,
                        tiles_per_seq=tiles_per_seq, tile=tile, start=start,
                        d_pool=d_pool, d_conv=d_conv, layer=tuple(mat_layer)),
      grid=(n_tiles + 2,),
      in_specs=([tok_in, _resident(xs), _resident(pool_sums), _resident(conv_hist)]
                + [_resident(w) for w in small_weights]
                + [pl.BlockSpec(memory_space=pl.ANY) for _ in mats]
                + [c[0] for c in cast_specs]),
      out_specs=([tok_out, per_seq(POOL_HIST, d_pool), per_seq(CONV_HIST, d_conv),
                  _whole_out((rows, d_model)), _whole_out((rows, d_pool)),
                  _whole_out((rows, d_conv))] + [c[1] for c in cast_specs]),
      out_shape=([jax.ShapeDtypeStruct(x.shape, f32),
                  jax.ShapeDtypeStruct((batch, POOL_HIST, d_pool), f32),
                  jax.ShapeDtypeStruct((batch, CONV_HIST, d_conv), f32),
                  jax.ShapeDtypeStruct((rows, d_model), f32),
                  jax.ShapeDtypeStruct((rows, d_pool), f32),
                  jax.ShapeDtypeStruct((rows, d_conv), f32)]
                 + [jax.ShapeDtypeStruct(skewed(m), jnp.bfloat16) for m in next_mats]),
      scratch_shapes=([pltpu.VMEM((POOL_PAD, d_pool), jnp.float32),
                       pltpu.VMEM((CONV_PAD, d_conv), jnp.float32),
                       pltpu.VMEM((2, tile, d_model), jnp.float32)]
                      + [pltpu.VMEM(skewed(m), jnp.bfloat16) for m in mats]
                      + [pltpu.VMEM((FETCH_SLOTS, FETCH_ROWS, widest), jnp.float32),
                         pltpu.SemaphoreType.DMA((FETCH_SLOTS,))]),
      compiler_params=_params(),
      name="even_layer",
  )(x, xs, pool_sums, conv_hist, *small_weights, *mats, *next_mats)


def _odd_layer(x, xs, weights, tile):
  n_tiles, _, _, tok_in, tok_out, _ = _prompt_specs(x, tile)
  rows, d_model = xs.shape
  d_gate = weights[ODD_WEIGHT_NAMES.index("g_v")].shape[-1]
  return pl.pallas_call(
      functools.partial(_odd_kernel, n_tiles=n_tiles, tile=tile, d_gate=d_gate),
      grid=(n_tiles + 2,),
      in_specs=[tok_in, _resident(xs)] + [_resident(w) for w in weights],
      out_specs=[tok_out, _whole_out((rows, 1, d_model)), _whole_out((rows, 1, d_gate))],
      out_shape=[jax.ShapeDtypeStruct(x.shape, x.dtype),
                 jax.ShapeDtypeStruct((rows, 1, d_model), x.dtype),
                 jax.ShapeDtypeStruct((rows, 1, d_gate), x.dtype)],
      scratch_shapes=[pltpu.VMEM((2, tile, d_model), jnp.float32)],
      compiler_params=_params(),
      name="odd_layer",
  )(x, xs, *weights)


def _pool_block_diag(w_pool):
  per_tile = V7X_MXU_DIM // POOL_GROUP
  n_tiles = w_pool.shape[0] // per_tile
  out = jnp.zeros((n_tiles, V7X_MXU_DIM, V7X_MXU_DIM), jnp.bfloat16)
  for g in range(w_pool.shape[0]):
    i, j = divmod(g, per_tile)
    lo = j * POOL_GROUP
    out = out.at[i, lo:lo + POOL_GROUP, lo:lo + POOL_GROUP].set(
        w_pool[g].astype(jnp.bfloat16))
  return out


def kernel(x_prompt, x_sample, state_pool, state_conv, norm_mix, norm_ffn, norm_final, w_in_even, w_pool, pool_scale, conv_w, w_out_even, w_in_odd, norm_sg, w_s, b_s, w_out_odd, ffn_w_gate, ffn_w_up, ffn_w_down):
  depth = norm_mix.shape[0]
  assert depth == 2 and w_in_even.shape[0] == 1 and w_in_odd.shape[0] == 1
  assert x_sample.shape[1] == 1
  assert x_prompt.shape[1] % PROMPT_TILE == 0 and PROMPT_TILE % CHUNK == 0

  row = lambda v: v.reshape(1, -1)

  even_small = (row(norm_mix[0]), _pool_block_diag(w_pool[0]), row(pool_scale[0]),
                conv_w[0], row(norm_ffn[0]))
  even_mats = (w_in_even, w_out_even, ffn_w_gate, ffn_w_up, ffn_w_down)
  even_mat_layer = (0, 0, 0, 0, 0)
  odd_mats = (w_in_odd, w_out_odd, ffn_w_gate, ffn_w_up, ffn_w_down)
  odd_mat_layer = (0, 0, 1, 1, 1)

  pool_hist = jnp.swapaxes(state_pool[0], 0, 1)
  conv_hist = jnp.swapaxes(state_conv[0], 0, 1)

  (x1_p, pool_p, conv_p, x1_s, p_s, c_s, w_in_o, w_out_o, wg_o, wu_o,
   wd_o) = _even_layer(x_prompt, x_sample, _pool_history_sums(pool_hist),
                       conv_hist, even_small, even_mats, even_mat_layer,
                       odd_mats, odd_mat_layer, PROMPT_TILE, PAST_LEN)

  odd_w = (row(norm_mix[1]), w_in_o, row(norm_sg[0]), w_s[0], b_s[0].T, w_out_o,
           row(norm_ffn[1]), wg_o, wu_o, wd_o, row(norm_final))
  y_p, y_s, v_s = _odd_layer(x1_p, x1_s, odd_w, PROMPT_TILE)

  pool_s = jnp.concatenate([state_pool[:, :, 1:], p_s[None, :, None, :]], axis=2)
  conv_s = jnp.concatenate([state_conv[:, :, 1:], c_s[None, :, None, :]], axis=2)

  return (y_p, y_s, pool_p[None], pool_s, conv_p[None], conv_s, v_s[None])
```

```python
import functools
import math

import jax
import jax.numpy as jnp
from jax import lax
from jax.experimental import pallas as pl
from jax.experimental.pallas import tpu as pltpu

POOL_WINDOWS = (2, 4, 8, 16)
POOL_GROUP = 128
POOL_HIST = max(POOL_WINDOWS) - 1
CONV_WIDTH = 3
CONV_HIST = CONV_WIDTH - 1
CHUNK = 128
N_SG_HEADS = 8
EPS = 1e-6
PAST_LEN = 16384

V7X_SUBLANES = 8
V7X_LANES = 128
V7X_BF16_SUBLANES = 16
V7X_MXU_DIM = 256
V7X_VMEM_LIMIT_BYTES = 60000 * 1024

POOL_PAD = -(-POOL_HIST // V7X_SUBLANES) * V7X_SUBLANES
CONV_PAD = -(-CONV_HIST // V7X_SUBLANES) * V7X_SUBLANES

PROMPT_TILE = 512
FETCH_ROWS = 128
FETCH_SLOTS = 4


def _bf16_dot(a, b):
  return jnp.dot(a.astype(jnp.bfloat16), b, preferred_element_type=jnp.float32)


def _rmsnorm(x, g):
  y = x * lax.rsqrt(jnp.mean(x * x, axis=-1, keepdims=True) + EPS)
  return y * g


def _gelu_exact(x):
  return 0.5 * x * (1.0 + lax.erf(x * math.sqrt(0.5)))


def _skew_cols(n):
  return V7X_LANES if (n // V7X_LANES) % V7X_SUBLANES == 0 else 0


def _skewed_dot(a, w_ref, n):
  return jnp.dot(a, w_ref[:, :n], preferred_element_type=jnp.float32)


def _pool_project(d, w_bd_ref, scale):
  halves = []
  for i in range(w_bd_ref.shape[0]):
    lo = i * V7X_MXU_DIM
    halves.append(_bf16_dot(d[:, lo:lo + V7X_MXU_DIM], w_bd_ref[i]))
  return jnp.concatenate(halves, axis=-1) * scale


def _ffn_stages(load_x1, w, emit, final_norm):
  st = {}

  def norm():
    st["x"] = load_x1()
    st["h"] = _rmsnorm(st["x"], w["g_ffn"][...]).astype(jnp.bfloat16)

  def gate():
    st["gate"] = jnp.dot(st["h"], w["wg"][...], preferred_element_type=jnp.float32)

  def up():
    st["up"] = jnp.dot(st["h"], w["wu"][...], preferred_element_type=jnp.float32)

  def activate():
    gate = st["gate"]
    st["act"] = (gate * jax.nn.sigmoid(gate) * st["up"]).astype(jnp.bfloat16)

  def down():
    y = st["x"] + _skewed_dot(st["act"], w["wd"], st["x"].shape[1])
    emit(_rmsnorm(y, w["g_final"][...]) if final_norm else y)

  return [norm, gate, up, activate, down]


def _even_mix_stages(load_x, w, emit, mixers, d_pool, d_conv):
  st = {}

  def norm():
    st["x"] = load_x()
    st["h"] = _rmsnorm(st["x"], w["g_mix"][...]).astype(jnp.bfloat16)

  def in_proj():
    st["z"] = _skewed_dot(st["h"], w["w_in"], d_pool + 3 * d_conv)

  def mix():
    z = st["z"]
    p = z[:, :d_pool]
    xb = z[:, d_pool:d_pool + d_conv]
    bg = z[:, d_pool + d_conv:d_pool + 2 * d_conv]
    cg = z[:, d_pool + 2 * d_conv:]
    pooled, conv_y = mixers(p, cg * xb)
    st["pool_d"] = (pooled - p).astype(jnp.bfloat16)
    st["conv_out"] = (bg * conv_y).astype(jnp.bfloat16)

  def project():
    a_out = _pool_project(st["pool_d"], w["w_pool"], w["pool_scale"][...])
    st["mixed"] = jnp.concatenate([a_out.astype(jnp.bfloat16), st["conv_out"]], axis=-1)

  def out_proj():
    emit(st["x"] + _skewed_dot(st["mixed"], w["w_out"], st["x"].shape[1]))

  return [norm, in_proj, mix, project, out_proj]


def _rows_back(x, k):
  return pltpu.roll(x, k, axis=0)


def _even_prompt_mixers(t, p_hist, c_hist, conv_w, tile):
  assert all(w & (w - 1) == 0 and w <= POOL_PAD for w in POOL_WINDOWS)

  def mixers(p, c):
    pos = t * tile + lax.broadcasted_iota(jnp.int32, (POOL_PAD, POOL_GROUP), 0)
    pooled = []
    for g, win in enumerate(POOL_WINDOWS):
      lo = g * POOL_GROUP
      sums = jnp.concatenate([p_hist[:, lo:lo + POOL_GROUP], p[:, lo:lo + POOL_GROUP]],
                             axis=0)
      span = 1
      while span < win:
        sums = sums + _rows_back(sums, span)
        span *= 2
      sums = sums[POOL_PAD:]
      head = sums[:POOL_PAD] / jnp.minimum(pos + 1, win).astype(jnp.float32)
      pooled.append(jnp.concatenate([head, sums[POOL_PAD:] * (1.0 / win)], axis=0))
    p_hist[...] = p[tile - POOL_PAD:, :]
    c_rows = jnp.concatenate([c_hist[...], c], axis=0)
    y = _rows_back(c_rows, 2)[CONV_PAD:] * conv_w[0:1, :]
    y = y + _rows_back(c_rows, 1)[CONV_PAD:] * conv_w[1:2, :]
    y = y + c * conv_w[2:3, :]
    c_hist[...] = c[tile - CONV_PAD:, :]
    return jnp.concatenate(pooled, axis=-1), y
  return mixers


def _pool_history_sums_kernel(pool_hist_ref, sums_ref):
  for g, win in enumerate(POOL_WINDOWS):
    lo = g * POOL_GROUP
    acc = pool_hist_ref[POOL_HIST - 1, :, lo:lo + POOL_GROUP]
    for k in range(2, win):
      acc = acc + pool_hist_ref[POOL_HIST - k, :, lo:lo + POOL_GROUP]
    sums_ref[:, lo:lo + POOL_GROUP] = acc


def _even_sample_mixers(pool_sums_ref, conv_hist_ref, conv_w, p_ref, c_ref, start):
  def mixers(p, c):
    p_ref[...] = p
    c_ref[...] = c
    pooled = []
    for g, win in enumerate(POOL_WINDOWS):
      lo = g * POOL_GROUP
      acc = p[:, lo:lo + POOL_GROUP] + pool_sums_ref[:, lo:lo + POOL_GROUP]
      pooled.append(acc / float(min(start + 1, win)))
    y = conv_hist_ref[0] * conv_w[0:1, :]
    y = y + conv_hist_ref[1] * conv_w[1:2, :]
    y = y + c * conv_w[2:3, :]
    return jnp.concatenate(pooled, axis=-1), y
  return mixers


def _even_mix_head(t, p_hist, c_hist):
  @pl.when(t == 0)
  def _():
    p_hist[...] = jnp.zeros(p_hist.shape, jnp.float32)
    c_hist[...] = jnp.zeros(c_hist.shape, jnp.float32)


def _even_mix_tail(t, pool_state_ref, conv_state_ref, p_hist, c_hist, *, last_t):
  @pl.when(t == last_t)
  def _():
    pool_state_ref[...] = p_hist[POOL_PAD - POOL_HIST:, :]
    conv_state_ref[...] = c_hist[CONV_PAD - CONV_HIST:, :]


EVEN_SMALL_WEIGHT_NAMES = ("g_mix", "w_pool", "pool_scale", "conv_w", "g_ffn")
EVEN_BIG_WEIGHT_NAMES = ("w_in", "w_out", "wg", "wu", "wd")
ODD_WEIGHT_NAMES = ("g_mix", "w_in", "g_v", "w_s", "b_s_t", "w_out",
                    "g_ffn", "wg", "wu", "wd", "g_final")
N_CAST = 5


def _fetch_bf16(srcs, dsts, stage, sems):
  chunks = []
  for src, dst in zip(srcs, dsts):
    n_rows, n_cols = src.shape
    if dst.shape[1] > n_cols:
      dst[:, n_cols:] = jnp.zeros((n_rows, dst.shape[1] - n_cols), dst.dtype)
    for r0 in range(0, n_rows, FETCH_ROWS):
      chunks.append((src, dst, r0, min(FETCH_ROWS, n_rows - r0), n_cols))

  n_slots = stage.shape[0]

  def copy(i):
    src, _, r0, n, n_cols = chunks[i]
    return pltpu.make_async_copy(src.at[pl.ds(r0, n), :],
                                 stage.at[i % n_slots, pl.ds(0, n), pl.ds(0, n_cols)],
                                 sems.at[i % n_slots])

  for i in range(min(n_slots - 1, len(chunks))):
    copy(i).start()
  for i, (_, dst, r0, n, n_cols) in enumerate(chunks):
    if i + n_slots - 1 < len(chunks):
      copy(i + n_slots - 1).start()
    copy(i).wait()
    dst[pl.ds(r0, n), :n_cols] = stage[i % n_slots, :n, :n_cols].astype(dst.dtype)


def _run(stages):
  for stage in stages:
    stage()


STEP_ORDER = ("f0", "f1", "f2", "m0", "m1", "m2", "f3", "f4", "m3", "m4")


def _pipelined_steps(s, n_tiles, mix_stages, ffn_stages, order=STEP_ORDER):
  @pl.when(s == 0)
  def _():
    _run(mix_stages())

  @pl.when((s > 0) & (s < n_tiles))
  def _():
    halves = {"m": mix_stages(), "f": ffn_stages()}
    _run([halves[tag[0]][int(tag[1])] for tag in order])

  @pl.when(s == n_tiles)
  def _():
    _run(ffn_stages())


def _set(ref):
  def emit(value):
    ref[...] = value
  return emit


def _set_rows(ref):
  def emit(value):
    ref[:, 0, :] = value
  return emit


def _even_kernel(*refs, n_tiles, tiles_per_seq, tile, start, d_pool, d_conv, layer):
  x_ref, xs_ref, pool_sums_ref, conv_hist_ref = refs[:4]
  n_small, n_big = len(EVEN_SMALL_WEIGHT_NAMES), len(EVEN_BIG_WEIGHT_NAMES)
  w = dict(zip(EVEN_SMALL_WEIGHT_NAMES, refs[4:4 + n_small]))
  big_hbm = refs[4 + n_small:4 + n_small + n_big]
  cast_in = refs[4 + n_small + n_big:4 + n_small + n_big + N_CAST]
  outs = refs[4 + n_small + n_big + N_CAST:]
  y_ref, pool_state_ref, conv_state_ref, ys_ref, ps_ref, cs_ref = outs[:6]
  cast_out = outs[6:6 + N_CAST]
  scratch = outs[6 + N_CAST:]
  p_hist, c_hist, x1_buf = scratch[:3]
  big_vmem = scratch[3:3 + n_big]
  stage, sems = scratch[3 + n_big:]
  w.update(zip(EVEN_BIG_WEIGHT_NAMES, big_vmem))

  s = pl.program_id(0)
  t = s % tiles_per_seq
  slot = s % 2

  @pl.when(s == 0)
  def _():
    _fetch_bf16([m.at[layer[i]] for i, m in enumerate(big_hbm)], big_vmem, stage, sems)

  @pl.when(s < n_tiles)
  def _():
    _even_mix_head(t, p_hist, c_hist)
    for src, dst in zip(cast_in, cast_out):
      n = src.shape[1]
      dst[:, :n] = src[...].astype(dst.dtype)
      if dst.shape[1] > n:
        dst[:, n:] = jnp.zeros((dst.shape[0], dst.shape[1] - n), dst.dtype)

  def mix_stages():
    return _even_mix_stages(
        lambda: x_ref[...], w, _set(x1_buf.at[slot]),
        _even_prompt_mixers(t, p_hist, c_hist, w["conv_w"], tile), d_pool, d_conv)

  def ffn_stages():
    return _ffn_stages(lambda: x1_buf[1 - slot], w, _set(y_ref), False)

  _pipelined_steps(s, n_tiles, mix_stages, ffn_stages)

  @pl.when(s < n_tiles)
  def _():
    _even_mix_tail(t, pool_state_ref, conv_state_ref, p_hist, c_hist,
                   last_t=tiles_per_seq - 1)

  @pl.when(s == n_tiles + 1)
  def _():
    st = {}
    mixers = _even_sample_mixers(pool_sums_ref, conv_hist_ref, w["conv_w"],
                                 ps_ref, cs_ref, start)
    _run(_even_mix_stages(lambda: xs_ref[:, 0, :], w, lambda v: st.update(x1=v),
                          mixers, d_pool, d_conv))
    _run(_ffn_stages(lambda: st["x1"], w, _set(ys_ref), False))


def _odd_mix_stages(load_x, w, emit, gating, d_gate, park=None):
  st = {}

  def norm():
    st["x"] = load_x()
    st["h"] = _rmsnorm(st["x"], w["g_mix"][...]).astype(jnp.bfloat16)

  def in_proj():
    st["z"] = _skewed_dot(st["h"], w["w_in"], 2 * d_gate)

  def activate():
    z = _gelu_exact(st["z"])
    u, v = z[:, :d_gate], _rmsnorm(z[:, d_gate:], w["g_v"][...])
    if park is None:
      st["u"], st["v"] = u, v
    else:
      park[0][...] = u
      park[1][...] = v.astype(park[1].dtype)

  def gate():
    u, v = (st["u"], st["v"]) if park is None else (park[0][...], park[1][...])
    st["gated"] = (u * gating(v)).astype(jnp.bfloat16)

  def out_proj():
    emit(st["x"] + _skewed_dot(st["gated"], w["w_out"], st["x"].shape[1]))

  return [norm, in_proj, activate, gate, out_proj]


def _odd_prompt_gating(w, tile, d_gate):
  def gating(v):
    vb = v.astype(jnp.bfloat16)
    n_chunks = tile // CHUNK
    head = d_gate // N_SG_HEADS
    row = lax.broadcasted_iota(jnp.int32, (CHUNK, CHUNK), 0)
    col = lax.broadcasted_iota(jnp.int32, (CHUNK, CHUNK), 1)
    causal = row >= col
    per_head = []
    for hd in range(N_SG_HEADS):
      w_h = jnp.where(causal, w["w_s"][hd], 0.0).astype(jnp.bfloat16)
      rhs = jnp.concatenate(
          [vb[c * CHUNK:(c + 1) * CHUNK, hd * head:(hd + 1) * head]
           for c in range(n_chunks)], axis=1)
      o = jnp.dot(w_h, rhs, preferred_element_type=jnp.float32)
      per_head.append(o + w["b_s_t"][:, hd:hd + 1])
    return jnp.concatenate(
        [jnp.concatenate([o[:, c * head:(c + 1) * head] for o in per_head], axis=1)
         for c in range(n_chunks)], axis=0)
  return gating


def _odd_sample_gating(w, v_ref, d_gate):
  def gating(v):
    v_ref[:, 0, :] = v
    head = d_gate // N_SG_HEADS
    rows = v.shape[0]
    vb = v.astype(jnp.bfloat16).astype(jnp.float32)
    per_head = []
    for hd in range(N_SG_HEADS):
      w00 = w["w_s"][hd, 0:1, 0:1].astype(jnp.bfloat16).astype(jnp.float32)
      b0 = w["b_s_t"][0:1, hd:hd + 1]
      per_head.append(vb[:, hd * head:(hd + 1) * head] * jnp.broadcast_to(w00, (rows, head))
                      + jnp.broadcast_to(b0, (rows, head)))
    return jnp.concatenate(per_head, axis=1)
  return gating


def _odd_kernel(*refs, n_tiles, tile, d_gate):
  x_ref, xs_ref = refs[:2]
  n_w = len(ODD_WEIGHT_NAMES)
  w = dict(zip(ODD_WEIGHT_NAMES, refs[2:2 + n_w]))
  y_ref, ys_ref, vs_ref, x1_buf, u_park, v_park = refs[2 + n_w:]

  s = pl.program_id(0)
  slot = s % 2

  def mix_stages():
    return _odd_mix_stages(lambda: x_ref[...], w, _set(x1_buf.at[slot]),
                           _odd_prompt_gating(w, tile, d_gate), d_gate,
                           park=(u_park, v_park))

  def ffn_stages():
    return _ffn_stages(lambda: x1_buf[1 - slot], w, _set(y_ref), True)

  _pipelined_steps(s, n_tiles, mix_stages, ffn_stages)

  @pl.when(s == n_tiles + 1)
  def _():
    st = {}
    _run(_odd_mix_stages(lambda: xs_ref[...], w, lambda v: st.update(x1=v),
                         _odd_sample_gating(w, vs_ref, d_gate), d_gate))
    _run(_ffn_stages(lambda: st["x1"], w, _set_rows(ys_ref), True))


def _resident(arr):
  zeros = (0,) * arr.ndim
  return pl.BlockSpec(arr.shape, lambda s: zeros, pipeline_mode=pl.Buffered(1))


def _whole_out(shape):
  zeros = (0,) * len(shape)
  return pl.BlockSpec(shape, lambda s: zeros)


def _params():
  return pltpu.CompilerParams(dimension_semantics=("arbitrary",),
                              vmem_limit_bytes=V7X_VMEM_LIMIT_BYTES)


def _prompt_specs(x, tile):
  batch, seq, d_model = x.shape
  tiles_per_seq = seq // tile
  n_tiles = batch * tiles_per_seq
  mix_tile = lambda s: jnp.clip(s, 0, n_tiles - 1)
  ffn_tile = lambda s: jnp.clip(s - 1, 0, n_tiles - 1)
  block = lambda which: pl.BlockSpec(
      (None, tile, d_model),
      lambda s: (which(s) // tiles_per_seq, which(s) % tiles_per_seq, 0))
  per_seq = lambda rows, ch: pl.BlockSpec(
      (None, rows, ch), lambda s: (mix_tile(s) // tiles_per_seq, 0, 0))
  return n_tiles, tiles_per_seq, mix_tile, block(mix_tile), block(ffn_tile), per_seq


def _cast_slab_spec(rows, cols, n_steps, step, layer):
  slab = rows // n_steps
  steps_per_slab = 1
  while slab % V7X_BF16_SUBLANES:
    slab *= 2
    steps_per_slab *= 2
  src = pl.BlockSpec((None, slab, cols), lambda s: (layer, step(s) // steps_per_slab, 0))
  dst = pl.BlockSpec((slab, cols + _skew_cols(cols)),
                     lambda s: (step(s) // steps_per_slab, 0))
  return src, dst


def _pool_history_sums(pool_hist):
  rows, d_pool = pool_hist.shape[1:]
  return pl.pallas_call(
      _pool_history_sums_kernel,
      grid=(1,),
      in_specs=[pl.BlockSpec(pool_hist.shape, lambda i: (0, 0, 0))],
      out_specs=pl.BlockSpec((rows, d_pool), lambda i: (0, 0)),
      out_shape=jax.ShapeDtypeStruct((rows, d_pool), pool_hist.dtype),
      name="pool_history_sums",
  )(pool_hist)


def _even_layer(x, xs, pool_sums, conv_hist, small_weights, mats, mat_layer,
                next_mats, next_layer, tile, start):
  n_tiles, tiles_per_seq, step, tok_in, tok_out, per_seq = _prompt_specs(x, tile)
  batch = x.shape[0]
  rows, _, d_model = xs.shape
  d_pool = pool_sums.shape[-1]
  d_conv = conv_hist.shape[-1]
  cast_specs = [_cast_slab_spec(m.shape[1], m.shape[2], n_tiles, step, next_layer[i])
                for i, m in enumerate(next_mats)]
  f32 = x.dtype
  skewed = lambda m: (m.shape[1], m.shape[2] + _skew_cols(m.shape[2]))
  widest = max(m.shape[2] for m in mats)
  return pl.pallas_call(
      functools.partial(_even_kernel, n_tiles=n_tiles,
                        tiles_per_seq=tiles_per_seq, tile=tile, start=start,
                        d_pool=d_pool, d_conv=d_conv, layer=tuple(mat_layer)),
      grid=(n_tiles + 2,),
      in_specs=([tok_in, _resident(xs), _resident(pool_sums), _resident(conv_hist)]
                + [_resident(w) for w in small_weights]
                + [pl.BlockSpec(memory_space=pl.ANY) for _ in mats]
                + [c[0] for c in cast_specs]),
      out_specs=([tok_out, per_seq(POOL_HIST, d_pool), per_seq(CONV_HIST, d_conv),
                  _whole_out((rows, d_model)), _whole_out((rows, d_pool)),
                  _whole_out((rows, d_conv))] + [c[1] for c in cast_specs]),
      out_shape=([jax.ShapeDtypeStruct(x.shape, f32),
                  jax.ShapeDtypeStruct((batch, POOL_HIST, d_pool), f32),
                  jax.ShapeDtypeStruct((batch, CONV_HIST, d_conv), f32),
                  jax.ShapeDtypeStruct((rows, d_model), f32),
                  jax.ShapeDtypeStruct((rows, d_pool), f32),
                  jax.ShapeDtypeStruct((rows, d_conv), f32)]
                 + [jax.ShapeDtypeStruct(skewed(m), jnp.bfloat16) for m in next_mats]),
      scratch_shapes=([pltpu.VMEM((POOL_PAD, d_pool), jnp.float32),
                       pltpu.VMEM((CONV_PAD, d_conv), jnp.float32),
                       pltpu.VMEM((2, tile, d_model), jnp.float32)]
                      + [pltpu.VMEM(skewed(m), jnp.bfloat16) for m in mats]
                      + [pltpu.VMEM((FETCH_SLOTS, FETCH_ROWS, widest), jnp.float32),
                         pltpu.SemaphoreType.DMA((FETCH_SLOTS,))]),
      compiler_params=_params(),
      name="even_layer",
  )(x, xs, pool_sums, conv_hist, *small_weights, *mats, *next_mats)


def _odd_layer(x, xs, weights, tile):
  n_tiles, _, _, tok_in, tok_out, _ = _prompt_specs(x, tile)
  rows, d_model = xs.shape
  d_gate = weights[ODD_WEIGHT_NAMES.index("g_v")].shape[-1]
  return pl.pallas_call(
      functools.partial(_odd_kernel, n_tiles=n_tiles, tile=tile, d_gate=d_gate),
      grid=(n_tiles + 2,),
      in_specs=[tok_in, _resident(xs)] + [_resident(w) for w in weights],
      out_specs=[tok_out, _whole_out((rows, 1, d_model)), _whole_out((rows, 1, d_gate))],
      out_shape=[jax.ShapeDtypeStruct(x.shape, x.dtype),
                 jax.ShapeDtypeStruct((rows, 1, d_model), x.dtype),
                 jax.ShapeDtypeStruct((rows, 1, d_gate), x.dtype)],
      scratch_shapes=[pltpu.VMEM((2, tile, d_model), jnp.float32),
                      pltpu.VMEM((tile, d_gate), jnp.float32),
                      pltpu.VMEM((tile, d_gate), jnp.bfloat16)],
      compiler_params=_params(),
      name="odd_layer",
  )(x, xs, *weights)


def _pool_block_diag(w_pool):
  per_tile = V7X_MXU_DIM // POOL_GROUP
  n_tiles = w_pool.shape[0] // per_tile
  out = jnp.zeros((n_tiles, V7X_MXU_DIM, V7X_MXU_DIM), jnp.bfloat16)
  for g in range(w_pool.shape[0]):
    i, j = divmod(g, per_tile)
    lo = j * POOL_GROUP
    out = out.at[i, lo:lo + POOL_GROUP, lo:lo + POOL_GROUP].set(
        w_pool[g].astype(jnp.bfloat16))
  return out


def kernel(x_prompt, x_sample, state_pool, state_conv, norm_mix, norm_ffn, norm_final, w_in_even, w_pool, pool_scale, conv_w, w_out_even, w_in_odd, norm_sg, w_s, b_s, w_out_odd, ffn_w_gate, ffn_w_up, ffn_w_down):
  depth = norm_mix.shape[0]
  assert depth == 2 and w_in_even.shape[0] == 1 and w_in_odd.shape[0] == 1
  assert x_sample.shape[1] == 1
  assert x_prompt.shape[1] % PROMPT_TILE == 0 and PROMPT_TILE % CHUNK == 0

  row = lambda v: v.reshape(1, -1)

  even_small = (row(norm_mix[0]), _pool_block_diag(w_pool[0]), row(pool_scale[0]),
                conv_w[0], row(norm_ffn[0]))
  even_mats = (w_in_even, w_out_even, ffn_w_gate, ffn_w_up, ffn_w_down)
  even_mat_layer = (0, 0, 0, 0, 0)
  odd_mats = (w_in_odd, w_out_odd, ffn_w_gate, ffn_w_up, ffn_w_down)
  odd_mat_layer = (0, 0, 1, 1, 1)

  pool_hist = jnp.swapaxes(state_pool[0], 0, 1)
  conv_hist = jnp.swapaxes(state_conv[0], 0, 1)

  (x1_p, pool_p, conv_p, x1_s, p_s, c_s, w_in_o, w_out_o, wg_o, wu_o,
   wd_o) = _even_layer(x_prompt, x_sample, _pool_history_sums(pool_hist),
                       conv_hist, even_small, even_mats, even_mat_layer,
                       odd_mats, odd_mat_layer, PROMPT_TILE, PAST_LEN)

  odd_w = (row(norm_mix[1]), w_in_o, row(norm_sg[0]), w_s[0], b_s[0].T, w_out_o,
           row(norm_ffn[1]), wg_o, wu_o, wd_o, row(norm_final))
  y_p, y_s, v_s = _odd_layer(x1_p, x1_s, odd_w, PROMPT_TILE)

  pool_s = jnp.concatenate([state_pool[:, :, 1:], p_s[None, :, None, :]], axis=2)
  conv_s = jnp.concatenate([state_conv[:, :, 1:], c_s[None, :, None, :]], axis=2)

  return (y_p, y_s, pool_p[None], pool_s, conv_p[None], conv_s, v_s[None])
```

```python
import functools
import math

import jax
import jax.numpy as jnp
from jax import lax
from jax.experimental import pallas as pl
from jax.experimental.pallas import tpu as pltpu

POOL_WINDOWS = (2, 4, 8, 16)
POOL_GROUP = 128
POOL_HIST = max(POOL_WINDOWS) - 1
CONV_WIDTH = 3
CONV_HIST = CONV_WIDTH - 1
CHUNK = 128
N_SG_HEADS = 8
EPS = 1e-6
PAST_LEN = 16384

V7X_SUBLANES = 8
V7X_LANES = 128
V7X_BF16_SUBLANES = 16
V7X_MXU_DIM = 256
V7X_VMEM_LIMIT_BYTES = 60000 * 1024

POOL_PAD = -(-POOL_HIST // V7X_SUBLANES) * V7X_SUBLANES
CONV_PAD = -(-CONV_HIST // V7X_SUBLANES) * V7X_SUBLANES

PROMPT_TILE = 512
FETCH_ROWS = 128
FETCH_SLOTS = 4


def _bf16_dot(a, b):
  return jnp.dot(a.astype(jnp.bfloat16), b, preferred_element_type=jnp.float32)


def _rmsnorm(x, g):
  y = x * lax.rsqrt(jnp.mean(x * x, axis=-1, keepdims=True) + EPS)
  return y * g


def _gelu_exact(x):
  return 0.5 * x * (1.0 + lax.erf(x * math.sqrt(0.5)))


def _skew_cols(n):
  return V7X_LANES if (n // V7X_LANES) % V7X_SUBLANES == 0 else 0


def _skewed_dot(a, w_ref, n):
  return jnp.dot(a, w_ref[:, :n], preferred_element_type=jnp.float32)


def _pool_project(d, w_bd_ref, scale):
  halves = []
  for i in range(w_bd_ref.shape[0]):
    lo = i * V7X_MXU_DIM
    halves.append(_bf16_dot(d[:, lo:lo + V7X_MXU_DIM], w_bd_ref[i]))
  return jnp.concatenate(halves, axis=-1) * scale


def _ffn_stages(load_x1, w, emit, final_norm):
  st = {}

  def norm():
    st["x"] = load_x1()
    st["h"] = _rmsnorm(st["x"], w["g_ffn"][...]).astype(jnp.bfloat16)

  def gate_up():
    st["gate"] = jnp.dot(st["h"], w["wg"][...], preferred_element_type=jnp.float32)
    st["up"] = jnp.dot(st["h"], w["wu"][...], preferred_element_type=jnp.float32)

  def activate():
    gate = st["gate"]
    st["act"] = (gate * jax.nn.sigmoid(gate) * st["up"]).astype(jnp.bfloat16)

  def down():
    y = st["x"] + _skewed_dot(st["act"], w["wd"], st["x"].shape[1])
    emit(_rmsnorm(y, w["g_final"][...]) if final_norm else y)

  return [norm, gate_up, activate, down]


def _even_mix_stages(load_x, w, emit, mixers, d_pool, d_conv):
  st = {}

  def norm():
    st["x"] = load_x()
    st["h"] = _rmsnorm(st["x"], w["g_mix"][...]).astype(jnp.bfloat16)

  def in_proj():
    st["z"] = _skewed_dot(st["h"], w["w_in"], d_pool + 3 * d_conv)

  def mix():
    z = st["z"]
    p = z[:, :d_pool]
    xb = z[:, d_pool:d_pool + d_conv]
    bg = z[:, d_pool + d_conv:d_pool + 2 * d_conv]
    cg = z[:, d_pool + 2 * d_conv:]
    pooled, conv_y = mixers(p, cg * xb)
    st["pool_d"] = (pooled - p).astype(jnp.bfloat16)
    st["conv_out"] = (bg * conv_y).astype(jnp.bfloat16)

  def project():
    a_out = _pool_project(st["pool_d"], w["w_pool"], w["pool_scale"][...])
    st["mixed"] = jnp.concatenate([a_out.astype(jnp.bfloat16), st["conv_out"]], axis=-1)

  def out_proj():
    emit(st["x"] + _skewed_dot(st["mixed"], w["w_out"], st["x"].shape[1]))

  return [norm, in_proj, mix, project, out_proj]


def _rows_back(x, k):
  return pltpu.roll(x, k, axis=0)


def _even_prompt_mixers(t, p_hist, c_hist, conv_w, tile):
  assert all(w & (w - 1) == 0 and w <= POOL_PAD for w in POOL_WINDOWS)

  def mixers(p, c):
    pos = t * tile + lax.broadcasted_iota(jnp.int32, (POOL_PAD, POOL_GROUP), 0)
    pooled = []
    for g, win in enumerate(POOL_WINDOWS):
      lo = g * POOL_GROUP
      sums = jnp.concatenate([p_hist[:, lo:lo + POOL_GROUP], p[:, lo:lo + POOL_GROUP]],
                             axis=0)
      span = 1
      while span < win:
        sums = sums + _rows_back(sums, span)
        span *= 2
      sums = sums[POOL_PAD:]
      head = sums[:POOL_PAD] / jnp.minimum(pos + 1, win).astype(jnp.float32)
      pooled.append(jnp.concatenate([head, sums[POOL_PAD:] * (1.0 / win)], axis=0))
    p_hist[...] = p[tile - POOL_PAD:, :]
    c_rows = jnp.concatenate([c_hist[...], c], axis=0)
    y = _rows_back(c_rows, 2)[CONV_PAD:] * conv_w[0:1, :]
    y = y + _rows_back(c_rows, 1)[CONV_PAD:] * conv_w[1:2, :]
    y = y + c * conv_w[2:3, :]
    c_hist[...] = c[tile - CONV_PAD:, :]
    return jnp.concatenate(pooled, axis=-1), y
  return mixers


def _pool_history_sums_kernel(pool_hist_ref, sums_ref):
  for g, win in enumerate(POOL_WINDOWS):
    lo = g * POOL_GROUP
    acc = pool_hist_ref[POOL_HIST - 1, :, lo:lo + POOL_GROUP]
    for k in range(2, win):
      acc = acc + pool_hist_ref[POOL_HIST - k, :, lo:lo + POOL_GROUP]
    sums_ref[:, lo:lo + POOL_GROUP] = acc


def _even_sample_mixers(pool_sums_ref, conv_hist_ref, conv_w, p_ref, c_ref, start):
  def mixers(p, c):
    p_ref[...] = p
    c_ref[...] = c
    pooled = []
    for g, win in enumerate(POOL_WINDOWS):
      lo = g * POOL_GROUP
      acc = p[:, lo:lo + POOL_GROUP] + pool_sums_ref[:, lo:lo + POOL_GROUP]
      pooled.append(acc / float(min(start + 1, win)))
    y = conv_hist_ref[0] * conv_w[0:1, :]
    y = y + conv_hist_ref[1] * conv_w[1:2, :]
    y = y + c * conv_w[2:3, :]
    return jnp.concatenate(pooled, axis=-1), y
  return mixers


def _even_mix_head(t, p_hist, c_hist):
  @pl.when(t == 0)
  def _():
    p_hist[...] = jnp.zeros(p_hist.shape, jnp.float32)
    c_hist[...] = jnp.zeros(c_hist.shape, jnp.float32)


def _even_mix_tail(t, pool_state_ref, conv_state_ref, p_hist, c_hist, *, last_t):
  @pl.when(t == last_t)
  def _():
    pool_state_ref[...] = p_hist[POOL_PAD - POOL_HIST:, :]
    conv_state_ref[...] = c_hist[CONV_PAD - CONV_HIST:, :]


EVEN_SMALL_WEIGHT_NAMES = ("g_mix", "w_pool", "pool_scale", "conv_w", "g_ffn")
EVEN_BIG_WEIGHT_NAMES = ("w_in", "w_out", "wg", "wu", "wd")
ODD_WEIGHT_NAMES = ("g_mix", "w_in", "g_v", "w_s", "b_s_t", "w_out",
                    "g_ffn", "wg", "wu", "wd", "g_final")
N_CAST = 5


def _fetch_bf16(srcs, dsts, stage, sems):
  chunks = []
  for src, dst in zip(srcs, dsts):
    n_rows, n_cols = src.shape
    if dst.shape[1] > n_cols:
      dst[:, n_cols:] = jnp.zeros((n_rows, dst.shape[1] - n_cols), dst.dtype)
    for r0 in range(0, n_rows, FETCH_ROWS):
      chunks.append((src, dst, r0, min(FETCH_ROWS, n_rows - r0), n_cols))

  n_slots = stage.shape[0]

  def copy(i):
    src, _, r0, n, n_cols = chunks[i]
    return pltpu.make_async_copy(src.at[pl.ds(r0, n), :],
                                 stage.at[i % n_slots, pl.ds(0, n), pl.ds(0, n_cols)],
                                 sems.at[i % n_slots])

  for i in range(min(n_slots - 1, len(chunks))):
    copy(i).start()
  for i, (_, dst, r0, n, n_cols) in enumerate(chunks):
    if i + n_slots - 1 < len(chunks):
      copy(i + n_slots - 1).start()
    copy(i).wait()
    dst[pl.ds(r0, n), :n_cols] = stage[i % n_slots, :n, :n_cols].astype(dst.dtype)


def _run(stages):
  for stage in stages:
    stage()


def _pipelined_steps(s, n_tiles, mix_stages, ffn_stages):
  @pl.when(s == 0)
  def _():
    _run(mix_stages())

  @pl.when((s > 0) & (s < n_tiles))
  def _():
    mix, ffn = mix_stages(), ffn_stages()
    _run([ffn[0], ffn[1], mix[0], mix[1], mix[2], ffn[2], ffn[3], mix[3], mix[4]])

  @pl.when(s == n_tiles)
  def _():
    _run(ffn_stages())


def _set(ref):
  def emit(value):
    ref[...] = value
  return emit


def _set_rows(ref):
  def emit(value):
    ref[:, 0, :] = value
  return emit


def _even_kernel(*refs, n_tiles, tiles_per_seq, tile, start, d_pool, d_conv, layer):
  x_ref, xs_ref, pool_sums_ref, conv_hist_ref = refs[:4]
  n_small, n_big = len(EVEN_SMALL_WEIGHT_NAMES), len(EVEN_BIG_WEIGHT_NAMES)
  w = dict(zip(EVEN_SMALL_WEIGHT_NAMES, refs[4:4 + n_small]))
  big_hbm = refs[4 + n_small:4 + n_small + n_big]
  cast_in = refs[4 + n_small + n_big:4 + n_small + n_big + N_CAST]
  outs = refs[4 + n_small + n_big + N_CAST:]
  y_ref, pool_state_ref, conv_state_ref, ys_ref, ps_ref, cs_ref = outs[:6]
  cast_out = outs[6:6 + N_CAST]
  scratch = outs[6 + N_CAST:]
  p_hist, c_hist, x1_buf = scratch[:3]
  big_vmem = scratch[3:3 + n_big]
  stage, sems = scratch[3 + n_big:]
  w.update(zip(EVEN_BIG_WEIGHT_NAMES, big_vmem))

  s = pl.program_id(0)
  t = s % tiles_per_seq
  slot = s % 2

  @pl.when(s == 0)
  def _():
    _fetch_bf16([m.at[layer[i]] for i, m in enumerate(big_hbm)], big_vmem, stage, sems)

  @pl.when(s < n_tiles)
  def _():
    _even_mix_head(t, p_hist, c_hist)

  def cast_next_layer_slabs():
    for src, dst in zip(cast_in, cast_out):
      n = src.shape[1]
      dst[:, :n] = src[...].astype(dst.dtype)
      if dst.shape[1] > n:
        dst[:, n:] = jnp.zeros((dst.shape[0], dst.shape[1] - n), dst.dtype)

  def mix_stages():
    stages = _even_mix_stages(
        lambda: x_ref[...], w, _set(x1_buf.at[slot]),
        _even_prompt_mixers(t, p_hist, c_hist, w["conv_w"], tile), d_pool, d_conv)
    return [lambda: (stages[0](), cast_next_layer_slabs())] + stages[1:]

  def ffn_stages():
    return _ffn_stages(lambda: x1_buf[1 - slot], w, _set(y_ref), False)

  _pipelined_steps(s, n_tiles, mix_stages, ffn_stages)

  @pl.when(s < n_tiles)
  def _():
    _even_mix_tail(t, pool_state_ref, conv_state_ref, p_hist, c_hist,
                   last_t=tiles_per_seq - 1)

  @pl.when(s == n_tiles + 1)
  def _():
    st = {}
    mixers = _even_sample_mixers(pool_sums_ref, conv_hist_ref, w["conv_w"],
                                 ps_ref, cs_ref, start)
    _run(_even_mix_stages(lambda: xs_ref[:, 0, :], w, lambda v: st.update(x1=v),
                          mixers, d_pool, d_conv))
    _run(_ffn_stages(lambda: st["x1"], w, _set(ys_ref), False))


def _odd_mix_stages(load_x, w, emit, gating, d_gate):
  st = {}

  def norm():
    st["x"] = load_x()
    st["h"] = _rmsnorm(st["x"], w["g_mix"][...]).astype(jnp.bfloat16)

  def in_proj():
    st["z"] = _skewed_dot(st["h"], w["w_in"], 2 * d_gate)

  def activate():
    z = _gelu_exact(st["z"])
    st["u"] = z[:, :d_gate]
    st["v"] = _rmsnorm(z[:, d_gate:], w["g_v"][...])

  def gate():
    st["gated"] = (st["u"] * gating(st["v"])).astype(jnp.bfloat16)

  def out_proj():
    emit(st["x"] + _skewed_dot(st["gated"], w["w_out"], st["x"].shape[1]))

  return [norm, in_proj, activate, gate, out_proj]


def _odd_prompt_gating(w, tile, d_gate):
  def gating(v):
    vb = v.astype(jnp.bfloat16)
    n_chunks = tile // CHUNK
    head = d_gate // N_SG_HEADS
    row = lax.broadcasted_iota(jnp.int32, (CHUNK, CHUNK), 0)
    col = lax.broadcasted_iota(jnp.int32, (CHUNK, CHUNK), 1)
    causal = row >= col
    per_head = []
    for hd in range(N_SG_HEADS):
      w_h = jnp.where(causal, w["w_s"][hd], 0.0).astype(jnp.bfloat16)
      rhs = jnp.concatenate(
          [vb[c * CHUNK:(c + 1) * CHUNK, hd * head:(hd + 1) * head]
           for c in range(n_chunks)], axis=1)
      o = jnp.dot(w_h, rhs, preferred_element_type=jnp.float32)
      per_head.append(o + w["b_s_t"][:, hd:hd + 1])
    return jnp.concatenate(
        [jnp.concatenate([o[:, c * head:(c + 1) * head] for o in per_head], axis=1)
         for c in range(n_chunks)], axis=0)
  return gating


def _odd_sample_gating(w, v_ref, d_gate):
  def gating(v):
    v_ref[:, 0, :] = v
    head = d_gate // N_SG_HEADS
    rows = v.shape[0]
    vb = v.astype(jnp.bfloat16).astype(jnp.float32)
    per_head = []
    for hd in range(N_SG_HEADS):
      w00 = w["w_s"][hd, 0:1, 0:1].astype(jnp.bfloat16).astype(jnp.float32)
      b0 = w["b_s_t"][0:1, hd:hd + 1]
      per_head.append(vb[:, hd * head:(hd + 1) * head] * jnp.broadcast_to(w00, (rows, head))
                      + jnp.broadcast_to(b0, (rows, head)))
    return jnp.concatenate(per_head, axis=1)
  return gating


def _odd_kernel(*refs, n_tiles, tile, d_gate):
  x_ref, xs_ref = refs[:2]
  n_w = len(ODD_WEIGHT_NAMES)
  w = dict(zip(ODD_WEIGHT_NAMES, refs[2:2 + n_w]))
  y_ref, ys_ref, vs_ref, x1_buf = refs[2 + n_w:]

  s = pl.program_id(0)
  slot = s % 2

  def mix_stages():
    return _odd_mix_stages(lambda: x_ref[...], w, _set(x1_buf.at[slot]),
                           _odd_prompt_gating(w, tile, d_gate), d_gate)

  def ffn_stages():
    return _ffn_stages(lambda: x1_buf[1 - slot], w, _set(y_ref), True)

  _pipelined_steps(s, n_tiles, mix_stages, ffn_stages)

  @pl.when(s == n_tiles + 1)
  def _():
    st = {}
    _run(_odd_mix_stages(lambda: xs_ref[...], w, lambda v: st.update(x1=v),
                         _odd_sample_gating(w, vs_ref, d_gate), d_gate))
    _run(_ffn_stages(lambda: st["x1"], w, _set_rows(ys_ref), True))


def _resident(arr):
  zeros = (0,) * arr.ndim
  return pl.BlockSpec(arr.shape, lambda s: zeros, pipeline_mode=pl.Buffered(1))


def _whole_out(shape):
  zeros = (0,) * len(shape)
  return pl.BlockSpec(shape, lambda s: zeros)


def _params():
  return pltpu.CompilerParams(dimension_semantics=("arbitrary",),
                              vmem_limit_bytes=V7X_VMEM_LIMIT_BYTES)


def _prompt_specs(x, tile):
  batch, seq, d_model = x.shape
  tiles_per_seq = seq // tile
  n_tiles = batch * tiles_per_seq
  mix_tile = lambda s: jnp.clip(s, 0, n_tiles - 1)
  ffn_tile = lambda s: jnp.clip(s - 1, 0, n_tiles - 1)
  block = lambda which: pl.BlockSpec(
      (None, tile, d_model),
      lambda s: (which(s) // tiles_per_seq, which(s) % tiles_per_seq, 0))
  per_seq = lambda rows, ch: pl.BlockSpec(
      (None, rows, ch), lambda s: (mix_tile(s) // tiles_per_seq, 0, 0))
  return n_tiles, tiles_per_seq, mix_tile, block(mix_tile), block(ffn_tile), per_seq


def _cast_slab_spec(rows, cols, n_steps, step, layer):
  slab = rows // n_steps
  steps_per_slab = 1
  while slab % V7X_BF16_SUBLANES:
    slab *= 2
    steps_per_slab *= 2
  src = pl.BlockSpec((None, slab, cols), lambda s: (layer, step(s) // steps_per_slab, 0))
  dst = pl.BlockSpec((slab, cols + _skew_cols(cols)),
                     lambda s: (step(s) // steps_per_slab, 0))
  return src, dst


def _pool_history_sums(pool_hist):
  rows, d_pool = pool_hist.shape[1:]
  return pl.pallas_call(
      _pool_history_sums_kernel,
      grid=(1,),
      in_specs=[pl.BlockSpec(pool_hist.shape, lambda i: (0, 0, 0))],
      out_specs=pl.BlockSpec((rows, d_pool), lambda i: (0, 0)),
      out_shape=jax.ShapeDtypeStruct((rows, d_pool), pool_hist.dtype),
      name="pool_history_sums",
  )(pool_hist)


def _even_layer(x, xs, pool_sums, conv_hist, small_weights, mats, mat_layer,
                next_mats, next_layer, tile, start):
  n_tiles, tiles_per_seq, step, tok_in, tok_out, per_seq = _prompt_specs(x, tile)
  batch = x.shape[0]
  rows, _, d_model = xs.shape
  d_pool = pool_sums.shape[-1]
  d_conv = conv_hist.shape[-1]
  cast_specs = [_cast_slab_spec(m.shape[1], m.shape[2], n_tiles, step, next_layer[i])
                for i, m in enumerate(next_mats)]
  f32 = x.dtype
  skewed = lambda m: (m.shape[1], m.shape[2] + _skew_cols(m.shape[2]))
  widest = max(m.shape[2] for m in mats)
  return pl.pallas_call(
      functools.partial(_even_kernel, n_tiles=n_tiles,
                        tiles_per_seq=tiles_per_seq, tile=tile, start=start,
                        d_pool=d_pool, d_conv=d_conv, layer=tuple(mat_layer)),
      grid=(n_tiles + 2,),
      in_specs=([tok_in, _resident(xs), _resident(pool_sums), _resident(conv_hist)]
                + [_resident(w) for w in small_weights]
                + [pl.BlockSpec(memory_space=pl.ANY) for _ in mats]
                + [c[0] for c in cast_specs]),
      out_specs=([tok_out, per_seq(POOL_HIST, d_pool), per_seq(CONV_HIST, d_conv),
                  _whole_out((rows, d_model)), _whole_out((rows, d_pool)),
                  _whole_out((rows, d_conv))] + [c[1] for c in cast_specs]),
      out_shape=([jax.ShapeDtypeStruct(x.shape, f32),
                  jax.ShapeDtypeStruct((batch, POOL_HIST, d_pool), f32),
                  jax.ShapeDtypeStruct((batch, CONV_HIST, d_conv), f32),
                  jax.ShapeDtypeStruct((rows, d_model), f32),
                  jax.ShapeDtypeStruct((rows, d_pool), f32),
                  jax.ShapeDtypeStruct((rows, d_conv), f32)]
                 + [jax.ShapeDtypeStruct(skewed(m), jnp.bfloat16) for m in next_mats]),
      scratch_shapes=([pltpu.VMEM((POOL_PAD, d_pool), jnp.float32),
                       pltpu.VMEM((CONV_PAD, d_conv), jnp.float32),
                       pltpu.VMEM((2, tile, d_model), jnp.float32)]
                      + [pltpu.VMEM(skewed(m), jnp.bfloat16) for m in mats]
                      + [pltpu.VMEM((FETCH_SLOTS, FETCH_ROWS, widest), jnp.float32),
                         pltpu.SemaphoreType.DMA((FETCH_SLOTS,))]),
      compiler_params=_params(),
      name="even_layer",
  )(x, xs, pool_sums, conv_hist, *small_weights, *mats, *next_mats)


def _odd_layer(x, xs, weights, tile):
  n_tiles, _, _, tok_in, tok_out, _ = _prompt_specs(x, tile)
  rows, d_model = xs.shape
  d_gate = weights[ODD_WEIGHT_NAMES.index("g_v")].shape[-1]
  return pl.pallas_call(
      functools.partial(_odd_kernel, n_tiles=n_tiles, tile=tile, d_gate=d_gate),
      grid=(n_tiles + 2,),
      in_specs=[tok_in, _resident(xs)] + [_resident(w) for w in weights],
      out_specs=[tok_out, _whole_out((rows, 1, d_model)), _whole_out((rows, 1, d_gate))],
      out_shape=[jax.ShapeDtypeStruct(x.shape, x.dtype),
                 jax.ShapeDtypeStruct((rows, 1, d_model), x.dtype),
                 jax.ShapeDtypeStruct((rows, 1, d_gate), x.dtype)],
      scratch_shapes=[pltpu.VMEM((2, tile, d_model), jnp.float32)],
      compiler_params=_params(),
      name="odd_layer",
  )(x, xs, *weights)


def _pool_block_diag(w_pool):
  per_tile = V7X_MXU_DIM // POOL_GROUP
  n_tiles = w_pool.shape[0] // per_tile
  out = jnp.zeros((n_tiles, V7X_MXU_DIM, V7X_MXU_DIM), jnp.bfloat16)
  for g in range(w_pool.shape[0]):
    i, j = divmod(g, per_tile)
    lo = j * POOL_GROUP
    out = out.at[i, lo:lo + POOL_GROUP, lo:lo + POOL_GROUP].set(
        w_pool[g].astype(jnp.bfloat16))
  return out


def kernel(x_prompt, x_sample, state_pool, state_conv, norm_mix, norm_ffn, norm_final, w_in_even, w_pool, pool_scale, conv_w, w_out_even, w_in_odd, norm_sg, w_s, b_s, w_out_odd, ffn_w_gate, ffn_w_up, ffn_w_down):
  depth = norm_mix.shape[0]
  assert depth == 2 and w_in_even.shape[0] == 1 and w_in_odd.shape[0] == 1
  assert x_sample.shape[1] == 1
  assert x_prompt.shape[1] % PROMPT_TILE == 0 and PROMPT_TILE % CHUNK == 0

  row = lambda v: v.reshape(1, -1)

  even_small = (row(norm_mix[0]), _pool_block_diag(w_pool[0]), row(pool_scale[0]),
                conv_w[0], row(norm_ffn[0]))
  even_mats = (w_in_even, w_out_even, ffn_w_gate, ffn_w_up, ffn_w_down)
  even_mat_layer = (0, 0, 0, 0, 0)
  odd_mats = (w_in_odd, w_out_odd, ffn_w_gate, ffn_w_up, ffn_w_down)
  odd_mat_layer = (0, 0, 1, 1, 1)

  pool_hist = jnp.swapaxes(state_pool[0], 0, 1)
  conv_hist = jnp.swapaxes(state_conv[0], 0, 1)

  (x1_p, pool_p, conv_p, x1_s, p_s, c_s, w_in_o, w_out_o, wg_o, wu_o,
   wd_o) = _even_layer(x_prompt, x_sample, _pool_history_sums(pool_hist),
                       conv_hist, even_small, even_mats, even_mat_layer,
                       odd_mats, odd_mat_layer, PROMPT_TILE, PAST_LEN)

  odd_w = (row(norm_mix[1]), w_in_o, row(norm_sg[0]), w_s[0], b_s[0].T, w_out_o,
           row(norm_ffn[1]), wg_o, wu_o, wd_o, row(norm_final))
  y_p, y_s, v_s = _odd_layer(x1_p, x1_s, odd_w, PROMPT_TILE)

  pool_s = jnp.concatenate([state_pool[:, :, 1:], p_s[None, :, None, :]], axis=2)
  conv_s = jnp.concatenate([state_conv[:, :, 1:], c_s[None, :, None, :]], axis=2)

  return (y_p, y_s, pool_p[None], pool_s, conv_p[None], conv_s, v_s[None])
```

```python
import functools
import math

import jax
import jax.numpy as jnp
from jax import lax
from jax.experimental import pallas as pl
from jax.experimental.pallas import tpu as pltpu

POOL_WINDOWS = (2, 4, 8, 16)
POOL_GROUP = 128
POOL_HIST = max(POOL_WINDOWS) - 1
CONV_WIDTH = 3
CONV_HIST = CONV_WIDTH - 1
CHUNK = 128
N_SG_HEADS = 8
EPS = 1e-6
PAST_LEN = 16384

V7X_SUBLANES = 8
V7X_LANES = 128
V7X_BF16_SUBLANES = 16
V7X_MXU_DIM = 256
V7X_VMEM_LIMIT_BYTES = 60000 * 1024

POOL_PAD = -(-POOL_HIST // V7X_SUBLANES) * V7X_SUBLANES
CONV_PAD = -(-CONV_HIST // V7X_SUBLANES) * V7X_SUBLANES

PROMPT_TILE = 512
FETCH_ROWS = 128
FETCH_SLOTS = 4


def _bf16_dot(a, b):
  return jnp.dot(a.astype(jnp.bfloat16), b, preferred_element_type=jnp.float32)


def _rmsnorm(x, g):
  y = x * lax.rsqrt(jnp.mean(x * x, axis=-1, keepdims=True) + EPS)
  return y * g


def _gelu_exact(x):
  return 0.5 * x * (1.0 + lax.erf(x * math.sqrt(0.5)))


def _skew_cols(n):
  return V7X_LANES if (n // V7X_LANES) % V7X_SUBLANES == 0 else 0


def _skewed_dot(a, w_ref, n):
  return jnp.dot(a, w_ref[:, :n], preferred_element_type=jnp.float32)


def _pool_project(d, w_bd_ref, scale):
  halves = []
  for i in range(w_bd_ref.shape[0]):
    lo = i * V7X_MXU_DIM
    halves.append(_bf16_dot(d[:, lo:lo + V7X_MXU_DIM], w_bd_ref[i]))
  return jnp.concatenate(halves, axis=-1) * scale


def _ffn_stages(load_x1, w, emit, final_norm):
  st = {}

  def norm():
    st["x"] = load_x1()
    st["h"] = _rmsnorm(st["x"], w["g_ffn"][...]).astype(jnp.bfloat16)

  def gate_up():
    st["gate"] = jnp.dot(st["h"], w["wg"][...], preferred_element_type=jnp.float32)
    st["up"] = jnp.dot(st["h"], w["wu"][...], preferred_element_type=jnp.float32)

  def activate():
    gate = st["gate"]
    st["act"] = (gate * jax.nn.sigmoid(gate) * st["up"]).astype(jnp.bfloat16)

  def down():
    y = st["x"] + _skewed_dot(st["act"], w["wd"], st["x"].shape[1])
    emit(_rmsnorm(y, w["g_final"][...]) if final_norm else y)

  return [norm, gate_up, activate, down]


def _even_mix_stages(load_x, w, emit, mixers, d_pool, d_conv):
  st = {}

  def norm():
    st["x"] = load_x()
    st["h"] = _rmsnorm(st["x"], w["g_mix"][...]).astype(jnp.bfloat16)

  def in_proj():
    st["z"] = _skewed_dot(st["h"], w["w_in"], d_pool + 3 * d_conv)

  def mix():
    z = st["z"]
    p = z[:, :d_pool]
    xb = z[:, d_pool:d_pool + d_conv]
    bg = z[:, d_pool + d_conv:d_pool + 2 * d_conv]
    cg = z[:, d_pool + 2 * d_conv:]
    pooled, conv_y = mixers(p, cg * xb)
    st["pool_d"] = (pooled - p).astype(jnp.bfloat16)
    st["conv_out"] = (bg * conv_y).astype(jnp.bfloat16)

  def project():
    a_out = _pool_project(st["pool_d"], w["w_pool"], w["pool_scale"][...])
    st["mixed"] = jnp.concatenate([a_out.astype(jnp.bfloat16), st["conv_out"]], axis=-1)

  def out_proj():
    emit(st["x"] + _skewed_dot(st["mixed"], w["w_out"], st["x"].shape[1]))

  return [norm, in_proj, mix, project, out_proj]


def _rows_back(x, k):
  return pltpu.roll(x, k, axis=0)


def _even_prompt_mixers(t, p_hist, c_hist, conv_w, tile):
  assert all(w & (w - 1) == 0 and w <= POOL_PAD for w in POOL_WINDOWS)

  def mixers(p, c):
    pos = t * tile + lax.broadcasted_iota(jnp.int32, (POOL_PAD, POOL_GROUP), 0)
    pooled = []
    for g, win in enumerate(POOL_WINDOWS):
      lo = g * POOL_GROUP
      sums = jnp.concatenate([p_hist[:, lo:lo + POOL_GROUP], p[:, lo:lo + POOL_GROUP]],
                             axis=0)
      span = 1
      while span < win:
        sums = sums + _rows_back(sums, span)
        span *= 2
      sums = sums[POOL_PAD:]
      head = sums[:POOL_PAD] / jnp.minimum(pos + 1, win).astype(jnp.float32)
      pooled.append(jnp.concatenate([head, sums[POOL_PAD:] * (1.0 / win)], axis=0))
    p_hist[...] = p[tile - POOL_PAD:, :]
    c_rows = jnp.concatenate([c_hist[...], c], axis=0)
    y = _rows_back(c_rows, 2)[CONV_PAD:] * conv_w[0:1, :]
    y = y + _rows_back(c_rows, 1)[CONV_PAD:] * conv_w[1:2, :]
    y = y + c * conv_w[2:3, :]
    c_hist[...] = c[tile - CONV_PAD:, :]
    return jnp.concatenate(pooled, axis=-1), y
  return mixers


def _pool_history_sums_kernel(pool_hist_ref, sums_ref):
  for g, win in enumerate(POOL_WINDOWS):
    lo = g * POOL_GROUP
    acc = pool_hist_ref[POOL_HIST - 1, :, lo:lo + POOL_GROUP]
    for k in range(2, win):
      acc = acc + pool_hist_ref[POOL_HIST - k, :, lo:lo + POOL_GROUP]
    sums_ref[:, lo:lo + POOL_GROUP] = acc


def _even_sample_mixers(pool_sums_ref, conv_hist_ref, conv_w, p_ref, c_ref, start):
  def mixers(p, c):
    p_ref[...] = p
    c_ref[...] = c
    pooled = []
    for g, win in enumerate(POOL_WINDOWS):
      lo = g * POOL_GROUP
      acc = p[:, lo:lo + POOL_GROUP] + pool_sums_ref[:, lo:lo + POOL_GROUP]
      pooled.append(acc / float(min(start + 1, win)))
    y = conv_hist_ref[0] * conv_w[0:1, :]
    y = y + conv_hist_ref[1] * conv_w[1:2, :]
    y = y + c * conv_w[2:3, :]
    return jnp.concatenate(pooled, axis=-1), y
  return mixers


def _even_mix_head(t, p_hist, c_hist):
  @pl.when(t == 0)
  def _():
    p_hist[...] = jnp.zeros(p_hist.shape, jnp.float32)
    c_hist[...] = jnp.zeros(c_hist.shape, jnp.float32)


def _even_mix_tail(t, pool_state_ref, conv_state_ref, p_hist, c_hist, *, last_t):
  @pl.when(t == last_t)
  def _():
    pool_state_ref[...] = p_hist[POOL_PAD - POOL_HIST:, :]
    conv_state_ref[...] = c_hist[CONV_PAD - CONV_HIST:, :]


EVEN_SMALL_WEIGHT_NAMES = ("g_mix", "w_pool", "pool_scale", "conv_w", "g_ffn")
EVEN_BIG_WEIGHT_NAMES = ("w_in", "w_out", "wg", "wu", "wd")
ODD_WEIGHT_NAMES = ("g_mix", "w_in", "g_v", "w_s", "b_s_t", "w_out",
                    "g_ffn", "wg", "wu", "wd", "g_final")
ODD_MIXER_WEIGHT_NAMES = ("w_in", "w_out")
ODD_BIG_WEIGHT_NAMES = ODD_MIXER_WEIGHT_NAMES + ("wg", "wu", "wd")
N_CAST = 5


def _fetch_bf16(srcs, dsts, stage, sems):
  chunks = []
  for src, dst in zip(srcs, dsts):
    n_rows, n_cols = src.shape
    if dst.shape[1] > n_cols:
      dst[:, n_cols:] = jnp.zeros((n_rows, dst.shape[1] - n_cols), dst.dtype)
    for r0 in range(0, n_rows, FETCH_ROWS):
      chunks.append((src, dst, r0, min(FETCH_ROWS, n_rows - r0), n_cols))

  n_slots = stage.shape[0]

  def copy(i):
    src, _, r0, n, n_cols = chunks[i]
    return pltpu.make_async_copy(src.at[pl.ds(r0, n), :],
                                 stage.at[i % n_slots, pl.ds(0, n), pl.ds(0, n_cols)],
                                 sems.at[i % n_slots])

  for i in range(min(n_slots - 1, len(chunks))):
    copy(i).start()
  for i, (_, dst, r0, n, n_cols) in enumerate(chunks):
    if i + n_slots - 1 < len(chunks):
      copy(i + n_slots - 1).start()
    copy(i).wait()
    dst[pl.ds(r0, n), :n_cols] = stage[i % n_slots, :n, :n_cols].astype(dst.dtype)


def _run(stages):
  for stage in stages:
    stage()


def _pipelined_steps(s, n_tiles, mix_stages, ffn_stages):
  @pl.when(s == 0)
  def _():
    _run(mix_stages())

  @pl.when((s > 0) & (s < n_tiles))
  def _():
    mix, ffn = mix_stages(), ffn_stages()
    _run([ffn[0], ffn[1], mix[0], mix[1], mix[2], ffn[2], ffn[3], mix[3], mix[4]])

  @pl.when(s == n_tiles)
  def _():
    _run(ffn_stages())


def _set(ref):
  def emit(value):
    ref[...] = value
  return emit


def _set_rows(ref):
  def emit(value):
    ref[:, 0, :] = value
  return emit


def _even_kernel(*refs, n_tiles, tiles_per_seq, tile, start, d_pool, d_conv, layer):
  x_ref, xs_ref, pool_sums_ref, conv_hist_ref = refs[:4]
  n_small, n_big = len(EVEN_SMALL_WEIGHT_NAMES), len(EVEN_BIG_WEIGHT_NAMES)
  w = dict(zip(EVEN_SMALL_WEIGHT_NAMES, refs[4:4 + n_small]))
  big_hbm = refs[4 + n_small:4 + n_small + n_big]
  cast_in = refs[4 + n_small + n_big:4 + n_small + n_big + N_CAST]
  outs = refs[4 + n_small + n_big + N_CAST:]
  y_ref, pool_state_ref, conv_state_ref, ys_ref, ps_ref, cs_ref = outs[:6]
  cast_out = outs[6:6 + N_CAST]
  scratch = outs[6 + N_CAST:]
  p_hist, c_hist, x1_buf = scratch[:3]
  big_vmem = scratch[3:3 + n_big]
  stage, sems = scratch[3 + n_big:]
  w.update(zip(EVEN_BIG_WEIGHT_NAMES, big_vmem))

  s = pl.program_id(0)
  t = s % tiles_per_seq
  slot = s % 2

  @pl.when(s == 0)
  def _():
    _fetch_bf16([m.at[layer[i]] for i, m in enumerate(big_hbm)], big_vmem, stage, sems)

  @pl.when(s < n_tiles)
  def _():
    _even_mix_head(t, p_hist, c_hist)
    for src, dst in zip(cast_in, cast_out):
      n = src.shape[1]
      dst[:, :n] = src[...].astype(dst.dtype)
      if dst.shape[1] > n:
        dst[:, n:] = jnp.zeros((dst.shape[0], dst.shape[1] - n), dst.dtype)

  def mix_stages():
    return _even_mix_stages(
        lambda: x_ref[...], w, _set(x1_buf.at[slot]),
        _even_prompt_mixers(t, p_hist, c_hist, w["conv_w"], tile), d_pool, d_conv)

  def ffn_stages():
    return _ffn_stages(lambda: x1_buf[1 - slot], w, _set(y_ref), False)

  _pipelined_steps(s, n_tiles, mix_stages, ffn_stages)

  @pl.when(s < n_tiles)
  def _():
    _even_mix_tail(t, pool_state_ref, conv_state_ref, p_hist, c_hist,
                   last_t=tiles_per_seq - 1)

  @pl.when(s == n_tiles + 1)
  def _():
    st = {}
    mixers = _even_sample_mixers(pool_sums_ref, conv_hist_ref, w["conv_w"],
                                 ps_ref, cs_ref, start)
    _run(_even_mix_stages(lambda: xs_ref[:, 0, :], w, lambda v: st.update(x1=v),
                          mixers, d_pool, d_conv))
    _run(_ffn_stages(lambda: st["x1"], w, _set(ys_ref), False))


def _odd_mix_stages(load_x, w, emit, gating, d_gate):
  st = {}

  def norm():
    st["x"] = load_x()
    st["h"] = _rmsnorm(st["x"], w["g_mix"][...]).astype(jnp.bfloat16)

  def in_proj():
    st["z"] = _skewed_dot(st["h"], w["w_in"], 2 * d_gate)

  def activate():
    z = _gelu_exact(st["z"])
    st["u"] = z[:, :d_gate]
    st["v"] = _rmsnorm(z[:, d_gate:], w["g_v"][...])

  def gate():
    st["gated"] = (st["u"] * gating(st["v"])).astype(jnp.bfloat16)

  def out_proj():
    emit(st["x"] + _skewed_dot(st["gated"], w["w_out"], st["x"].shape[1]))

  return [norm, in_proj, activate, gate, out_proj]


def _odd_prompt_gating(w, tile, d_gate):
  def gating(v):
    vb = v.astype(jnp.bfloat16)
    n_chunks = tile // CHUNK
    head = d_gate // N_SG_HEADS
    row = lax.broadcasted_iota(jnp.int32, (CHUNK, CHUNK), 0)
    col = lax.broadcasted_iota(jnp.int32, (CHUNK, CHUNK), 1)
    causal = row >= col
    per_head = []
    for hd in range(N_SG_HEADS):
      w_h = jnp.where(causal, w["w_s"][hd], 0.0).astype(jnp.bfloat16)
      rhs = jnp.concatenate(
          [vb[c * CHUNK:(c + 1) * CHUNK, hd * head:(hd + 1) * head]
           for c in range(n_chunks)], axis=1)
      o = jnp.dot(w_h, rhs, preferred_element_type=jnp.float32)
      per_head.append(o + w["b_s_t"][:, hd:hd + 1])
    return jnp.concatenate(
        [jnp.concatenate([o[:, c * head:(c + 1) * head] for o in per_head], axis=1)
         for c in range(n_chunks)], axis=0)
  return gating


def _odd_sample_gating(w, v_ref, d_gate):
  def gating(v):
    v_ref[:, 0, :] = v
    head = d_gate // N_SG_HEADS
    rows = v.shape[0]
    vb = v.astype(jnp.bfloat16).astype(jnp.float32)
    per_head = []
    for hd in range(N_SG_HEADS):
      w00 = w["w_s"][hd, 0:1, 0:1].astype(jnp.bfloat16).astype(jnp.float32)
      b0 = w["b_s_t"][0:1, hd:hd + 1]
      per_head.append(vb[:, hd * head:(hd + 1) * head] * jnp.broadcast_to(w00, (rows, head))
                      + jnp.broadcast_to(b0, (rows, head)))
    return jnp.concatenate(per_head, axis=1)
  return gating


def _odd_kernel(*refs, n_tiles, tile, d_gate):
  x_ref, xs_ref = refs[:2]
  n_w = len(ODD_WEIGHT_NAMES)
  w = dict(zip(ODD_WEIGHT_NAMES, refs[2:2 + n_w]))
  y_ref, ys_ref, vs_ref, x1_buf = refs[2 + n_w:2 + n_w + 4]
  big_vmem = refs[2 + n_w + 4:-1]
  sems = refs[-1]
  copies = {name: pltpu.make_async_copy(w[name], buf, sems.at[i])
            for i, (name, buf) in enumerate(zip(ODD_BIG_WEIGHT_NAMES, big_vmem))}
  w.update(zip(ODD_BIG_WEIGHT_NAMES, big_vmem))

  s = pl.program_id(0)
  slot = s % 2

  @pl.when(s == 0)
  def _():
    for name in ODD_BIG_WEIGHT_NAMES:
      copies[name].start()
    for name in ODD_MIXER_WEIGHT_NAMES:
      copies[name].wait()

  def mix_stages():
    return _odd_mix_stages(lambda: x_ref[...], w, _set(x1_buf.at[slot]),
                           _odd_prompt_gating(w, tile, d_gate), d_gate)

  def ffn_stages():
    return _ffn_stages(lambda: x1_buf[1 - slot], w, _set(y_ref), True)

  _pipelined_steps(s, n_tiles, mix_stages, ffn_stages)

  @pl.when(s == 0)
  def _():
    for name in ODD_BIG_WEIGHT_NAMES:
      if name not in ODD_MIXER_WEIGHT_NAMES:
        copies[name].wait()

  @pl.when(s == n_tiles + 1)
  def _():
    st = {}
    _run(_odd_mix_stages(lambda: xs_ref[...], w, lambda v: st.update(x1=v),
                         _odd_sample_gating(w, vs_ref, d_gate), d_gate))
    _run(_ffn_stages(lambda: st["x1"], w, _set_rows(ys_ref), True))


def _resident(arr):
  zeros = (0,) * arr.ndim
  return pl.BlockSpec(arr.shape, lambda s: zeros, pipeline_mode=pl.Buffered(1))


def _whole_out(shape):
  zeros = (0,) * len(shape)
  return pl.BlockSpec(shape, lambda s: zeros)


def _params():
  return pltpu.CompilerParams(dimension_semantics=("arbitrary",),
                              vmem_limit_bytes=V7X_VMEM_LIMIT_BYTES)


def _prompt_specs(x, tile):
  batch, seq, d_model = x.shape
  tiles_per_seq = seq // tile
  n_tiles = batch * tiles_per_seq
  mix_tile = lambda s: jnp.clip(s, 0, n_tiles - 1)
  ffn_tile = lambda s: jnp.clip(s - 1, 0, n_tiles - 1)
  block = lambda which: pl.BlockSpec(
      (None, tile, d_model),
      lambda s: (which(s) // tiles_per_seq, which(s) % tiles_per_seq, 0))
  per_seq = lambda rows, ch: pl.BlockSpec(
      (None, rows, ch), lambda s: (mix_tile(s) // tiles_per_seq, 0, 0))
  return n_tiles, tiles_per_seq, mix_tile, block(mix_tile), block(ffn_tile), per_seq


def _cast_slab_spec(rows, cols, n_steps, step, layer):
  slab = rows // n_steps
  steps_per_slab = 1
  while slab % V7X_BF16_SUBLANES:
    slab *= 2
    steps_per_slab *= 2
  src = pl.BlockSpec((None, slab, cols), lambda s: (layer, step(s) // steps_per_slab, 0))
  dst = pl.BlockSpec((slab, cols + _skew_cols(cols)),
                     lambda s: (step(s) // steps_per_slab, 0))
  return src, dst


def _pool_history_sums(pool_hist):
  rows, d_pool = pool_hist.shape[1:]
  return pl.pallas_call(
      _pool_history_sums_kernel,
      grid=(1,),
      in_specs=[pl.BlockSpec(pool_hist.shape, lambda i: (0, 0, 0))],
      out_specs=pl.BlockSpec((rows, d_pool), lambda i: (0, 0)),
      out_shape=jax.ShapeDtypeStruct((rows, d_pool), pool_hist.dtype),
      name="pool_history_sums",
  )(pool_hist)


def _even_layer(x, xs, pool_sums, conv_hist, small_weights, mats, mat_layer,
                next_mats, next_layer, tile, start):
  n_tiles, tiles_per_seq, step, tok_in, tok_out, per_seq = _prompt_specs(x, tile)
  batch = x.shape[0]
  rows, _, d_model = xs.shape
  d_pool = pool_sums.shape[-1]
  d_conv = conv_hist.shape[-1]
  cast_specs = [_cast_slab_spec(m.shape[1], m.shape[2], n_tiles, step, next_layer[i])
                for i, m in enumerate(next_mats)]
  f32 = x.dtype
  skewed = lambda m: (m.shape[1], m.shape[2] + _skew_cols(m.shape[2]))
  widest = max(m.shape[2] for m in mats)
  return pl.pallas_call(
      functools.partial(_even_kernel, n_tiles=n_tiles,
                        tiles_per_seq=tiles_per_seq, tile=tile, start=start,
                        d_pool=d_pool, d_conv=d_conv, layer=tuple(mat_layer)),
      grid=(n_tiles + 2,),
      in_specs=([tok_in, _resident(xs), _resident(pool_sums), _resident(conv_hist)]
                + [_resident(w) for w in small_weights]
                + [pl.BlockSpec(memory_space=pl.ANY) for _ in mats]
                + [c[0] for c in cast_specs]),
      out_specs=([tok_out, per_seq(POOL_HIST, d_pool), per_seq(CONV_HIST, d_conv),
                  _whole_out((rows, d_model)), _whole_out((rows, d_pool)),
                  _whole_out((rows, d_conv))] + [c[1] for c in cast_specs]),
      out_shape=([jax.ShapeDtypeStruct(x.shape, f32),
                  jax.ShapeDtypeStruct((batch, POOL_HIST, d_pool), f32),
                  jax.ShapeDtypeStruct((batch, CONV_HIST, d_conv), f32),
                  jax.ShapeDtypeStruct((rows, d_model), f32),
                  jax.ShapeDtypeStruct((rows, d_pool), f32),
                  jax.ShapeDtypeStruct((rows, d_conv), f32)]
                 + [jax.ShapeDtypeStruct(skewed(m), jnp.bfloat16) for m in next_mats]),
      scratch_shapes=([pltpu.VMEM((POOL_PAD, d_pool), jnp.float32),
                       pltpu.VMEM((CONV_PAD, d_conv), jnp.float32),
                       pltpu.VMEM((2, tile, d_model), jnp.float32)]
                      + [pltpu.VMEM(skewed(m), jnp.bfloat16) for m in mats]
                      + [pltpu.VMEM((FETCH_SLOTS, FETCH_ROWS, widest), jnp.float32),
                         pltpu.SemaphoreType.DMA((FETCH_SLOTS,))]),
      compiler_params=_params(),
      name="even_layer",
  )(x, xs, pool_sums, conv_hist, *small_weights, *mats, *next_mats)


def _odd_layer(x, xs, weights, tile):
  n_tiles, _, _, tok_in, tok_out, _ = _prompt_specs(x, tile)
  rows, d_model = xs.shape
  d_gate = weights[ODD_WEIGHT_NAMES.index("g_v")].shape[-1]
  big = {name: w for name, w in zip(ODD_WEIGHT_NAMES, weights)
         if name in ODD_BIG_WEIGHT_NAMES}
  return pl.pallas_call(
      functools.partial(_odd_kernel, n_tiles=n_tiles, tile=tile, d_gate=d_gate),
      grid=(n_tiles + 2,),
      in_specs=([tok_in, _resident(xs)]
                + [pl.BlockSpec(memory_space=pl.ANY) if name in ODD_BIG_WEIGHT_NAMES
                   else _resident(w) for name, w in zip(ODD_WEIGHT_NAMES, weights)]),
      out_specs=[tok_out, _whole_out((rows, 1, d_model)), _whole_out((rows, 1, d_gate))],
      out_shape=[jax.ShapeDtypeStruct(x.shape, x.dtype),
                 jax.ShapeDtypeStruct((rows, 1, d_model), x.dtype),
                 jax.ShapeDtypeStruct((rows, 1, d_gate), x.dtype)],
      scratch_shapes=([pltpu.VMEM((2, tile, d_model), jnp.float32)]
                      + [pltpu.VMEM(big[name].shape, big[name].dtype)
                         for name in ODD_BIG_WEIGHT_NAMES]
                      + [pltpu.SemaphoreType.DMA((len(ODD_BIG_WEIGHT_NAMES),))]),
      compiler_params=_params(),
      name="odd_layer",
  )(x, xs, *weights)


def _pool_block_diag(w_pool):
  per_tile = V7X_MXU_DIM // POOL_GROUP
  n_tiles = w_pool.shape[0] // per_tile
  out = jnp.zeros((n_tiles, V7X_MXU_DIM, V7X_MXU_DIM), jnp.bfloat16)
  for g in range(w_pool.shape[0]):
    i, j = divmod(g, per_tile)
    lo = j * POOL_GROUP
    out = out.at[i, lo:lo + POOL_GROUP, lo:lo + POOL_GROUP].set(
        w_pool[g].astype(jnp.bfloat16))
  return out


def kernel(x_prompt, x_sample, state_pool, state_conv, norm_mix, norm_ffn, norm_final, w_in_even, w_pool, pool_scale, conv_w, w_out_even, w_in_odd, norm_sg, w_s, b_s, w_out_odd, ffn_w_gate, ffn_w_up, ffn_w_down):
  depth = norm_mix.shape[0]
  assert depth == 2 and w_in_even.shape[0] == 1 and w_in_odd.shape[0] == 1
  assert x_sample.shape[1] == 1
  assert x_prompt.shape[1] % PROMPT_TILE == 0 and PROMPT_TILE % CHUNK == 0

  row = lambda v: v.reshape(1, -1)

  even_small = (row(norm_mix[0]), _pool_block_diag(w_pool[0]), row(pool_scale[0]),
                conv_w[0], row(norm_ffn[0]))
  even_mats = (w_in_even, w_out_even, ffn_w_gate, ffn_w_up, ffn_w_down)
  even_mat_layer = (0, 0, 0, 0, 0)
  odd_mats = (w_in_odd, w_out_odd, ffn_w_gate, ffn_w_up, ffn_w_down)
  odd_mat_layer = (0, 0, 1, 1, 1)

  pool_hist = jnp.swapaxes(state_pool[0], 0, 1)
  conv_hist = jnp.swapaxes(state_conv[0], 0, 1)

  (x1_p, pool_p, conv_p, x1_s, p_s, c_s, w_in_o, w_out_o, wg_o, wu_o,
   wd_o) = _even_layer(x_prompt, x_sample, _pool_history_sums(pool_hist),
                       conv_hist, even_small, even_mats, even_mat_layer,
                       odd_mats, odd_mat_layer, PROMPT_TILE, PAST_LEN)

  odd_w = (row(norm_mix[1]), w_in_o, row(norm_sg[0]), w_s[0], b_s[0].T, w_out_o,
           row(norm_ffn[1]), wg_o, wu_o, wd_o, row(norm_final))
  y_p, y_s, v_s = _odd_layer(x1_p, x1_s, odd_w, PROMPT_TILE)

  pool_s = jnp.concatenate([state_pool[:, :, 1:], p_s[None, :, None, :]], axis=2)
  conv_s = jnp.concatenate([state_conv[:, :, 1:], c_s[None, :, None, :]], axis=2)

  return (y_p, y_s, pool_p[None], pool_s, conv_p[None], conv_s, v_s[None])
```

```python
import functools
import math

import jax
import jax.numpy as jnp
from jax import lax
from jax.experimental import pallas as pl
from jax.experimental.pallas import tpu as pltpu

POOL_WINDOWS = (2, 4, 8, 16)
POOL_GROUP = 128
POOL_HIST = max(POOL_WINDOWS) - 1
CONV_WIDTH = 3
CONV_HIST = CONV_WIDTH - 1
CHUNK = 128
N_SG_HEADS = 8
EPS = 1e-6
PAST_LEN = 16384

V7X_SUBLANES = 8
V7X_LANES = 128
V7X_BF16_SUBLANES = 16
V7X_MXU_DIM = 256
V7X_VMEM_LIMIT_BYTES = 60000 * 1024

POOL_PAD = -(-POOL_HIST // V7X_SUBLANES) * V7X_SUBLANES
CONV_PAD = -(-CONV_HIST // V7X_SUBLANES) * V7X_SUBLANES

PROMPT_TILE = 512
FETCH_ROWS = 128
FETCH_SLOTS = 4


def _bf16_dot(a, b):
  return jnp.dot(a.astype(jnp.bfloat16), b, preferred_element_type=jnp.float32)


def _rmsnorm(x, g):
  y = x * lax.rsqrt(jnp.mean(x * x, axis=-1, keepdims=True) + EPS)
  return y * g


def _gelu_exact(x):
  return 0.5 * x * (1.0 + lax.erf(x * math.sqrt(0.5)))


def _skew_cols(n):
  return V7X_LANES if (n // V7X_LANES) % V7X_SUBLANES == 0 else 0


def _skewed_dot(a, w_ref, n):
  return jnp.dot(a, w_ref[:, :n], preferred_element_type=jnp.float32)


def _pool_project(d, w_bd_ref, scale):
  halves = []
  for i in range(w_bd_ref.shape[0]):
    lo = i * V7X_MXU_DIM
    halves.append(_bf16_dot(d[:, lo:lo + V7X_MXU_DIM], w_bd_ref[i]))
  return jnp.concatenate(halves, axis=-1) * scale


def _ffn_stages(load_x1, w, emit, final_norm):
  st = {}

  def norm():
    st["x"] = load_x1()
    st["h"] = _rmsnorm(st["x"], w["g_ffn"][...]).astype(jnp.bfloat16)

  def gate_up():
    st["gate"] = jnp.dot(st["h"], w["wg"][...], preferred_element_type=jnp.float32)
    st["up"] = jnp.dot(st["h"], w["wu"][...], preferred_element_type=jnp.float32)

  def activate():
    gate = st["gate"]
    st["act"] = (gate * jax.nn.sigmoid(gate) * st["up"]).astype(jnp.bfloat16)

  def down():
    y = st["x"] + _skewed_dot(st["act"], w["wd"], st["x"].shape[1])
    emit(_rmsnorm(y, w["g_final"][...]) if final_norm else y)

  return [norm, gate_up, activate, down]


def _even_mix_stages(load_x, w, emit, mixers, d_pool, d_conv):
  st = {}

  def norm():
    st["x"] = load_x()
    st["h"] = _rmsnorm(st["x"], w["g_mix"][...]).astype(jnp.bfloat16)

  def in_proj():
    st["z"] = _skewed_dot(st["h"], w["w_in"], d_pool + 3 * d_conv)

  def mix():
    z = st["z"]
    p = z[:, :d_pool]
    xb = z[:, d_pool:d_pool + d_conv]
    bg = z[:, d_pool + d_conv:d_pool + 2 * d_conv]
    cg = z[:, d_pool + 2 * d_conv:]
    pooled, conv_y = mixers(p, cg * xb)
    st["pool_d"] = (pooled - p).astype(jnp.bfloat16)
    st["conv_out"] = (bg * conv_y).astype(jnp.bfloat16)

  def project():
    a_out = _pool_project(st["pool_d"], w["w_pool"], w["pool_scale"][...])
    st["mixed"] = jnp.concatenate([a_out.astype(jnp.bfloat16), st["conv_out"]], axis=-1)

  def out_proj():
    emit(st["x"] + _skewed_dot(st["mixed"], w["w_out"], st["x"].shape[1]))

  return [norm, in_proj, mix, project, out_proj]


def _rows_back(x, k):
  return pltpu.roll(x, k, axis=0)


def _even_prompt_mixers(t, p_hist, c_hist, conv_w, tile):
  assert all(w & (w - 1) == 0 and w <= POOL_PAD for w in POOL_WINDOWS)

  def mixers(p, c):
    pos = t * tile + lax.broadcasted_iota(jnp.int32, (POOL_PAD, POOL_GROUP), 0)
    pooled = []
    for g, win in enumerate(POOL_WINDOWS):
      lo = g * POOL_GROUP
      sums = jnp.concatenate([p_hist[:, lo:lo + POOL_GROUP], p[:, lo:lo + POOL_GROUP]],
                             axis=0)
      span = 1
      while span < win:
        sums = sums + _rows_back(sums, span)
        span *= 2
      sums = sums[POOL_PAD:]
      head = sums[:POOL_PAD] / jnp.minimum(pos + 1, win).astype(jnp.float32)
      pooled.append(jnp.concatenate([head, sums[POOL_PAD:] * (1.0 / win)], axis=0))
    p_hist[...] = p[tile - POOL_PAD:, :]
    c_rows = jnp.concatenate([c_hist[...], c], axis=0)
    y = _rows_back(c_rows, 2)[CONV_PAD:] * conv_w[0:1, :]
    y = y + _rows_back(c_rows, 1)[CONV_PAD:] * conv_w[1:2, :]
    y = y + c * conv_w[2:3, :]
    c_hist[...] = c[tile - CONV_PAD:, :]
    return jnp.concatenate(pooled, axis=-1), y
  return mixers


def _even_sample_mixers(pool_row, conv_hist_ref, conv_w, p_ref, c_ref, start):
  def mixers(p, c):
    p_ref[...] = p
    c_ref[...] = c
    pooled = []
    for g, win in enumerate(POOL_WINDOWS):
      lo = g * POOL_GROUP
      acc = p[:, lo:lo + POOL_GROUP]
      for k in range(1, win):
        acc = acc + pool_row(POOL_HIST - k)[:, lo:lo + POOL_GROUP]
      pooled.append(acc / float(min(start + 1, win)))
    y = conv_hist_ref[0] * conv_w[0:1, :]
    y = y + conv_hist_ref[1] * conv_w[1:2, :]
    y = y + c * conv_w[2:3, :]
    return jnp.concatenate(pooled, axis=-1), y
  return mixers


def _even_mix_head(t, p_hist, c_hist):
  @pl.when(t == 0)
  def _():
    p_hist[...] = jnp.zeros(p_hist.shape, jnp.float32)
    c_hist[...] = jnp.zeros(c_hist.shape, jnp.float32)


def _even_mix_tail(t, pool_state_ref, conv_state_ref, p_hist, c_hist, *, last_t):
  @pl.when(t == last_t)
  def _():
    pool_state_ref[...] = p_hist[POOL_PAD - POOL_HIST:, :]
    conv_state_ref[...] = c_hist[CONV_PAD - CONV_HIST:, :]


EVEN_SMALL_WEIGHT_NAMES = ("g_mix", "w_pool", "pool_scale", "conv_w", "g_ffn")
EVEN_BIG_WEIGHT_NAMES = ("w_in", "w_out", "wg", "wu", "wd")
ODD_WEIGHT_NAMES = ("g_mix", "w_in", "g_v", "w_s", "b_s_t", "w_out",
                    "g_ffn", "wg", "wu", "wd", "g_final")
ODD_MIXER_WEIGHT_NAMES = ("w_in", "w_out")
ODD_BIG_WEIGHT_NAMES = ODD_MIXER_WEIGHT_NAMES + ("wg", "wu", "wd")
N_CAST = 5


def _fetch_bf16(srcs, dsts, stage, sems):
  chunks = []
  for src, dst in zip(srcs, dsts):
    n_rows, n_cols = src.shape
    if dst.shape[1] > n_cols:
      dst[:, n_cols:] = jnp.zeros((n_rows, dst.shape[1] - n_cols), dst.dtype)
    for r0 in range(0, n_rows, FETCH_ROWS):
      chunks.append((src, dst, r0, min(FETCH_ROWS, n_rows - r0), n_cols))

  n_slots = stage.shape[0]

  def copy(i):
    src, _, r0, n, n_cols = chunks[i]
    return pltpu.make_async_copy(src.at[pl.ds(r0, n), :],
                                 stage.at[i % n_slots, pl.ds(0, n), pl.ds(0, n_cols)],
                                 sems.at[i % n_slots])

  for i in range(min(n_slots - 1, len(chunks))):
    copy(i).start()
  for i, (_, dst, r0, n, n_cols) in enumerate(chunks):
    if i + n_slots - 1 < len(chunks):
      copy(i + n_slots - 1).start()
    copy(i).wait()
    dst[pl.ds(r0, n), :n_cols] = stage[i % n_slots, :n, :n_cols].astype(dst.dtype)


def _run(stages):
  for stage in stages:
    stage()


def _pipelined_steps(s, n_tiles, mix_stages, ffn_stages):
  @pl.when(s == 0)
  def _():
    _run(mix_stages())

  @pl.when((s > 0) & (s < n_tiles))
  def _():
    mix, ffn = mix_stages(), ffn_stages()
    _run([ffn[0], ffn[1], mix[0], mix[1], mix[2], ffn[2], ffn[3], mix[3], mix[4]])

  @pl.when(s == n_tiles)
  def _():
    _run(ffn_stages())


def _set(ref):
  def emit(value):
    ref[...] = value
  return emit


def _set_rows(ref):
  def emit(value):
    ref[:, 0, :] = value
  return emit


def _even_kernel(*refs, n_tiles, tiles_per_seq, tile, start, d_pool, d_conv, layer):
  x_ref, xs_ref, pool_hist_hbm, conv_hist_ref = refs[:4]
  n_small, n_big = len(EVEN_SMALL_WEIGHT_NAMES), len(EVEN_BIG_WEIGHT_NAMES)
  w = dict(zip(EVEN_SMALL_WEIGHT_NAMES, refs[4:4 + n_small]))
  big_hbm = refs[4 + n_small:4 + n_small + n_big]
  cast_in = refs[4 + n_small + n_big:4 + n_small + n_big + N_CAST]
  outs = refs[4 + n_small + n_big + N_CAST:]
  y_ref, pool_state_ref, conv_state_ref, ys_ref, ps_ref, cs_ref = outs[:6]
  cast_out = outs[6:6 + N_CAST]
  pool_next_hbm = outs[6 + N_CAST]
  scratch = outs[7 + N_CAST:]
  p_hist, c_hist, x1_buf = scratch[:3]
  big_vmem = scratch[3:3 + n_big]
  stage, sems, hist_sems = scratch[3 + n_big:]
  w.update(zip(EVEN_BIG_WEIGHT_NAMES, big_vmem))

  rows = xs_ref.shape[0]
  per_slot = stage.shape[2] // d_pool
  assert rows <= stage.shape[1] and POOL_HIST <= stage.shape[0] * per_slot

  def pool_row(k):
    return stage.at[k // per_slot, pl.ds(0, rows), pl.ds((k % per_slot) * d_pool, d_pool)]

  def row_in(k):
    return pltpu.make_async_copy(pool_hist_hbm.at[k], pool_row(k), hist_sems.at[0, k])

  def row_out(k):
    src = pool_row(k + 1) if k + 1 < POOL_HIST else ps_ref
    return pltpu.make_async_copy(src, pool_next_hbm.at[k], hist_sems.at[1, k])

  s = pl.program_id(0)
  t = s % tiles_per_seq
  slot = s % 2

  @pl.when(s == 0)
  def _():
    _fetch_bf16([m.at[layer[i]] for i, m in enumerate(big_hbm)], big_vmem, stage, sems)

  @pl.when(s < n_tiles)
  def _():
    _even_mix_head(t, p_hist, c_hist)
    for src, dst in zip(cast_in, cast_out):
      n = src.shape[1]
      dst[:, :n] = src[...].astype(dst.dtype)
      if dst.shape[1] > n:
        dst[:, n:] = jnp.zeros((dst.shape[0], dst.shape[1] - n), dst.dtype)

  def mix_stages():
    return _even_mix_stages(
        lambda: x_ref[...], w, _set(x1_buf.at[slot]),
        _even_prompt_mixers(t, p_hist, c_hist, w["conv_w"], tile), d_pool, d_conv)

  def ffn_stages():
    return _ffn_stages(lambda: x1_buf[1 - slot], w, _set(y_ref), False)

  @pl.when(s == n_tiles)
  def _():
    for k in range(POOL_HIST):
      row_in(k).start()

  _pipelined_steps(s, n_tiles, mix_stages, ffn_stages)

  @pl.when(s < n_tiles)
  def _():
    _even_mix_tail(t, pool_state_ref, conv_state_ref, p_hist, c_hist,
                   last_t=tiles_per_seq - 1)

  @pl.when(s == n_tiles + 1)
  def _():
    for k in range(POOL_HIST):
      row_in(k).wait()
    for k in range(POOL_HIST - 1):
      row_out(k).start()
    st = {}
    mixers = _even_sample_mixers(pool_row, conv_hist_ref, w["conv_w"],
                                 ps_ref, cs_ref, start)
    _run(_even_mix_stages(lambda: xs_ref[:, 0, :], w, lambda v: st.update(x1=v),
                          mixers, d_pool, d_conv))
    row_out(POOL_HIST - 1).start()
    _run(_ffn_stages(lambda: st["x1"], w, _set(ys_ref), False))
    for k in range(POOL_HIST):
      row_out(k).wait()


def _odd_mix_stages(load_x, w, emit, gating, d_gate):
  st = {}

  def norm():
    st["x"] = load_x()
    st["h"] = _rmsnorm(st["x"], w["g_mix"][...]).astype(jnp.bfloat16)

  def in_proj():
    st["z"] = _skewed_dot(st["h"], w["w_in"], 2 * d_gate)

  def activate():
    z = _gelu_exact(st["z"])
    st["u"] = z[:, :d_gate]
    st["v"] = _rmsnorm(z[:, d_gate:], w["g_v"][...])

  def gate():
    st["gated"] = (st["u"] * gating(st["v"])).astype(jnp.bfloat16)

  def out_proj():
    emit(st["x"] + _skewed_dot(st["gated"], w["w_out"], st["x"].shape[1]))

  return [norm, in_proj, activate, gate, out_proj]


def _odd_prompt_gating(w, tile, d_gate):
  def gating(v):
    vb = v.astype(jnp.bfloat16)
    n_chunks = tile // CHUNK
    head = d_gate // N_SG_HEADS
    row = lax.broadcasted_iota(jnp.int32, (CHUNK, CHUNK), 0)
    col = lax.broadcasted_iota(jnp.int32, (CHUNK, CHUNK), 1)
    causal = row >= col
    per_head = []
    for hd in range(N_SG_HEADS):
      w_h = jnp.where(causal, w["w_s"][hd], 0.0).astype(jnp.bfloat16)
      rhs = jnp.concatenate(
          [vb[c * CHUNK:(c + 1) * CHUNK, hd * head:(hd + 1) * head]
           for c in range(n_chunks)], axis=1)
      o = jnp.dot(w_h, rhs, preferred_element_type=jnp.float32)
      per_head.append(o + w["b_s_t"][:, hd:hd + 1])
    return jnp.concatenate(
        [jnp.concatenate([o[:, c * head:(c + 1) * head] for o in per_head], axis=1)
         for c in range(n_chunks)], axis=0)
  return gating


def _odd_sample_gating(w, v_ref, d_gate):
  def gating(v):
    v_ref[:, 0, :] = v
    head = d_gate // N_SG_HEADS
    rows = v.shape[0]
    vb = v.astype(jnp.bfloat16).astype(jnp.float32)
    per_head = []
    for hd in range(N_SG_HEADS):
      w00 = w["w_s"][hd, 0:1, 0:1].astype(jnp.bfloat16).astype(jnp.float32)
      b0 = w["b_s_t"][0:1, hd:hd + 1]
      per_head.append(vb[:, hd * head:(hd + 1) * head] * jnp.broadcast_to(w00, (rows, head))
                      + jnp.broadcast_to(b0, (rows, head)))
    return jnp.concatenate(per_head, axis=1)
  return gating


def _odd_kernel(*refs, n_tiles, tile, d_gate):
  x_ref, xs_ref = refs[:2]
  n_w = len(ODD_WEIGHT_NAMES)
  w = dict(zip(ODD_WEIGHT_NAMES, refs[2:2 + n_w]))
  y_ref, ys_ref, vs_ref, x1_buf = refs[2 + n_w:2 + n_w + 4]
  big_vmem = refs[2 + n_w + 4:-1]
  sems = refs[-1]
  copies = {name: pltpu.make_async_copy(w[name], buf, sems.at[i])
            for i, (name, buf) in enumerate(zip(ODD_BIG_WEIGHT_NAMES, big_vmem))}
  w.update(zip(ODD_BIG_WEIGHT_NAMES, big_vmem))

  s = pl.program_id(0)
  slot = s % 2

  @pl.when(s == 0)
  def _():
    for name in ODD_BIG_WEIGHT_NAMES:
      copies[name].start()
    for name in ODD_MIXER_WEIGHT_NAMES:
      copies[name].wait()

  def mix_stages():
    return _odd_mix_stages(lambda: x_ref[...], w, _set(x1_buf.at[slot]),
                           _odd_prompt_gating(w, tile, d_gate), d_gate)

  def ffn_stages():
    return _ffn_stages(lambda: x1_buf[1 - slot], w, _set(y_ref), True)

  _pipelined_steps(s, n_tiles, mix_stages, ffn_stages)

  @pl.when(s == 0)
  def _():
    for name in ODD_BIG_WEIGHT_NAMES:
      if name not in ODD_MIXER_WEIGHT_NAMES:
        copies[name].wait()

  @pl.when(s == n_tiles + 1)
  def _():
    st = {}
    _run(_odd_mix_stages(lambda: xs_ref[...], w, lambda v: st.update(x1=v),
                         _odd_sample_gating(w, vs_ref, d_gate), d_gate))
    _run(_ffn_stages(lambda: st["x1"], w, _set_rows(ys_ref), True))


def _resident(arr):
  zeros = (0,) * arr.ndim
  return pl.BlockSpec(arr.shape, lambda s: zeros, pipeline_mode=pl.Buffered(1))


def _whole_out(shape):
  zeros = (0,) * len(shape)
  return pl.BlockSpec(shape, lambda s: zeros)


def _params():
  return pltpu.CompilerParams(dimension_semantics=("arbitrary",),
                              vmem_limit_bytes=V7X_VMEM_LIMIT_BYTES)


def _prompt_specs(x, tile):
  batch, seq, d_model = x.shape
  tiles_per_seq = seq // tile
  n_tiles = batch * tiles_per_seq
  mix_tile = lambda s: jnp.clip(s, 0, n_tiles - 1)
  ffn_tile = lambda s: jnp.clip(s - 1, 0, n_tiles - 1)
  block = lambda which: pl.BlockSpec(
      (None, tile, d_model),
      lambda s: (which(s) // tiles_per_seq, which(s) % tiles_per_seq, 0))
  per_seq = lambda rows, ch: pl.BlockSpec(
      (None, rows, ch), lambda s: (mix_tile(s) // tiles_per_seq, 0, 0))
  return n_tiles, tiles_per_seq, mix_tile, block(mix_tile), block(ffn_tile), per_seq


def _cast_slab_spec(rows, cols, n_steps, step, layer):
  slab = rows // n_steps
  steps_per_slab = 1
  while slab % V7X_BF16_SUBLANES:
    slab *= 2
    steps_per_slab *= 2
  src = pl.BlockSpec((None, slab, cols), lambda s: (layer, step(s) // steps_per_slab, 0))
  dst = pl.BlockSpec((slab, cols + _skew_cols(cols)),
                     lambda s: (step(s) // steps_per_slab, 0))
  return src, dst


def _even_layer(x, xs, pool_hist, conv_hist, small_weights, mats, mat_layer,
                next_mats, next_layer, tile, start):
  n_tiles, tiles_per_seq, step, tok_in, tok_out, per_seq = _prompt_specs(x, tile)
  batch = x.shape[0]
  rows, _, d_model = xs.shape
  d_pool = pool_hist.shape[-1]
  d_conv = conv_hist.shape[-1]
  cast_specs = [_cast_slab_spec(m.shape[1], m.shape[2], n_tiles, step, next_layer[i])
                for i, m in enumerate(next_mats)]
  f32 = x.dtype
  skewed = lambda m: (m.shape[1], m.shape[2] + _skew_cols(m.shape[2]))
  widest = max(m.shape[2] for m in mats)
  return pl.pallas_call(
      functools.partial(_even_kernel, n_tiles=n_tiles,
                        tiles_per_seq=tiles_per_seq, tile=tile, start=start,
                        d_pool=d_pool, d_conv=d_conv, layer=tuple(mat_layer)),
      grid=(n_tiles + 2,),
      in_specs=([tok_in, _resident(xs), pl.BlockSpec(memory_space=pl.ANY),
                 _resident(conv_hist)]
                + [_resident(w) for w in small_weights]
                + [pl.BlockSpec(memory_space=pl.ANY) for _ in mats]
                + [c[0] for c in cast_specs]),
      out_specs=([tok_out, per_seq(POOL_HIST, d_pool), per_seq(CONV_HIST, d_conv),
                  _whole_out((rows, d_model)), _whole_out((rows, d_pool)),
                  _whole_out((rows, d_conv))] + [c[1] for c in cast_specs]
                 + [pl.BlockSpec(memory_space=pl.ANY)]),
      out_shape=([jax.ShapeDtypeStruct(x.shape, f32),
                  jax.ShapeDtypeStruct((batch, POOL_HIST, d_pool), f32),
                  jax.ShapeDtypeStruct((batch, CONV_HIST, d_conv), f32),
                  jax.ShapeDtypeStruct((rows, d_model), f32),
                  jax.ShapeDtypeStruct((rows, d_pool), f32),
                  jax.ShapeDtypeStruct((rows, d_conv), f32)]
                 + [jax.ShapeDtypeStruct(skewed(m), jnp.bfloat16) for m in next_mats]
                 + [jax.ShapeDtypeStruct(pool_hist.shape, f32)]),
      scratch_shapes=([pltpu.VMEM((POOL_PAD, d_pool), jnp.float32),
                       pltpu.VMEM((CONV_PAD, d_conv), jnp.float32),
                       pltpu.VMEM((2, tile, d_model), jnp.float32)]
                      + [pltpu.VMEM(skewed(m), jnp.bfloat16) for m in mats]
                      + [pltpu.VMEM((FETCH_SLOTS, FETCH_ROWS, widest), jnp.float32),
                         pltpu.SemaphoreType.DMA((FETCH_SLOTS,)),
                         pltpu.SemaphoreType.DMA((2, POOL_HIST))]),
      compiler_params=_params(),
      name="even_layer",
  )(x, xs, pool_hist, conv_hist, *small_weights, *mats, *next_mats)


def _odd_layer(x, xs, weights, tile):
  n_tiles, _, _, tok_in, tok_out, _ = _prompt_specs(x, tile)
  rows, d_model = xs.shape
  d_gate = weights[ODD_WEIGHT_NAMES.index("g_v")].shape[-1]
  big = {name: w for name, w in zip(ODD_WEIGHT_NAMES, weights)
         if name in ODD_BIG_WEIGHT_NAMES}
  return pl.pallas_call(
      functools.partial(_odd_kernel, n_tiles=n_tiles, tile=tile, d_gate=d_gate),
      grid=(n_tiles + 2,),
      in_specs=([tok_in, _resident(xs)]
                + [pl.BlockSpec(memory_space=pl.ANY) if name in ODD_BIG_WEIGHT_NAMES
                   else _resident(w) for name, w in zip(ODD_WEIGHT_NAMES, weights)]),
      out_specs=[tok_out, _whole_out((rows, 1, d_model)), _whole_out((rows, 1, d_gate))],
      out_shape=[jax.ShapeDtypeStruct(x.shape, x.dtype),
                 jax.ShapeDtypeStruct((rows, 1, d_model), x.dtype),
                 jax.ShapeDtypeStruct((rows, 1, d_gate), x.dtype)],
      scratch_shapes=([pltpu.VMEM((2, tile, d_model), jnp.float32)]
                      + [pltpu.VMEM(big[name].shape, big[name].dtype)
                         for name in ODD_BIG_WEIGHT_NAMES]
                      + [pltpu.SemaphoreType.DMA((len(ODD_BIG_WEIGHT_NAMES),))]),
      compiler_params=_params(),
      name="odd_layer",
  )(x, xs, *weights)


def _pool_block_diag(w_pool):
  per_tile = V7X_MXU_DIM // POOL_GROUP
  n_tiles = w_pool.shape[0] // per_tile
  out = jnp.zeros((n_tiles, V7X_MXU_DIM, V7X_MXU_DIM), jnp.bfloat16)
  for g in range(w_pool.shape[0]):
    i, j = divmod(g, per_tile)
    lo = j * POOL_GROUP
    out = out.at[i, lo:lo + POOL_GROUP, lo:lo + POOL_GROUP].set(
        w_pool[g].astype(jnp.bfloat16))
  return out


def kernel(x_prompt, x_sample, state_pool, state_conv, norm_mix, norm_ffn, norm_final, w_in_even, w_pool, pool_scale, conv_w, w_out_even, w_in_odd, norm_sg, w_s, b_s, w_out_odd, ffn_w_gate, ffn_w_up, ffn_w_down):
  depth = norm_mix.shape[0]
  assert depth == 2 and w_in_even.shape[0] == 1 and w_in_odd.shape[0] == 1
  assert x_sample.shape[1] == 1
  assert x_prompt.shape[1] % PROMPT_TILE == 0 and PROMPT_TILE % CHUNK == 0

  row = lambda v: v.reshape(1, -1)

  even_small = (row(norm_mix[0]), _pool_block_diag(w_pool[0]), row(pool_scale[0]),
                conv_w[0], row(norm_ffn[0]))
  even_mats = (w_in_even, w_out_even, ffn_w_gate, ffn_w_up, ffn_w_down)
  even_mat_layer = (0, 0, 0, 0, 0)
  odd_mats = (w_in_odd, w_out_odd, ffn_w_gate, ffn_w_up, ffn_w_down)
  odd_mat_layer = (0, 0, 1, 1, 1)

  pool_hist = jnp.swapaxes(state_pool[0], 0, 1)
  conv_hist = jnp.swapaxes(state_conv[0], 0, 1)

  (x1_p, pool_p, conv_p, x1_s, _, c_s, w_in_o, w_out_o, wg_o, wu_o, wd_o,
   pool_s) = _even_layer(x_prompt, x_sample, pool_hist, conv_hist, even_small,
                         even_mats, even_mat_layer, odd_mats, odd_mat_layer,
                         PROMPT_TILE, PAST_LEN)

  odd_w = (row(norm_mix[1]), w_in_o, row(norm_sg[0]), w_s[0], b_s[0].T, w_out_o,
           row(norm_ffn[1]), wg_o, wu_o, wd_o, row(norm_final))
  y_p, y_s, v_s = _odd_layer(x1_p, x1_s, odd_w, PROMPT_TILE)

  pool_s = jnp.swapaxes(pool_s, 0, 1)[None]
  conv_s = jnp.concatenate([state_conv[:, :, 1:], c_s[None, :, None, :]], axis=2)

  return (y_p, y_s, pool_p[None], pool_s, conv_p[None], conv_s, v_s[None])
```

```python
import functools
import math

import jax
import jax.numpy as jnp
from jax import lax
from jax.experimental import pallas as pl
from jax.experimental.pallas import tpu as pltpu

POOL_WINDOWS = (2, 4, 8, 16)
POOL_GROUP = 128
POOL_HIST = max(POOL_WINDOWS) - 1
CONV_WIDTH = 3
CONV_HIST = CONV_WIDTH - 1
CHUNK = 128
N_SG_HEADS = 8
EPS = 1e-6
PAST_LEN = 16384

V7X_SUBLANES = 8
V7X_LANES = 128
V7X_BF16_SUBLANES = 16
V7X_MXU_DIM = 256
V7X_VMEM_LIMIT_BYTES = 60000 * 1024

POOL_PAD = -(-POOL_HIST // V7X_SUBLANES) * V7X_SUBLANES
CONV_PAD = -(-CONV_HIST // V7X_SUBLANES) * V7X_SUBLANES

PROMPT_TILE = 512
FETCH_ROWS = 128
FETCH_SLOTS = 4


def _bf16_dot(a, b):
  return jnp.dot(a.astype(jnp.bfloat16), b, preferred_element_type=jnp.float32)


def _rmsnorm(x, g):
  y = x * lax.rsqrt(jnp.mean(x * x, axis=-1, keepdims=True) + EPS)
  return y * g


def _gelu_exact(x):
  return 0.5 * x * (1.0 + lax.erf(x * math.sqrt(0.5)))


def _skew_cols(n):
  return V7X_LANES if (n // V7X_LANES) % V7X_SUBLANES == 0 else 0


def _skewed_dot(a, w_ref, n):
  return jnp.dot(a, w_ref[:, :n], preferred_element_type=jnp.float32)


def _pool_project(d, w_bd_ref, scale):
  halves = []
  for i in range(w_bd_ref.shape[0]):
    lo = i * V7X_MXU_DIM
    halves.append(_bf16_dot(d[:, lo:lo + V7X_MXU_DIM], w_bd_ref[i]))
  return jnp.concatenate(halves, axis=-1) * scale


def _ffn_stages(load_x1, w, emit, final_norm):
  st = {}

  def norm():
    st["x"] = load_x1()
    st["h"] = _rmsnorm(st["x"], w["g_ffn"][...]).astype(jnp.bfloat16)

  def gate_up():
    st["gate"] = jnp.dot(st["h"], w["wg"][...], preferred_element_type=jnp.float32)
    st["up"] = jnp.dot(st["h"], w["wu"][...], preferred_element_type=jnp.float32)

  def activate():
    gate = st["gate"]
    st["act"] = (gate * jax.nn.sigmoid(gate) * st["up"]).astype(jnp.bfloat16)

  def down():
    y = st["x"] + _skewed_dot(st["act"], w["wd"], st["x"].shape[1])
    emit(_rmsnorm(y, w["g_final"][...]) if final_norm else y)

  return [norm, gate_up, activate, down]


def _even_mix_stages(load_x, w, emit, mixers, d_pool, d_conv):
  st = {}

  def norm():
    st["x"] = load_x()
    st["h"] = _rmsnorm(st["x"], w["g_mix"][...]).astype(jnp.bfloat16)

  def in_proj():
    st["z"] = _skewed_dot(st["h"], w["w_in"], d_pool + 3 * d_conv)

  def mix():
    z = st["z"]
    p = z[:, :d_pool]
    xb = z[:, d_pool:d_pool + d_conv]
    bg = z[:, d_pool + d_conv:d_pool + 2 * d_conv]
    cg = z[:, d_pool + 2 * d_conv:]
    pooled, conv_y = mixers(p, cg * xb)
    st["pool_d"] = (pooled - p).astype(jnp.bfloat16)
    st["conv_out"] = (bg * conv_y).astype(jnp.bfloat16)

  def project():
    a_out = _pool_project(st["pool_d"], w["w_pool"], w["pool_scale"][...])
    st["mixed"] = jnp.concatenate([a_out.astype(jnp.bfloat16), st["conv_out"]], axis=-1)

  def out_proj():
    emit(st["x"] + _skewed_dot(st["mixed"], w["w_out"], st["x"].shape[1]))

  return [norm, in_proj, mix, project, out_proj]


def _rows_back(x, k):
  return pltpu.roll(x, k, axis=0)


def _even_prompt_mixers(t, p_hist, c_hist, conv_w, tile):
  assert all(w & (w - 1) == 0 and w <= POOL_PAD for w in POOL_WINDOWS)

  def mixers(p, c):
    pos = t * tile + lax.broadcasted_iota(jnp.int32, (POOL_PAD, POOL_GROUP), 0)
    pooled = []
    for g, win in enumerate(POOL_WINDOWS):
      lo = g * POOL_GROUP
      sums = jnp.concatenate([p_hist[:, lo:lo + POOL_GROUP], p[:, lo:lo + POOL_GROUP]],
                             axis=0)
      span = 1
      while span < win:
        sums = sums + _rows_back(sums, span)
        span *= 2
      sums = sums[POOL_PAD:]
      head = sums[:POOL_PAD] / jnp.minimum(pos + 1, win).astype(jnp.float32)
      pooled.append(jnp.concatenate([head, sums[POOL_PAD:] * (1.0 / win)], axis=0))
    p_hist[...] = p[tile - POOL_PAD:, :]
    c_rows = jnp.concatenate([c_hist[...], c], axis=0)
    y = _rows_back(c_rows, 2)[CONV_PAD:] * conv_w[0:1, :]
    y = y + _rows_back(c_rows, 1)[CONV_PAD:] * conv_w[1:2, :]
    y = y + c * conv_w[2:3, :]
    c_hist[...] = c[tile - CONV_PAD:, :]
    return jnp.concatenate(pooled, axis=-1), y
  return mixers


def _even_sample_mixers(pool_row, conv_hist_ref, conv_w, p_ref, c_ref, start):
  def mixers(p, c):
    p_ref[...] = p
    c_ref[...] = c
    pooled = []
    for g, win in enumerate(POOL_WINDOWS):
      lo = g * POOL_GROUP
      acc = p[:, lo:lo + POOL_GROUP]
      for k in range(1, win):
        acc = acc + pool_row(POOL_HIST - k)[:, lo:lo + POOL_GROUP]
      pooled.append(acc / float(min(start + 1, win)))
    y = conv_hist_ref[0] * conv_w[0:1, :]
    y = y + conv_hist_ref[1] * conv_w[1:2, :]
    y = y + c * conv_w[2:3, :]
    return jnp.concatenate(pooled, axis=-1), y
  return mixers


def _even_mix_head(t, p_hist, c_hist):
  @pl.when(t == 0)
  def _():
    p_hist[...] = jnp.zeros(p_hist.shape, jnp.float32)
    c_hist[...] = jnp.zeros(c_hist.shape, jnp.float32)


def _even_mix_tail(t, pool_state_ref, conv_state_ref, p_hist, c_hist, *, last_t):
  @pl.when(t == last_t)
  def _():
    pool_state_ref[...] = p_hist[POOL_PAD - POOL_HIST:, :]
    conv_state_ref[...] = c_hist[CONV_PAD - CONV_HIST:, :]


EVEN_SMALL_WEIGHT_NAMES = ("g_mix", "w_pool", "pool_scale", "conv_w", "g_ffn")
EVEN_BIG_WEIGHT_NAMES = ("w_in", "w_out", "wg", "wu", "wd")
ODD_WEIGHT_NAMES = ("g_mix", "w_in", "g_v", "w_s", "b_s_t", "w_out",
                    "g_ffn", "wg", "wu", "wd", "g_final")
ODD_MIXER_WEIGHT_NAMES = ("w_in", "w_out")
ODD_BIG_WEIGHT_NAMES = ODD_MIXER_WEIGHT_NAMES + ("wg", "wu", "wd")
N_CAST = 5


def _pool_block_diag(groups_ref, tiles_ref):
  per_tile = V7X_MXU_DIM // POOL_GROUP
  tiles_ref[...] = jnp.zeros(tiles_ref.shape, tiles_ref.dtype)
  for g in range(groups_ref.shape[0]):
    i, j = divmod(g, per_tile)
    lo = j * POOL_GROUP
    tiles_ref[i, lo:lo + POOL_GROUP, lo:lo + POOL_GROUP] = (
        groups_ref[g].astype(tiles_ref.dtype))


def _fetch_bf16(srcs, dsts, stage, sems):
  chunks = []
  for src, dst in zip(srcs, dsts):
    n_rows, n_cols = src.shape
    if dst.shape[1] > n_cols:
      dst[:, n_cols:] = jnp.zeros((n_rows, dst.shape[1] - n_cols), dst.dtype)
    for r0 in range(0, n_rows, FETCH_ROWS):
      chunks.append((src, dst, r0, min(FETCH_ROWS, n_rows - r0), n_cols))

  n_slots = stage.shape[0]

  def copy(i):
    src, _, r0, n, n_cols = chunks[i]
    return pltpu.make_async_copy(src.at[pl.ds(r0, n), :],
                                 stage.at[i % n_slots, pl.ds(0, n), pl.ds(0, n_cols)],
                                 sems.at[i % n_slots])

  for i in range(min(n_slots - 1, len(chunks))):
    copy(i).start()
  for i, (_, dst, r0, n, n_cols) in enumerate(chunks):
    if i + n_slots - 1 < len(chunks):
      copy(i + n_slots - 1).start()
    copy(i).wait()
    dst[pl.ds(r0, n), :n_cols] = stage[i % n_slots, :n, :n_cols].astype(dst.dtype)


def _run(stages):
  for stage in stages:
    stage()


def _pipelined_steps(s, n_tiles, mix_stages, ffn_stages):
  @pl.when(s == 0)
  def _():
    _run(mix_stages())

  @pl.when((s > 0) & (s < n_tiles))
  def _():
    mix, ffn = mix_stages(), ffn_stages()
    _run([ffn[0], ffn[1], mix[0], mix[1], mix[2], ffn[2], ffn[3], mix[3], mix[4]])

  @pl.when(s == n_tiles)
  def _():
    _run(ffn_stages())


def _set(ref):
  def emit(value):
    ref[...] = value
  return emit


def _set_rows(ref):
  def emit(value):
    ref[:, 0, :] = value
  return emit


def _even_kernel(*refs, n_tiles, tiles_per_seq, tile, start, d_pool, d_conv, layer,
                 norm_row):
  x_ref, xs_ref, pool_hist_hbm, conv_hist_ref = refs[:4]
  n_small, n_big = len(EVEN_SMALL_WEIGHT_NAMES), len(EVEN_BIG_WEIGHT_NAMES)
  w = dict(zip(EVEN_SMALL_WEIGHT_NAMES, refs[4:4 + n_small]))
  big_hbm = refs[4 + n_small:4 + n_small + n_big]
  cast_in = refs[4 + n_small + n_big:4 + n_small + n_big + N_CAST]
  outs = refs[4 + n_small + n_big + N_CAST:]
  y_ref, pool_state_ref, conv_state_ref, ys_ref, ps_ref, cs_ref = outs[:6]
  cast_out = outs[6:6 + N_CAST]
  pool_next_hbm = outs[6 + N_CAST]
  scratch = outs[7 + N_CAST:]
  p_hist, c_hist, x1_buf = scratch[:3]
  big_vmem = scratch[3:3 + n_big]
  stage, sems, hist_sems, pool_tiles = scratch[3 + n_big:]
  w.update(zip(EVEN_BIG_WEIGHT_NAMES, big_vmem))
  w["g_mix"] = w["g_mix"].at[pl.ds(norm_row, 1)]
  w["g_ffn"] = w["g_ffn"].at[pl.ds(norm_row, 1)]
  w["conv_w"] = w["conv_w"].at[0]
  pool_groups = w["w_pool"]
  w["w_pool"] = pool_tiles

  rows = xs_ref.shape[0]
  per_slot = stage.shape[2] // d_pool
  assert rows <= stage.shape[1] and POOL_HIST <= stage.shape[0] * per_slot

  def pool_row(k):
    return stage.at[k // per_slot, pl.ds(0, rows), pl.ds((k % per_slot) * d_pool, d_pool)]

  def row_in(k):
    return pltpu.make_async_copy(pool_hist_hbm.at[k], pool_row(k), hist_sems.at[0, k])

  def row_out(k):
    src = pool_row(k + 1) if k + 1 < POOL_HIST else ps_ref
    return pltpu.make_async_copy(src, pool_next_hbm.at[k], hist_sems.at[1, k])

  s = pl.program_id(0)
  t = s % tiles_per_seq
  slot = s % 2

  @pl.when(s == 0)
  def _():
    _pool_block_diag(pool_groups, pool_tiles)
    _fetch_bf16([m.at[layer[i]] for i, m in enumerate(big_hbm)], big_vmem, stage, sems)

  @pl.when(s < n_tiles)
  def _():
    _even_mix_head(t, p_hist, c_hist)
    for src, dst in zip(cast_in, cast_out):
      n = src.shape[1]
      dst[:, :n] = src[...].astype(dst.dtype)
      if dst.shape[1] > n:
        dst[:, n:] = jnp.zeros((dst.shape[0], dst.shape[1] - n), dst.dtype)

  def mix_stages():
    return _even_mix_stages(
        lambda: x_ref[...], w, _set(x1_buf.at[slot]),
        _even_prompt_mixers(t, p_hist, c_hist, w["conv_w"], tile), d_pool, d_conv)

  def ffn_stages():
    return _ffn_stages(lambda: x1_buf[1 - slot], w, _set(y_ref), False)

  @pl.when(s == n_tiles)
  def _():
    for k in range(POOL_HIST):
      row_in(k).start()

  _pipelined_steps(s, n_tiles, mix_stages, ffn_stages)

  @pl.when(s < n_tiles)
  def _():
    _even_mix_tail(t, pool_state_ref, conv_state_ref, p_hist, c_hist,
                   last_t=tiles_per_seq - 1)

  @pl.when(s == n_tiles + 1)
  def _():
    for k in range(POOL_HIST):
      row_in(k).wait()
    for k in range(POOL_HIST - 1):
      row_out(k).start()
    st = {}
    mixers = _even_sample_mixers(pool_row, conv_hist_ref, w["conv_w"],
                                 ps_ref, cs_ref, start)
    _run(_even_mix_stages(lambda: xs_ref[:, 0, :], w, lambda v: st.update(x1=v),
                          mixers, d_pool, d_conv))
    row_out(POOL_HIST - 1).start()
    _run(_ffn_stages(lambda: st["x1"], w, _set(ys_ref), False))
    for k in range(POOL_HIST):
      row_out(k).wait()


def _odd_mix_stages(load_x, w, emit, gating, d_gate):
  st = {}

  def norm():
    st["x"] = load_x()
    st["h"] = _rmsnorm(st["x"], w["g_mix"][...]).astype(jnp.bfloat16)

  def in_proj():
    st["z"] = _skewed_dot(st["h"], w["w_in"], 2 * d_gate)

  def activate():
    z = _gelu_exact(st["z"])
    st["u"] = z[:, :d_gate]
    st["v"] = _rmsnorm(z[:, d_gate:], w["g_v"][...])

  def gate():
    st["gated"] = (st["u"] * gating(st["v"])).astype(jnp.bfloat16)

  def out_proj():
    emit(st["x"] + _skewed_dot(st["gated"], w["w_out"], st["x"].shape[1]))

  return [norm, in_proj, activate, gate, out_proj]


def _odd_prompt_gating(w, tile, d_gate):
  def gating(v):
    vb = v.astype(jnp.bfloat16)
    n_chunks = tile // CHUNK
    head = d_gate // N_SG_HEADS
    row = lax.broadcasted_iota(jnp.int32, (CHUNK, CHUNK), 0)
    col = lax.broadcasted_iota(jnp.int32, (CHUNK, CHUNK), 1)
    causal = row >= col
    per_head = []
    for hd in range(N_SG_HEADS):
      w_h = jnp.where(causal, w["w_s"][hd], 0.0).astype(jnp.bfloat16)
      rhs = jnp.concatenate(
          [vb[c * CHUNK:(c + 1) * CHUNK, hd * head:(hd + 1) * head]
           for c in range(n_chunks)], axis=1)
      o = jnp.dot(w_h, rhs, preferred_element_type=jnp.float32)
      per_head.append(o + w["b_s_t"][:, hd:hd + 1])
    return jnp.concatenate(
        [jnp.concatenate([o[:, c * head:(c + 1) * head] for o in per_head], axis=1)
         for c in range(n_chunks)], axis=0)
  return gating


def _odd_sample_gating(w, v_ref, d_gate):
  def gating(v):
    v_ref[:, 0, :] = v
    head = d_gate // N_SG_HEADS
    rows = v.shape[0]
    vb = v.astype(jnp.bfloat16).astype(jnp.float32)
    per_head = []
    for hd in range(N_SG_HEADS):
      w00 = w["w_s"][hd, 0:1, 0:1].astype(jnp.bfloat16).astype(jnp.float32)
      b0 = w["b_s_t"][0:1, hd:hd + 1]
      per_head.append(vb[:, hd * head:(hd + 1) * head] * jnp.broadcast_to(w00, (rows, head))
                      + jnp.broadcast_to(b0, (rows, head)))
    return jnp.concatenate(per_head, axis=1)
  return gating


def _odd_kernel(*refs, n_tiles, tile, d_gate, norm_row):
  x_ref, xs_ref = refs[:2]
  n_w = len(ODD_WEIGHT_NAMES)
  w = dict(zip(ODD_WEIGHT_NAMES, refs[2:2 + n_w]))
  y_ref, ys_ref, vs_ref, x1_buf = refs[2 + n_w:2 + n_w + 4]
  big_vmem = refs[2 + n_w + 4:-1]
  sems = refs[-1]
  copies = {name: pltpu.make_async_copy(w[name], buf, sems.at[i])
            for i, (name, buf) in enumerate(zip(ODD_BIG_WEIGHT_NAMES, big_vmem))}
  w.update(zip(ODD_BIG_WEIGHT_NAMES, big_vmem))
  w["g_mix"] = w["g_mix"].at[pl.ds(norm_row, 1)]
  w["g_ffn"] = w["g_ffn"].at[pl.ds(norm_row, 1)]

  s = pl.program_id(0)
  slot = s % 2

  @pl.when(s == 0)
  def _():
    for name in ODD_BIG_WEIGHT_NAMES:
      copies[name].start()
    for name in ODD_MIXER_WEIGHT_NAMES:
      copies[name].wait()

  def mix_stages():
    return _odd_mix_stages(lambda: x_ref[...], w, _set(x1_buf.at[slot]),
                           _odd_prompt_gating(w, tile, d_gate), d_gate)

  def ffn_stages():
    return _ffn_stages(lambda: x1_buf[1 - slot], w, _set(y_ref), True)

  _pipelined_steps(s, n_tiles, mix_stages, ffn_stages)

  @pl.when(s == 0)
  def _():
    for name in ODD_BIG_WEIGHT_NAMES:
      if name not in ODD_MIXER_WEIGHT_NAMES:
        copies[name].wait()

  @pl.when(s == n_tiles + 1)
  def _():
    st = {}
    _run(_odd_mix_stages(lambda: xs_ref[...], w, lambda v: st.update(x1=v),
                         _odd_sample_gating(w, vs_ref, d_gate), d_gate))
    _run(_ffn_stages(lambda: st["x1"], w, _set_rows(ys_ref), True))


def _resident(arr):
  zeros = (0,) * arr.ndim
  return pl.BlockSpec(arr.shape, lambda s: zeros, pipeline_mode=pl.Buffered(1))


def _whole_out(shape):
  zeros = (0,) * len(shape)
  return pl.BlockSpec(shape, lambda s: zeros)


def _params():
  return pltpu.CompilerParams(dimension_semantics=("arbitrary",),
                              vmem_limit_bytes=V7X_VMEM_LIMIT_BYTES)


def _prompt_specs(x, tile):
  batch, seq, d_model = x.shape
  tiles_per_seq = seq // tile
  n_tiles = batch * tiles_per_seq
  mix_tile = lambda s: jnp.clip(s, 0, n_tiles - 1)
  ffn_tile = lambda s: jnp.clip(s - 1, 0, n_tiles - 1)
  block = lambda which: pl.BlockSpec(
      (None, tile, d_model),
      lambda s: (which(s) // tiles_per_seq, which(s) % tiles_per_seq, 0))
  per_seq = lambda rows, ch: pl.BlockSpec(
      (None, rows, ch), lambda s: (mix_tile(s) // tiles_per_seq, 0, 0))
  return n_tiles, tiles_per_seq, mix_tile, block(mix_tile), block(ffn_tile), per_seq


def _cast_slab_spec(rows, cols, n_steps, step, layer):
  slab = rows // n_steps
  steps_per_slab = 1
  while slab % V7X_BF16_SUBLANES:
    slab *= 2
    steps_per_slab *= 2
  src = pl.BlockSpec((None, slab, cols), lambda s: (layer, step(s) // steps_per_slab, 0))
  dst = pl.BlockSpec((slab, cols + _skew_cols(cols)),
                     lambda s: (step(s) // steps_per_slab, 0))
  return src, dst


def _even_layer(x, xs, pool_hist, conv_hist, small_weights, norm_row, mats, mat_layer,
                next_mats, next_layer, tile, start):
  n_tiles, tiles_per_seq, step, tok_in, tok_out, per_seq = _prompt_specs(x, tile)
  batch = x.shape[0]
  rows, _, d_model = xs.shape
  d_pool = pool_hist.shape[-1]
  d_conv = conv_hist.shape[-1]
  cast_specs = [_cast_slab_spec(m.shape[1], m.shape[2], n_tiles, step, next_layer[i])
                for i, m in enumerate(next_mats)]
  f32 = x.dtype
  skewed = lambda m: (m.shape[1], m.shape[2] + _skew_cols(m.shape[2]))
  widest = max(m.shape[2] for m in mats)
  return pl.pallas_call(
      functools.partial(_even_kernel, n_tiles=n_tiles,
                        tiles_per_seq=tiles_per_seq, tile=tile, start=start,
                        d_pool=d_pool, d_conv=d_conv, layer=tuple(mat_layer),
                        norm_row=norm_row),
      grid=(n_tiles + 2,),
      in_specs=([tok_in, _resident(xs), pl.BlockSpec(memory_space=pl.ANY),
                 _resident(conv_hist)]
                + [_resident(w) for w in small_weights]
                + [pl.BlockSpec(memory_space=pl.ANY) for _ in mats]
                + [c[0] for c in cast_specs]),
      out_specs=([tok_out, per_seq(POOL_HIST, d_pool), per_seq(CONV_HIST, d_conv),
                  _whole_out((rows, d_model)), _whole_out((rows, d_pool)),
                  _whole_out((rows, d_conv))] + [c[1] for c in cast_specs]
                 + [pl.BlockSpec(memory_space=pl.ANY)]),
      out_shape=([jax.ShapeDtypeStruct(x.shape, f32),
                  jax.ShapeDtypeStruct((batch, POOL_HIST, d_pool), f32),
                  jax.ShapeDtypeStruct((batch, CONV_HIST, d_conv), f32),
                  jax.ShapeDtypeStruct((rows, d_model), f32),
                  jax.ShapeDtypeStruct((rows, d_pool), f32),
                  jax.ShapeDtypeStruct((rows, d_conv), f32)]
                 + [jax.ShapeDtypeStruct(skewed(m), jnp.bfloat16) for m in next_mats]
                 + [jax.ShapeDtypeStruct(pool_hist.shape, f32)]),
      scratch_shapes=([pltpu.VMEM((POOL_PAD, d_pool), jnp.float32),
                       pltpu.VMEM((CONV_PAD, d_conv), jnp.float32),
                       pltpu.VMEM((2, tile, d_model), jnp.float32)]
                      + [pltpu.VMEM(skewed(m), jnp.bfloat16) for m in mats]
                      + [pltpu.VMEM((FETCH_SLOTS, FETCH_ROWS, widest), jnp.float32),
                         pltpu.SemaphoreType.DMA((FETCH_SLOTS,)),
                         pltpu.SemaphoreType.DMA((2, POOL_HIST)),
                         pltpu.VMEM((len(POOL_WINDOWS) * POOL_GROUP // V7X_MXU_DIM,
                                     V7X_MXU_DIM, V7X_MXU_DIM), jnp.bfloat16)]),
      compiler_params=_params(),
      name="even_layer",
  )(x, xs, pool_hist, conv_hist, *small_weights, *mats, *next_mats)


def _odd_layer(x, xs, weights, norm_row, tile):
  n_tiles, _, _, tok_in, tok_out, _ = _prompt_specs(x, tile)
  rows, d_model = xs.shape
  d_gate = weights[ODD_WEIGHT_NAMES.index("g_v")].shape[-1]
  big = {name: w for name, w in zip(ODD_WEIGHT_NAMES, weights)
         if name in ODD_BIG_WEIGHT_NAMES}
  return pl.pallas_call(
      functools.partial(_odd_kernel, n_tiles=n_tiles, tile=tile, d_gate=d_gate,
                        norm_row=norm_row),
      grid=(n_tiles + 2,),
      in_specs=([tok_in, _resident(xs)]
                + [pl.BlockSpec(memory_space=pl.ANY) if name in ODD_BIG_WEIGHT_NAMES
                   else _resident(w) for name, w in zip(ODD_WEIGHT_NAMES, weights)]),
      out_specs=[tok_out, _whole_out((rows, 1, d_model)), _whole_out((rows, 1, d_gate))],
      out_shape=[jax.ShapeDtypeStruct(x.shape, x.dtype),
                 jax.ShapeDtypeStruct((rows, 1, d_model), x.dtype),
                 jax.ShapeDtypeStruct((rows, 1, d_gate), x.dtype)],
      scratch_shapes=([pltpu.VMEM((2, tile, d_model), jnp.float32)]
                      + [pltpu.VMEM(big[name].shape, big[name].dtype)
                         for name in ODD_BIG_WEIGHT_NAMES]
                      + [pltpu.SemaphoreType.DMA((len(ODD_BIG_WEIGHT_NAMES),))]),
      compiler_params=_params(),
      name="odd_layer",
  )(x, xs, *weights)


def kernel(x_prompt, x_sample, state_pool, state_conv, norm_mix, norm_ffn, norm_final, w_in_even, w_pool, pool_scale, conv_w, w_out_even, w_in_odd, norm_sg, w_s, b_s, w_out_odd, ffn_w_gate, ffn_w_up, ffn_w_down):
  depth = norm_mix.shape[0]
  assert depth == 2 and w_in_even.shape[0] == 1 and w_in_odd.shape[0] == 1
  assert x_sample.shape[1] == 1
  assert x_prompt.shape[1] % PROMPT_TILE == 0 and PROMPT_TILE % CHUNK == 0

  even_small = (norm_mix, w_pool[0], pool_scale, conv_w, norm_ffn)
  even_mats = (w_in_even, w_out_even, ffn_w_gate, ffn_w_up, ffn_w_down)
  even_mat_layer = (0, 0, 0, 0, 0)
  odd_mats = (w_in_odd, w_out_odd, ffn_w_gate, ffn_w_up, ffn_w_down)
  odd_mat_layer = (0, 0, 1, 1, 1)

  pool_hist = jnp.swapaxes(state_pool[0], 0, 1)
  conv_hist = jnp.swapaxes(state_conv[0], 0, 1)

  (x1_p, pool_p, conv_p, x1_s, _, c_s, w_in_o, w_out_o, wg_o, wu_o, wd_o,
   pool_s) = _even_layer(x_prompt, x_sample, pool_hist, conv_hist, even_small, 0,
                         even_mats, even_mat_layer, odd_mats, odd_mat_layer,
                         PROMPT_TILE, PAST_LEN)

  odd_w = (norm_mix, w_in_o, norm_sg, w_s[0], b_s[0].T, w_out_o,
           norm_ffn, wg_o, wu_o, wd_o, norm_final.reshape(1, -1))
  y_p, y_s, v_s = _odd_layer(x1_p, x1_s, odd_w, 1, PROMPT_TILE)

  pool_s = jnp.swapaxes(pool_s, 0, 1)[None]
  conv_s = jnp.concatenate([state_conv[:, :, 1:], c_s[None, :, None, :]], axis=2)

  return (y_p, y_s, pool_p[None], pool_s, conv_p[None], conv_s, v_s[None])
```

```python
import functools
import math

import jax
import jax.numpy as jnp
from jax import lax
from jax.experimental import pallas as pl
from jax.experimental.pallas import tpu as pltpu

POOL_WINDOWS = (2, 4, 8, 16)
POOL_GROUP = 128
POOL_HIST = max(POOL_WINDOWS) - 1
CONV_WIDTH = 3
CONV_HIST = CONV_WIDTH - 1
CHUNK = 128
N_SG_HEADS = 8
EPS = 1e-6
PAST_LEN = 16384

V7X_SUBLANES = 8
V7X_LANES = 128
V7X_BF16_SUBLANES = 16
V7X_MXU_DIM = 256
V7X_VMEM_LIMIT_BYTES = 60000 * 1024

POOL_PAD = -(-POOL_HIST // V7X_SUBLANES) * V7X_SUBLANES
CONV_PAD = -(-CONV_HIST // V7X_SUBLANES) * V7X_SUBLANES

PROMPT_TILE = 512
FETCH_ROWS = 128
FETCH_SLOTS = 4


def _bf16_dot(a, b):
  return jnp.dot(a.astype(jnp.bfloat16), b, preferred_element_type=jnp.float32)


def _rmsnorm(x, g):
  y = x * lax.rsqrt(jnp.mean(x * x, axis=-1, keepdims=True) + EPS)
  return y * g


def _gelu_exact(x):
  return 0.5 * x * (1.0 + lax.erf(x * math.sqrt(0.5)))


def _skew_cols(n):
  return V7X_LANES if (n // V7X_LANES) % V7X_SUBLANES == 0 else 0


def _skewed_dot(a, w_ref, n):
  return jnp.dot(a, w_ref[:, :n], preferred_element_type=jnp.float32)


def _pool_project(d, w_bd_ref, scale):
  halves = []
  for i in range(w_bd_ref.shape[0]):
    lo = i * V7X_MXU_DIM
    halves.append(_bf16_dot(d[:, lo:lo + V7X_MXU_DIM], w_bd_ref[i]))
  return jnp.concatenate(halves, axis=-1) * scale


def _ffn_stages(load_x1, w, emit, final_norm):
  st = {}

  def norm():
    st["x"] = load_x1()
    st["h"] = _rmsnorm(st["x"], w["g_ffn"][...]).astype(jnp.bfloat16)

  def gate_up():
    st["gate"] = jnp.dot(st["h"], w["wg"][...], preferred_element_type=jnp.float32)
    st["up"] = jnp.dot(st["h"], w["wu"][...], preferred_element_type=jnp.float32)

  def activate():
    gate = st["gate"]
    st["act"] = (gate * jax.nn.sigmoid(gate) * st["up"]).astype(jnp.bfloat16)

  def down():
    y = st["x"] + _skewed_dot(st["act"], w["wd"], st["x"].shape[1])
    emit(_rmsnorm(y, w["g_final"][...]) if final_norm else y)

  return [norm, gate_up, activate, down]


def _even_mix_stages(load_x, w, emit, mixers, d_pool, d_conv):
  st = {}

  def norm():
    st["x"] = load_x()
    st["h"] = _rmsnorm(st["x"], w["g_mix"][...]).astype(jnp.bfloat16)

  def in_proj():
    st["z"] = _skewed_dot(st["h"], w["w_in"], d_pool + 3 * d_conv)

  def mix():
    z = st["z"]
    p = z[:, :d_pool]
    xb = z[:, d_pool:d_pool + d_conv]
    bg = z[:, d_pool + d_conv:d_pool + 2 * d_conv]
    cg = z[:, d_pool + 2 * d_conv:]
    pooled, conv_y = mixers(p, cg * xb)
    st["pool_d"] = (pooled - p).astype(jnp.bfloat16)
    st["conv_out"] = (bg * conv_y).astype(jnp.bfloat16)

  def project():
    a_out = _pool_project(st["pool_d"], w["w_pool"], w["pool_scale"][...])
    st["mixed"] = jnp.concatenate([a_out.astype(jnp.bfloat16), st["conv_out"]], axis=-1)

  def out_proj():
    emit(st["x"] + _skewed_dot(st["mixed"], w["w_out"], st["x"].shape[1]))

  return [norm, in_proj, mix, project, out_proj]


def _rows_back(x, k):
  return pltpu.roll(x, k, axis=0)


def _even_prompt_mixers(t, p_hist, c_hist, conv_w, tile):
  assert all(w & (w - 1) == 0 and w <= POOL_PAD for w in POOL_WINDOWS)

  def mixers(p, c):
    pos = t * tile + lax.broadcasted_iota(jnp.int32, (POOL_PAD, POOL_GROUP), 0)
    pooled = []
    for g, win in enumerate(POOL_WINDOWS):
      lo = g * POOL_GROUP
      sums = jnp.concatenate([p_hist[:, lo:lo + POOL_GROUP], p[:, lo:lo + POOL_GROUP]],
                             axis=0)
      span = 1
      while span < win:
        sums = sums + _rows_back(sums, span)
        span *= 2
      sums = sums[POOL_PAD:]
      head = sums[:POOL_PAD] / jnp.minimum(pos + 1, win).astype(jnp.float32)
      pooled.append(jnp.concatenate([head, sums[POOL_PAD:] * (1.0 / win)], axis=0))
    p_hist[...] = p[tile - POOL_PAD:, :]
    c_rows = jnp.concatenate([c_hist[...], c], axis=0)
    y = _rows_back(c_rows, 2)[CONV_PAD:] * conv_w[0:1, :]
    y = y + _rows_back(c_rows, 1)[CONV_PAD:] * conv_w[1:2, :]
    y = y + c * conv_w[2:3, :]
    c_hist[...] = c[tile - CONV_PAD:, :]
    return jnp.concatenate(pooled, axis=-1), y
  return mixers


def _even_sample_mixers(pool_row, conv_hist_ref, conv_w, p_ref, conv_next_ref, start):
  def mixers(p, c):
    p_ref[...] = p
    for k in range(CONV_HIST - 1):
      conv_next_ref[:, k, :] = conv_hist_ref[:, k + 1, :]
    conv_next_ref[:, CONV_HIST - 1, :] = c
    pooled = []
    for g, win in enumerate(POOL_WINDOWS):
      lo = g * POOL_GROUP
      acc = p[:, lo:lo + POOL_GROUP]
      for k in range(1, win):
        acc = acc + pool_row(POOL_HIST - k)[:, lo:lo + POOL_GROUP]
      pooled.append(acc / float(min(start + 1, win)))
    y = conv_hist_ref[:, 0, :] * conv_w[0:1, :]
    y = y + conv_hist_ref[:, 1, :] * conv_w[1:2, :]
    y = y + c * conv_w[2:3, :]
    return jnp.concatenate(pooled, axis=-1), y
  return mixers


def _even_mix_head(t, p_hist, c_hist):
  @pl.when(t == 0)
  def _():
    p_hist[...] = jnp.zeros(p_hist.shape, jnp.float32)
    c_hist[...] = jnp.zeros(c_hist.shape, jnp.float32)


def _even_mix_tail(t, pool_state_ref, conv_state_ref, p_hist, c_hist, *, last_t):
  @pl.when(t == last_t)
  def _():
    pool_state_ref[...] = p_hist[POOL_PAD - POOL_HIST:, :]
    conv_state_ref[...] = c_hist[CONV_PAD - CONV_HIST:, :]


EVEN_SMALL_WEIGHT_NAMES = ("g_mix", "w_pool", "pool_scale", "conv_w", "g_ffn")
EVEN_BIG_WEIGHT_NAMES = ("w_in", "w_out", "wg", "wu", "wd")
ODD_WEIGHT_NAMES = ("g_mix", "w_in", "g_v", "w_s", "b_s_t", "w_out",
                    "g_ffn", "wg", "wu", "wd", "g_final")
ODD_MIXER_WEIGHT_NAMES = ("w_in", "w_out")
ODD_BIG_WEIGHT_NAMES = ODD_MIXER_WEIGHT_NAMES + ("wg", "wu", "wd")
N_CAST = 5


def _pool_block_diag(groups_ref, tiles_ref):
  per_tile = V7X_MXU_DIM // POOL_GROUP
  tiles_ref[...] = jnp.zeros(tiles_ref.shape, tiles_ref.dtype)
  for g in range(groups_ref.shape[0]):
    i, j = divmod(g, per_tile)
    lo = j * POOL_GROUP
    tiles_ref[i, lo:lo + POOL_GROUP, lo:lo + POOL_GROUP] = (
        groups_ref[g].astype(tiles_ref.dtype))


def _fetch_bf16(srcs, dsts, stage, sems):
  chunks = []
  for src, dst in zip(srcs, dsts):
    n_rows, n_cols = src.shape
    if dst.shape[1] > n_cols:
      dst[:, n_cols:] = jnp.zeros((n_rows, dst.shape[1] - n_cols), dst.dtype)
    for r0 in range(0, n_rows, FETCH_ROWS):
      chunks.append((src, dst, r0, min(FETCH_ROWS, n_rows - r0), n_cols))

  n_slots = stage.shape[0]

  def copy(i):
    src, _, r0, n, n_cols = chunks[i]
    return pltpu.make_async_copy(src.at[pl.ds(r0, n), :],
                                 stage.at[i % n_slots, pl.ds(0, n), pl.ds(0, n_cols)],
                                 sems.at[i % n_slots])

  for i in range(min(n_slots - 1, len(chunks))):
    copy(i).start()
  for i, (_, dst, r0, n, n_cols) in enumerate(chunks):
    if i + n_slots - 1 < len(chunks):
      copy(i + n_slots - 1).start()
    copy(i).wait()
    dst[pl.ds(r0, n), :n_cols] = stage[i % n_slots, :n, :n_cols].astype(dst.dtype)


def _run(stages):
  for stage in stages:
    stage()


def _pipelined_steps(s, n_tiles, mix_stages, ffn_stages):
  @pl.when(s == 0)
  def _():
    _run(mix_stages())

  @pl.when((s > 0) & (s < n_tiles))
  def _():
    mix, ffn = mix_stages(), ffn_stages()
    _run([ffn[0], ffn[1], mix[0], mix[1], mix[2], ffn[2], ffn[3], mix[3], mix[4]])

  @pl.when(s == n_tiles)
  def _():
    _run(ffn_stages())


def _set(ref):
  def emit(value):
    ref[...] = value
  return emit


def _set_rows(ref):
  def emit(value):
    ref[:, 0, :] = value
  return emit


def _even_kernel(*refs, n_tiles, tiles_per_seq, tile, start, d_pool, d_conv, layer,
                 norm_row):
  x_ref, xs_ref, pool_hist_hbm, conv_hist_ref = refs[:4]
  n_small, n_big = len(EVEN_SMALL_WEIGHT_NAMES), len(EVEN_BIG_WEIGHT_NAMES)
  w = dict(zip(EVEN_SMALL_WEIGHT_NAMES, refs[4:4 + n_small]))
  big_hbm = refs[4 + n_small:4 + n_small + n_big]
  cast_in = refs[4 + n_small + n_big:4 + n_small + n_big + N_CAST]
  outs = refs[4 + n_small + n_big + N_CAST:]
  y_ref, pool_state_ref, conv_state_ref, ys_ref, ps_ref, cs_ref = outs[:6]
  cast_out = outs[6:6 + N_CAST]
  pool_next_hbm = outs[6 + N_CAST]
  scratch = outs[7 + N_CAST:]
  p_hist, c_hist, x1_buf = scratch[:3]
  big_vmem = scratch[3:3 + n_big]
  stage, sems, hist_sems, pool_tiles = scratch[3 + n_big:]
  w.update(zip(EVEN_BIG_WEIGHT_NAMES, big_vmem))
  w["g_mix"] = w["g_mix"].at[pl.ds(norm_row, 1)]
  w["g_ffn"] = w["g_ffn"].at[pl.ds(norm_row, 1)]
  w["conv_w"] = w["conv_w"].at[0]
  pool_groups = w["w_pool"]
  w["w_pool"] = pool_tiles

  rows = xs_ref.shape[0]
  per_slot = stage.shape[2] // d_pool
  assert rows <= stage.shape[1] and POOL_HIST <= stage.shape[0] * per_slot

  def pool_row(k):
    return stage.at[k // per_slot, pl.ds(0, rows), pl.ds((k % per_slot) * d_pool, d_pool)]

  def row_in(k):
    return pltpu.make_async_copy(pool_hist_hbm.at[k], pool_row(k), hist_sems.at[0, k])

  def row_out(k):
    src = pool_row(k + 1) if k + 1 < POOL_HIST else ps_ref
    return pltpu.make_async_copy(src, pool_next_hbm.at[k], hist_sems.at[1, k])

  s = pl.program_id(0)
  t = s % tiles_per_seq
  slot = s % 2

  @pl.when(s == 0)
  def _():
    _pool_block_diag(pool_groups, pool_tiles)
    _fetch_bf16([m.at[layer[i]] for i, m in enumerate(big_hbm)], big_vmem, stage, sems)

  @pl.when(s < n_tiles)
  def _():
    _even_mix_head(t, p_hist, c_hist)
    for src, dst in zip(cast_in, cast_out):
      n = src.shape[1]
      dst[:, :n] = src[...].astype(dst.dtype)
      if dst.shape[1] > n:
        dst[:, n:] = jnp.zeros((dst.shape[0], dst.shape[1] - n), dst.dtype)

  def mix_stages():
    return _even_mix_stages(
        lambda: x_ref[...], w, _set(x1_buf.at[slot]),
        _even_prompt_mixers(t, p_hist, c_hist, w["conv_w"], tile), d_pool, d_conv)

  def ffn_stages():
    return _ffn_stages(lambda: x1_buf[1 - slot], w, _set(y_ref), False)

  @pl.when(s == n_tiles)
  def _():
    for k in range(POOL_HIST):
      row_in(k).start()

  _pipelined_steps(s, n_tiles, mix_stages, ffn_stages)

  @pl.when(s < n_tiles)
  def _():
    _even_mix_tail(t, pool_state_ref, conv_state_ref, p_hist, c_hist,
                   last_t=tiles_per_seq - 1)

  @pl.when(s == n_tiles + 1)
  def _():
    for k in range(POOL_HIST):
      row_in(k).wait()
    for k in range(POOL_HIST - 1):
      row_out(k).start()
    st = {}
    mixers = _even_sample_mixers(pool_row, conv_hist_ref, w["conv_w"],
                                 ps_ref, cs_ref, start)
    _run(_even_mix_stages(lambda: xs_ref[:, 0, :], w, lambda v: st.update(x1=v),
                          mixers, d_pool, d_conv))
    row_out(POOL_HIST - 1).start()
    _run(_ffn_stages(lambda: st["x1"], w, _set(ys_ref), False))
    for k in range(POOL_HIST):
      row_out(k).wait()


def _odd_mix_stages(load_x, w, emit, gating, d_gate):
  st = {}

  def norm():
    st["x"] = load_x()
    st["h"] = _rmsnorm(st["x"], w["g_mix"][...]).astype(jnp.bfloat16)

  def in_proj():
    st["z"] = _skewed_dot(st["h"], w["w_in"], 2 * d_gate)

  def activate():
    z = _gelu_exact(st["z"])
    st["u"] = z[:, :d_gate]
    st["v"] = _rmsnorm(z[:, d_gate:], w["g_v"][...])

  def gate():
    st["gated"] = (st["u"] * gating(st["v"])).astype(jnp.bfloat16)

  def out_proj():
    emit(st["x"] + _skewed_dot(st["gated"], w["w_out"], st["x"].shape[1]))

  return [norm, in_proj, activate, gate, out_proj]


def _odd_prompt_gating(w, tile, d_gate):
  def gating(v):
    vb = v.astype(jnp.bfloat16)
    n_chunks = tile // CHUNK
    head = d_gate // N_SG_HEADS
    row = lax.broadcasted_iota(jnp.int32, (CHUNK, CHUNK), 0)
    col = lax.broadcasted_iota(jnp.int32, (CHUNK, CHUNK), 1)
    causal = row >= col
    per_head = []
    for hd in range(N_SG_HEADS):
      w_h = jnp.where(causal, w["w_s"][hd], 0.0).astype(jnp.bfloat16)
      rhs = jnp.concatenate(
          [vb[c * CHUNK:(c + 1) * CHUNK, hd * head:(hd + 1) * head]
           for c in range(n_chunks)], axis=1)
      o = jnp.dot(w_h, rhs, preferred_element_type=jnp.float32)
      per_head.append(o + w["b_s_t"][:, hd:hd + 1])
    return jnp.concatenate(
        [jnp.concatenate([o[:, c * head:(c + 1) * head] for o in per_head], axis=1)
         for c in range(n_chunks)], axis=0)
  return gating


def _odd_sample_gating(w, v_ref, d_gate):
  def gating(v):
    v_ref[:, 0, :] = v
    head = d_gate // N_SG_HEADS
    rows = v.shape[0]
    vb = v.astype(jnp.bfloat16).astype(jnp.float32)
    per_head = []
    for hd in range(N_SG_HEADS):
      w00 = w["w_s"][hd, 0:1, 0:1].astype(jnp.bfloat16).astype(jnp.float32)
      b0 = w["b_s_t"][0:1, hd:hd + 1]
      per_head.append(vb[:, hd * head:(hd + 1) * head] * jnp.broadcast_to(w00, (rows, head))
                      + jnp.broadcast_to(b0, (rows, head)))
    return jnp.concatenate(per_head, axis=1)
  return gating


def _odd_kernel(*refs, n_tiles, tile, d_gate, norm_row):
  x_ref, xs_ref = refs[:2]
  n_w = len(ODD_WEIGHT_NAMES)
  w = dict(zip(ODD_WEIGHT_NAMES, refs[2:2 + n_w]))
  y_ref, ys_ref, vs_ref, x1_buf = refs[2 + n_w:2 + n_w + 4]
  big_vmem = refs[2 + n_w + 4:-1]
  sems = refs[-1]
  copies = {name: pltpu.make_async_copy(w[name], buf, sems.at[i])
            for i, (name, buf) in enumerate(zip(ODD_BIG_WEIGHT_NAMES, big_vmem))}
  w.update(zip(ODD_BIG_WEIGHT_NAMES, big_vmem))
  w["g_mix"] = w["g_mix"].at[pl.ds(norm_row, 1)]
  w["g_ffn"] = w["g_ffn"].at[pl.ds(norm_row, 1)]

  s = pl.program_id(0)
  slot = s % 2

  @pl.when(s == 0)
  def _():
    for name in ODD_BIG_WEIGHT_NAMES:
      copies[name].start()
    for name in ODD_MIXER_WEIGHT_NAMES:
      copies[name].wait()

  def mix_stages():
    return _odd_mix_stages(lambda: x_ref[...], w, _set(x1_buf.at[slot]),
                           _odd_prompt_gating(w, tile, d_gate), d_gate)

  def ffn_stages():
    return _ffn_stages(lambda: x1_buf[1 - slot], w, _set(y_ref), True)

  _pipelined_steps(s, n_tiles, mix_stages, ffn_stages)

  @pl.when(s == 0)
  def _():
    for name in ODD_BIG_WEIGHT_NAMES:
      if name not in ODD_MIXER_WEIGHT_NAMES:
        copies[name].wait()

  @pl.when(s == n_tiles + 1)
  def _():
    st = {}
    _run(_odd_mix_stages(lambda: xs_ref[...], w, lambda v: st.update(x1=v),
                         _odd_sample_gating(w, vs_ref, d_gate), d_gate))
    _run(_ffn_stages(lambda: st["x1"], w, _set_rows(ys_ref), True))


def _resident(arr):
  zeros = (0,) * arr.ndim
  return pl.BlockSpec(arr.shape, lambda s: zeros, pipeline_mode=pl.Buffered(1))


def _whole_out(shape):
  zeros = (0,) * len(shape)
  return pl.BlockSpec(shape, lambda s: zeros)


def _params():
  return pltpu.CompilerParams(dimension_semantics=("arbitrary",),
                              vmem_limit_bytes=V7X_VMEM_LIMIT_BYTES)


def _prompt_specs(x, tile):
  batch, seq, d_model = x.shape
  tiles_per_seq = seq // tile
  n_tiles = batch * tiles_per_seq
  mix_tile = lambda s: jnp.clip(s, 0, n_tiles - 1)
  ffn_tile = lambda s: jnp.clip(s - 1, 0, n_tiles - 1)
  block = lambda which: pl.BlockSpec(
      (None, tile, d_model),
      lambda s: (which(s) // tiles_per_seq, which(s) % tiles_per_seq, 0))
  per_seq = lambda rows, ch: pl.BlockSpec(
      (None, rows, ch), lambda s: (mix_tile(s) // tiles_per_seq, 0, 0))
  return n_tiles, tiles_per_seq, mix_tile, block(mix_tile), block(ffn_tile), per_seq


def _cast_slab_spec(rows, cols, n_steps, step, layer):
  slab = rows // n_steps
  steps_per_slab = 1
  while slab % V7X_BF16_SUBLANES:
    slab *= 2
    steps_per_slab *= 2
  src = pl.BlockSpec((None, slab, cols), lambda s: (layer, step(s) // steps_per_slab, 0))
  dst = pl.BlockSpec((slab, cols + _skew_cols(cols)),
                     lambda s: (step(s) // steps_per_slab, 0))
  return src, dst


def _even_layer(x, xs, pool_hist, conv_hist, small_weights, norm_row, mats, mat_layer,
                next_mats, next_layer, tile, start):
  n_tiles, tiles_per_seq, step, tok_in, tok_out, per_seq = _prompt_specs(x, tile)
  batch = x.shape[0]
  rows, _, d_model = xs.shape
  d_pool = pool_hist.shape[-1]
  d_conv = conv_hist.shape[-1]
  cast_specs = [_cast_slab_spec(m.shape[1], m.shape[2], n_tiles, step, next_layer[i])
                for i, m in enumerate(next_mats)]
  f32 = x.dtype
  skewed = lambda m: (m.shape[1], m.shape[2] + _skew_cols(m.shape[2]))
  widest = max(m.shape[2] for m in mats)
  return pl.pallas_call(
      functools.partial(_even_kernel, n_tiles=n_tiles,
                        tiles_per_seq=tiles_per_seq, tile=tile, start=start,
                        d_pool=d_pool, d_conv=d_conv, layer=tuple(mat_layer),
                        norm_row=norm_row),
      grid=(n_tiles + 2,),
      in_specs=([tok_in, _resident(xs), pl.BlockSpec(memory_space=pl.ANY),
                 _resident(conv_hist)]
                + [_resident(w) for w in small_weights]
                + [pl.BlockSpec(memory_space=pl.ANY) for _ in mats]
                + [c[0] for c in cast_specs]),
      out_specs=([tok_out, per_seq(POOL_HIST, d_pool), per_seq(CONV_HIST, d_conv),
                  _whole_out((rows, d_model)), _whole_out((rows, d_pool)),
                  _whole_out(conv_hist.shape)] + [c[1] for c in cast_specs]
                 + [pl.BlockSpec(memory_space=pl.ANY)]),
      out_shape=([jax.ShapeDtypeStruct(x.shape, f32),
                  jax.ShapeDtypeStruct((batch, POOL_HIST, d_pool), f32),
                  jax.ShapeDtypeStruct((batch, CONV_HIST, d_conv), f32),
                  jax.ShapeDtypeStruct((rows, d_model), f32),
                  jax.ShapeDtypeStruct((rows, d_pool), f32),
                  jax.ShapeDtypeStruct(conv_hist.shape, f32)]
                 + [jax.ShapeDtypeStruct(skewed(m), jnp.bfloat16) for m in next_mats]
                 + [jax.ShapeDtypeStruct(pool_hist.shape, f32)]),
      scratch_shapes=([pltpu.VMEM((POOL_PAD, d_pool), jnp.float32),
                       pltpu.VMEM((CONV_PAD, d_conv), jnp.float32),
                       pltpu.VMEM((2, tile, d_model), jnp.float32)]
                      + [pltpu.VMEM(skewed(m), jnp.bfloat16) for m in mats]
                      + [pltpu.VMEM((FETCH_SLOTS, FETCH_ROWS, widest), jnp.float32),
                         pltpu.SemaphoreType.DMA((FETCH_SLOTS,)),
                         pltpu.SemaphoreType.DMA((2, POOL_HIST)),
                         pltpu.VMEM((len(POOL_WINDOWS) * POOL_GROUP // V7X_MXU_DIM,
                                     V7X_MXU_DIM, V7X_MXU_DIM), jnp.bfloat16)]),
      compiler_params=_params(),
      name="even_layer",
  )(x, xs, pool_hist, conv_hist, *small_weights, *mats, *next_mats)


def _odd_layer(x, xs, weights, norm_row, tile):
  n_tiles, _, _, tok_in, tok_out, _ = _prompt_specs(x, tile)
  rows, d_model = xs.shape
  d_gate = weights[ODD_WEIGHT_NAMES.index("g_v")].shape[-1]
  big = {name: w for name, w in zip(ODD_WEIGHT_NAMES, weights)
         if name in ODD_BIG_WEIGHT_NAMES}
  return pl.pallas_call(
      functools.partial(_odd_kernel, n_tiles=n_tiles, tile=tile, d_gate=d_gate,
                        norm_row=norm_row),
      grid=(n_tiles + 2,),
      in_specs=([tok_in, _resident(xs)]
                + [pl.BlockSpec(memory_space=pl.ANY) if name in ODD_BIG_WEIGHT_NAMES
                   else _resident(w) for name, w in zip(ODD_WEIGHT_NAMES, weights)]),
      out_specs=[tok_out, _whole_out((rows, 1, d_model)), _whole_out((rows, 1, d_gate))],
      out_shape=[jax.ShapeDtypeStruct(x.shape, x.dtype),
                 jax.ShapeDtypeStruct((rows, 1, d_model), x.dtype),
                 jax.ShapeDtypeStruct((rows, 1, d_gate), x.dtype)],
      scratch_shapes=([pltpu.VMEM((2, tile, d_model), jnp.float32)]
                      + [pltpu.VMEM(big[name].shape, big[name].dtype)
                         for name in ODD_BIG_WEIGHT_NAMES]
                      + [pltpu.SemaphoreType.DMA((len(ODD_BIG_WEIGHT_NAMES),))]),
      compiler_params=_params(),
      name="odd_layer",
  )(x, xs, *weights)


def kernel(x_prompt, x_sample, state_pool, state_conv, norm_mix, norm_ffn, norm_final, w_in_even, w_pool, pool_scale, conv_w, w_out_even, w_in_odd, norm_sg, w_s, b_s, w_out_odd, ffn_w_gate, ffn_w_up, ffn_w_down):
  depth = norm_mix.shape[0]
  assert depth == 2 and w_in_even.shape[0] == 1 and w_in_odd.shape[0] == 1
  assert x_sample.shape[1] == 1
  assert x_prompt.shape[1] % PROMPT_TILE == 0 and PROMPT_TILE % CHUNK == 0

  even_small = (norm_mix, w_pool[0], pool_scale, conv_w, norm_ffn)
  even_mats = (w_in_even, w_out_even, ffn_w_gate, ffn_w_up, ffn_w_down)
  even_mat_layer = (0, 0, 0, 0, 0)
  odd_mats = (w_in_odd, w_out_odd, ffn_w_gate, ffn_w_up, ffn_w_down)
  odd_mat_layer = (0, 0, 1, 1, 1)

  pool_hist = jnp.swapaxes(state_pool[0], 0, 1)
  conv_hist = state_conv[0]

  (x1_p, pool_p, conv_p, x1_s, _, conv_s, w_in_o, w_out_o, wg_o, wu_o, wd_o,
   pool_s) = _even_layer(x_prompt, x_sample, pool_hist, conv_hist, even_small, 0,
                         even_mats, even_mat_layer, odd_mats, odd_mat_layer,
                         PROMPT_TILE, PAST_LEN)

  odd_w = (norm_mix, w_in_o, norm_sg, w_s[0], b_s[0].T, w_out_o,
           norm_ffn, wg_o, wu_o, wd_o, norm_final.reshape(1, -1))
  y_p, y_s, v_s = _odd_layer(x1_p, x1_s, odd_w, 1, PROMPT_TILE)

  pool_s = jnp.swapaxes(pool_s, 0, 1)[None]

  return (y_p, y_s, pool_p[None], pool_s, conv_p[None], conv_s[None], v_s[None])
```

```python
import functools
import math

import jax
import jax.numpy as jnp
from jax import lax
from jax.experimental import pallas as pl
from jax.experimental.pallas import tpu as pltpu

POOL_WINDOWS = (2, 4, 8, 16)
POOL_GROUP = 128
POOL_HIST = max(POOL_WINDOWS) - 1
CONV_WIDTH = 3
CONV_HIST = CONV_WIDTH - 1
CHUNK = 128
N_SG_HEADS = 8
EPS = 1e-6
PAST_LEN = 16384

V7X_SUBLANES = 8
V7X_LANES = 128
V7X_BF16_SUBLANES = 16
V7X_MXU_DIM = 256
V7X_VMEM_LIMIT_BYTES = 60000 * 1024

POOL_PAD = -(-POOL_HIST // V7X_SUBLANES) * V7X_SUBLANES
CONV_PAD = -(-CONV_HIST // V7X_SUBLANES) * V7X_SUBLANES

PROMPT_TILE = 512
FETCH_ROWS = 128
FETCH_SLOTS = 4


def _bf16_dot(a, b):
  return jnp.dot(a.astype(jnp.bfloat16), b, preferred_element_type=jnp.float32)


def _rmsnorm(x, g):
  y = x * lax.rsqrt(jnp.mean(x * x, axis=-1, keepdims=True) + EPS)
  return y * g


def _gelu_exact(x):
  return 0.5 * x * (1.0 + lax.erf(x * math.sqrt(0.5)))


def _skew_cols(n):
  return V7X_LANES if (n // V7X_LANES) % V7X_SUBLANES == 0 else 0


def _skewed_dot(a, w_ref, n):
  return jnp.dot(a, w_ref[:, :n], preferred_element_type=jnp.float32)


def _pool_project(d, w_bd_ref, scale):
  halves = []
  for i in range(w_bd_ref.shape[0]):
    lo = i * V7X_MXU_DIM
    halves.append(_bf16_dot(d[:, lo:lo + V7X_MXU_DIM], w_bd_ref[i]))
  return jnp.concatenate(halves, axis=-1) * scale


def _ffn_stages(load_x1, w, emit, final_norm):
  st = {}

  def norm():
    st["x"] = load_x1()
    st["h"] = _rmsnorm(st["x"], w["g_ffn"][...]).astype(jnp.bfloat16)

  def gate_up():
    st["gate"] = jnp.dot(st["h"], w["wg"][...], preferred_element_type=jnp.float32)
    st["up"] = jnp.dot(st["h"], w["wu"][...], preferred_element_type=jnp.float32)

  def activate():
    gate = st["gate"]
    st["act"] = (gate * jax.nn.sigmoid(gate) * st["up"]).astype(jnp.bfloat16)

  def down():
    y = st["x"] + _skewed_dot(st["act"], w["wd"], st["x"].shape[1])
    emit(_rmsnorm(y, w["g_final"][...]) if final_norm else y)

  return [norm, gate_up, activate, down]


def _even_mix_stages(load_x, w, emit, mixers, d_pool, d_conv):
  st = {}

  def norm():
    st["x"] = load_x()
    st["h"] = _rmsnorm(st["x"], w["g_mix"][...]).astype(jnp.bfloat16)

  def in_proj():
    st["z"] = _skewed_dot(st["h"], w["w_in"], d_pool + 3 * d_conv)

  def mix():
    z = st["z"]
    p = z[:, :d_pool]
    xb = z[:, d_pool:d_pool + d_conv]
    bg = z[:, d_pool + d_conv:d_pool + 2 * d_conv]
    cg = z[:, d_pool + 2 * d_conv:]
    pooled, conv_y = mixers(p, cg * xb)
    st["pool_d"] = (pooled - p).astype(jnp.bfloat16)
    st["conv_out"] = (bg * conv_y).astype(jnp.bfloat16)

  def project():
    a_out = _pool_project(st["pool_d"], w["w_pool"], w["pool_scale"][...])
    st["mixed"] = jnp.concatenate([a_out.astype(jnp.bfloat16), st["conv_out"]], axis=-1)

  def out_proj():
    emit(st["x"] + _skewed_dot(st["mixed"], w["w_out"], st["x"].shape[1]))

  return [norm, in_proj, mix, project, out_proj]


def _rows_back(x, k):
  return pltpu.roll(x, k, axis=0)


def _even_prompt_mixers(t, p_hist, c_hist, conv_w, tile):
  assert all(w & (w - 1) == 0 and w <= POOL_PAD for w in POOL_WINDOWS)

  def mixers(p, c):
    pos = t * tile + lax.broadcasted_iota(jnp.int32, (POOL_PAD, POOL_GROUP), 0)
    pooled = []
    for g, win in enumerate(POOL_WINDOWS):
      lo = g * POOL_GROUP
      sums = jnp.concatenate([p_hist[:, lo:lo + POOL_GROUP], p[:, lo:lo + POOL_GROUP]],
                             axis=0)
      span = 1
      while span < win:
        sums = sums + _rows_back(sums, span)
        span *= 2
      sums = sums[POOL_PAD:]
      head = sums[:POOL_PAD] / jnp.minimum(pos + 1, win).astype(jnp.float32)
      pooled.append(jnp.concatenate([head, sums[POOL_PAD:] * (1.0 / win)], axis=0))
    p_hist[...] = p[tile - POOL_PAD:, :]
    c_rows = jnp.concatenate([c_hist[...], c], axis=0)
    y = _rows_back(c_rows, 2)[CONV_PAD:] * conv_w[0:1, :]
    y = y + _rows_back(c_rows, 1)[CONV_PAD:] * conv_w[1:2, :]
    y = y + c * conv_w[2:3, :]
    c_hist[...] = c[tile - CONV_PAD:, :]
    return jnp.concatenate(pooled, axis=-1), y
  return mixers


def _even_sample_mixers(pool_row, conv_hist_ref, conv_w, p_ref, conv_next_ref, start):
  def mixers(p, c):
    p_ref[...] = p
    for k in range(CONV_HIST - 1):
      conv_next_ref[:, k, :] = conv_hist_ref[:, k + 1, :]
    conv_next_ref[:, CONV_HIST - 1, :] = c
    pooled = []
    for g, win in enumerate(POOL_WINDOWS):
      lo = g * POOL_GROUP
      acc = p[:, lo:lo + POOL_GROUP]
      for k in range(1, win):
        acc = acc + pool_row(POOL_HIST - k)[:, lo:lo + POOL_GROUP]
      pooled.append(acc / float(min(start + 1, win)))
    y = conv_hist_ref[:, 0, :] * conv_w[0:1, :]
    y = y + conv_hist_ref[:, 1, :] * conv_w[1:2, :]
    y = y + c * conv_w[2:3, :]
    return jnp.concatenate(pooled, axis=-1), y
  return mixers


def _even_mix_head(t, p_hist, c_hist):
  @pl.when(t == 0)
  def _():
    p_hist[...] = jnp.zeros(p_hist.shape, jnp.float32)
    c_hist[...] = jnp.zeros(c_hist.shape, jnp.float32)


def _even_mix_tail(t, pool_state_ref, conv_state_ref, p_hist, c_hist, *, last_t):
  @pl.when(t == last_t)
  def _():
    pool_state_ref[...] = p_hist[POOL_PAD - POOL_HIST:, :]
    conv_state_ref[...] = c_hist[CONV_PAD - CONV_HIST:, :]


EVEN_SMALL_WEIGHT_NAMES = ("g_mix", "w_pool", "pool_scale", "conv_w", "g_ffn")
EVEN_BIG_WEIGHT_NAMES = ("w_in", "w_out", "wg", "wu", "wd")
ODD_WEIGHT_NAMES = ("g_mix", "w_in", "g_v", "w_s", "b_s", "w_out",
                    "g_ffn", "wg", "wu", "wd", "g_final")
ODD_MIXER_WEIGHT_NAMES = ("w_in", "w_out")
ODD_BIG_WEIGHT_NAMES = ODD_MIXER_WEIGHT_NAMES + ("wg", "wu", "wd")
N_CAST = 5


def _pool_block_diag(groups_ref, tiles_ref):
  per_tile = V7X_MXU_DIM // POOL_GROUP
  tiles_ref[...] = jnp.zeros(tiles_ref.shape, tiles_ref.dtype)
  for g in range(groups_ref.shape[0]):
    i, j = divmod(g, per_tile)
    lo = j * POOL_GROUP
    tiles_ref[i, lo:lo + POOL_GROUP, lo:lo + POOL_GROUP] = (
        groups_ref[g].astype(tiles_ref.dtype))


def _fetch_bf16(srcs, dsts, stage, sems):
  chunks = []
  for src, dst in zip(srcs, dsts):
    n_rows, n_cols = src.shape
    if dst.shape[1] > n_cols:
      dst[:, n_cols:] = jnp.zeros((n_rows, dst.shape[1] - n_cols), dst.dtype)
    for r0 in range(0, n_rows, FETCH_ROWS):
      chunks.append((src, dst, r0, min(FETCH_ROWS, n_rows - r0), n_cols))

  n_slots = stage.shape[0]

  def copy(i):
    src, _, r0, n, n_cols = chunks[i]
    return pltpu.make_async_copy(src.at[pl.ds(r0, n), :],
                                 stage.at[i % n_slots, pl.ds(0, n), pl.ds(0, n_cols)],
                                 sems.at[i % n_slots])

  for i in range(min(n_slots - 1, len(chunks))):
    copy(i).start()
  for i, (_, dst, r0, n, n_cols) in enumerate(chunks):
    if i + n_slots - 1 < len(chunks):
      copy(i + n_slots - 1).start()
    copy(i).wait()
    dst[pl.ds(r0, n), :n_cols] = stage[i % n_slots, :n, :n_cols].astype(dst.dtype)


def _run(stages):
  for stage in stages:
    stage()


def _pipelined_steps(s, n_tiles, mix_stages, ffn_stages):
  @pl.when(s == 0)
  def _():
    _run(mix_stages())

  @pl.when((s > 0) & (s < n_tiles))
  def _():
    mix, ffn = mix_stages(), ffn_stages()
    _run([ffn[0], ffn[1], mix[0], mix[1], mix[2], ffn[2], ffn[3], mix[3], mix[4]])

  @pl.when(s == n_tiles)
  def _():
    _run(ffn_stages())


def _set(ref):
  def emit(value):
    ref[...] = value
  return emit


def _set_rows(ref):
  def emit(value):
    ref[:, 0, :] = value
  return emit


def _even_kernel(*refs, n_tiles, tiles_per_seq, tile, start, d_pool, d_conv, layer,
                 norm_row):
  x_ref, xs_ref, pool_hist_hbm, conv_hist_ref = refs[:4]
  n_small, n_big = len(EVEN_SMALL_WEIGHT_NAMES), len(EVEN_BIG_WEIGHT_NAMES)
  w = dict(zip(EVEN_SMALL_WEIGHT_NAMES, refs[4:4 + n_small]))
  big_hbm = refs[4 + n_small:4 + n_small + n_big]
  cast_in = refs[4 + n_small + n_big:4 + n_small + n_big + N_CAST]
  outs = refs[4 + n_small + n_big + N_CAST:]
  y_ref, pool_state_ref, conv_state_ref, ys_ref, ps_ref, cs_ref = outs[:6]
  cast_out = outs[6:6 + N_CAST]
  pool_next_hbm = outs[6 + N_CAST]
  scratch = outs[7 + N_CAST:]
  p_hist, c_hist, x1_buf = scratch[:3]
  big_vmem = scratch[3:3 + n_big]
  stage, sems, hist_sems, pool_tiles = scratch[3 + n_big:]
  w.update(zip(EVEN_BIG_WEIGHT_NAMES, big_vmem))
  w["g_mix"] = w["g_mix"].at[pl.ds(norm_row, 1)]
  w["g_ffn"] = w["g_ffn"].at[pl.ds(norm_row, 1)]
  w["conv_w"] = w["conv_w"].at[0]
  pool_groups = w["w_pool"]
  w["w_pool"] = pool_tiles

  rows = xs_ref.shape[0]
  per_slot = stage.shape[2] // d_pool
  assert rows <= stage.shape[1] and POOL_HIST <= stage.shape[0] * per_slot

  def pool_row(k):
    return stage.at[k // per_slot, pl.ds(0, rows), pl.ds((k % per_slot) * d_pool, d_pool)]

  def row_in(k):
    return pltpu.make_async_copy(pool_hist_hbm.at[k], pool_row(k), hist_sems.at[0, k])

  def row_out(k):
    src = pool_row(k + 1) if k + 1 < POOL_HIST else ps_ref
    return pltpu.make_async_copy(src, pool_next_hbm.at[k], hist_sems.at[1, k])

  s = pl.program_id(0)
  t = s % tiles_per_seq
  slot = s % 2

  @pl.when(s == 0)
  def _():
    _pool_block_diag(pool_groups, pool_tiles)
    _fetch_bf16([m.at[layer[i]] for i, m in enumerate(big_hbm)], big_vmem, stage, sems)

  @pl.when(s < n_tiles)
  def _():
    _even_mix_head(t, p_hist, c_hist)
    for src, dst in zip(cast_in, cast_out):
      n = src.shape[1]
      dst[:, :n] = src[...].astype(dst.dtype)
      if dst.shape[1] > n:
        dst[:, n:] = jnp.zeros((dst.shape[0], dst.shape[1] - n), dst.dtype)

  def mix_stages():
    return _even_mix_stages(
        lambda: x_ref[...], w, _set(x1_buf.at[slot]),
        _even_prompt_mixers(t, p_hist, c_hist, w["conv_w"], tile), d_pool, d_conv)

  def ffn_stages():
    return _ffn_stages(lambda: x1_buf[1 - slot], w, _set(y_ref), False)

  @pl.when(s == n_tiles)
  def _():
    for k in range(POOL_HIST):
      row_in(k).start()

  _pipelined_steps(s, n_tiles, mix_stages, ffn_stages)

  @pl.when(s < n_tiles)
  def _():
    _even_mix_tail(t, pool_state_ref, conv_state_ref, p_hist, c_hist,
                   last_t=tiles_per_seq - 1)

  @pl.when(s == n_tiles + 1)
  def _():
    for k in range(POOL_HIST):
      row_in(k).wait()
    for k in range(POOL_HIST - 1):
      row_out(k).start()
    st = {}
    mixers = _even_sample_mixers(pool_row, conv_hist_ref, w["conv_w"],
                                 ps_ref, cs_ref, start)
    _run(_even_mix_stages(lambda: xs_ref[:, 0, :], w, lambda v: st.update(x1=v),
                          mixers, d_pool, d_conv))
    row_out(POOL_HIST - 1).start()
    _run(_ffn_stages(lambda: st["x1"], w, _set(ys_ref), False))
    for k in range(POOL_HIST):
      row_out(k).wait()


def _odd_mix_stages(load_x, w, emit, gating, d_gate):
  st = {}

  def norm():
    st["x"] = load_x()
    st["h"] = _rmsnorm(st["x"], w["g_mix"][...]).astype(jnp.bfloat16)

  def in_proj():
    st["z"] = _skewed_dot(st["h"], w["w_in"], 2 * d_gate)

  def activate():
    z = _gelu_exact(st["z"])
    st["u"] = z[:, :d_gate]
    st["v"] = _rmsnorm(z[:, d_gate:], w["g_v"][...])

  def gate():
    st["gated"] = (st["u"] * gating(st["v"])).astype(jnp.bfloat16)

  def out_proj():
    emit(st["x"] + _skewed_dot(st["gated"], w["w_out"], st["x"].shape[1]))

  return [norm, in_proj, activate, gate, out_proj]


def _odd_prompt_gating(w, tile, d_gate):
  def gating(v):
    vb = v.astype(jnp.bfloat16)
    n_chunks = tile // CHUNK
    head = d_gate // N_SG_HEADS
    row = lax.broadcasted_iota(jnp.int32, (CHUNK, CHUNK), 0)
    col = lax.broadcasted_iota(jnp.int32, (CHUNK, CHUNK), 1)
    causal = row >= col
    bias = w["b_s"][...].T
    per_head = []
    for hd in range(N_SG_HEADS):
      w_h = jnp.where(causal, w["w_s"][hd], 0.0).astype(jnp.bfloat16)
      rhs = jnp.concatenate(
          [vb[c * CHUNK:(c + 1) * CHUNK, hd * head:(hd + 1) * head]
           for c in range(n_chunks)], axis=1)
      o = jnp.dot(w_h, rhs, preferred_element_type=jnp.float32)
      per_head.append(o + bias[:, hd:hd + 1])
    return jnp.concatenate(
        [jnp.concatenate([o[:, c * head:(c + 1) * head] for o in per_head], axis=1)
         for c in range(n_chunks)], axis=0)
  return gating


def _odd_sample_gating(w, v_ref, d_gate):
  def gating(v):
    v_ref[:, 0, :] = v
    head = d_gate // N_SG_HEADS
    rows = v.shape[0]
    vb = v.astype(jnp.bfloat16).astype(jnp.float32)
    per_head = []
    for hd in range(N_SG_HEADS):
      w00 = w["w_s"][hd, 0:1, 0:1].astype(jnp.bfloat16).astype(jnp.float32)
      b0 = w["b_s"][hd:hd + 1, 0:1]
      per_head.append(vb[:, hd * head:(hd + 1) * head] * jnp.broadcast_to(w00, (rows, head))
                      + jnp.broadcast_to(b0, (rows, head)))
    return jnp.concatenate(per_head, axis=1)
  return gating


def _odd_kernel(*refs, n_tiles, tile, d_gate, norm_row):
  x_ref, xs_ref = refs[:2]
  n_w = len(ODD_WEIGHT_NAMES)
  w = dict(zip(ODD_WEIGHT_NAMES, refs[2:2 + n_w]))
  y_ref, ys_ref, vs_ref, x1_buf = refs[2 + n_w:2 + n_w + 4]
  big_vmem = refs[2 + n_w + 4:-1]
  sems = refs[-1]
  copies = {name: pltpu.make_async_copy(w[name], buf, sems.at[i])
            for i, (name, buf) in enumerate(zip(ODD_BIG_WEIGHT_NAMES, big_vmem))}
  w.update(zip(ODD_BIG_WEIGHT_NAMES, big_vmem))
  w["g_mix"] = w["g_mix"].at[pl.ds(norm_row, 1)]
  w["g_ffn"] = w["g_ffn"].at[pl.ds(norm_row, 1)]

  s = pl.program_id(0)
  slot = s % 2

  @pl.when(s == 0)
  def _():
    for name in ODD_BIG_WEIGHT_NAMES:
      copies[name].start()
    for name in ODD_MIXER_WEIGHT_NAMES:
      copies[name].wait()

  def mix_stages():
    return _odd_mix_stages(lambda: x_ref[...], w, _set(x1_buf.at[slot]),
                           _odd_prompt_gating(w, tile, d_gate), d_gate)

  def ffn_stages():
    return _ffn_stages(lambda: x1_buf[1 - slot], w, _set(y_ref), True)

  _pipelined_steps(s, n_tiles, mix_stages, ffn_stages)

  @pl.when(s == 0)
  def _():
    for name in ODD_BIG_WEIGHT_NAMES:
      if name not in ODD_MIXER_WEIGHT_NAMES:
        copies[name].wait()

  @pl.when(s == n_tiles + 1)
  def _():
    st = {}
    _run(_odd_mix_stages(lambda: xs_ref[...], w, lambda v: st.update(x1=v),
                         _odd_sample_gating(w, vs_ref, d_gate), d_gate))
    _run(_ffn_stages(lambda: st["x1"], w, _set_rows(ys_ref), True))


def _resident(arr):
  zeros = (0,) * arr.ndim
  return pl.BlockSpec(arr.shape, lambda s: zeros, pipeline_mode=pl.Buffered(1))


def _whole_out(shape):
  zeros = (0,) * len(shape)
  return pl.BlockSpec(shape, lambda s: zeros)


def _params():
  return pltpu.CompilerParams(dimension_semantics=("arbitrary",),
                              vmem_limit_bytes=V7X_VMEM_LIMIT_BYTES)


def _prompt_specs(x, tile):
  batch, seq, d_model = x.shape
  tiles_per_seq = seq // tile
  n_tiles = batch * tiles_per_seq
  mix_tile = lambda s: jnp.clip(s, 0, n_tiles - 1)
  ffn_tile = lambda s: jnp.clip(s - 1, 0, n_tiles - 1)
  block = lambda which: pl.BlockSpec(
      (None, tile, d_model),
      lambda s: (which(s) // tiles_per_seq, which(s) % tiles_per_seq, 0))
  per_seq = lambda rows, ch: pl.BlockSpec(
      (None, rows, ch), lambda s: (mix_tile(s) // tiles_per_seq, 0, 0))
  return n_tiles, tiles_per_seq, mix_tile, block(mix_tile), block(ffn_tile), per_seq


def _cast_slab_spec(rows, cols, n_steps, step, layer):
  slab = rows // n_steps
  steps_per_slab = 1
  while slab % V7X_BF16_SUBLANES:
    slab *= 2
    steps_per_slab *= 2
  src = pl.BlockSpec((None, slab, cols), lambda s: (layer, step(s) // steps_per_slab, 0))
  dst = pl.BlockSpec((slab, cols + _skew_cols(cols)),
                     lambda s: (step(s) // steps_per_slab, 0))
  return src, dst


def _even_layer(x, xs, pool_hist, conv_hist, small_weights, norm_row, mats, mat_layer,
                next_mats, next_layer, tile, start):
  n_tiles, tiles_per_seq, step, tok_in, tok_out, per_seq = _prompt_specs(x, tile)
  batch = x.shape[0]
  rows, _, d_model = xs.shape
  d_pool = pool_hist.shape[-1]
  d_conv = conv_hist.shape[-1]
  cast_specs = [_cast_slab_spec(m.shape[1], m.shape[2], n_tiles, step, next_layer[i])
                for i, m in enumerate(next_mats)]
  f32 = x.dtype
  skewed = lambda m: (m.shape[1], m.shape[2] + _skew_cols(m.shape[2]))
  widest = max(m.shape[2] for m in mats)
  return pl.pallas_call(
      functools.partial(_even_kernel, n_tiles=n_tiles,
                        tiles_per_seq=tiles_per_seq, tile=tile, start=start,
                        d_pool=d_pool, d_conv=d_conv, layer=tuple(mat_layer),
                        norm_row=norm_row),
      grid=(n_tiles + 2,),
      in_specs=([tok_in, _resident(xs), pl.BlockSpec(memory_space=pl.ANY),
                 _resident(conv_hist)]
                + [_resident(w) for w in small_weights]
                + [pl.BlockSpec(memory_space=pl.ANY) for _ in mats]
                + [c[0] for c in cast_specs]),
      out_specs=([tok_out, per_seq(POOL_HIST, d_pool), per_seq(CONV_HIST, d_conv),
                  _whole_out((rows, d_model)), _whole_out((rows, d_pool)),
                  _whole_out(conv_hist.shape)] + [c[1] for c in cast_specs]
                 + [pl.BlockSpec(memory_space=pl.ANY)]),
      out_shape=([jax.ShapeDtypeStruct(x.shape, f32),
                  jax.ShapeDtypeStruct((batch, POOL_HIST, d_pool), f32),
                  jax.ShapeDtypeStruct((batch, CONV_HIST, d_conv), f32),
                  jax.ShapeDtypeStruct((rows, d_model), f32),
                  jax.ShapeDtypeStruct((rows, d_pool), f32),
                  jax.ShapeDtypeStruct(conv_hist.shape, f32)]
                 + [jax.ShapeDtypeStruct(skewed(m), jnp.bfloat16) for m in next_mats]
                 + [jax.ShapeDtypeStruct(pool_hist.shape, f32)]),
      scratch_shapes=([pltpu.VMEM((POOL_PAD, d_pool), jnp.float32),
                       pltpu.VMEM((CONV_PAD, d_conv), jnp.float32),
                       pltpu.VMEM((2, tile, d_model), jnp.float32)]
                      + [pltpu.VMEM(skewed(m), jnp.bfloat16) for m in mats]
                      + [pltpu.VMEM((FETCH_SLOTS, FETCH_ROWS, widest), jnp.float32),
                         pltpu.SemaphoreType.DMA((FETCH_SLOTS,)),
                         pltpu.SemaphoreType.DMA((2, POOL_HIST)),
                         pltpu.VMEM((len(POOL_WINDOWS) * POOL_GROUP // V7X_MXU_DIM,
                                     V7X_MXU_DIM, V7X_MXU_DIM), jnp.bfloat16)]),
      compiler_params=_params(),
      name="even_layer",
  )(x, xs, pool_hist, conv_hist, *small_weights, *mats, *next_mats)


def _odd_layer(x, xs, weights, norm_row, tile):
  n_tiles, _, _, tok_in, tok_out, _ = _prompt_specs(x, tile)
  rows, d_model = xs.shape
  d_gate = weights[ODD_WEIGHT_NAMES.index("g_v")].shape[-1]
  big = {name: w for name, w in zip(ODD_WEIGHT_NAMES, weights)
         if name in ODD_BIG_WEIGHT_NAMES}
  return pl.pallas_call(
      functools.partial(_odd_kernel, n_tiles=n_tiles, tile=tile, d_gate=d_gate,
                        norm_row=norm_row),
      grid=(n_tiles + 2,),
      in_specs=([tok_in, _resident(xs)]
                + [pl.BlockSpec(memory_space=pl.ANY) if name in ODD_BIG_WEIGHT_NAMES
                   else _resident(w) for name, w in zip(ODD_WEIGHT_NAMES, weights)]),
      out_specs=[tok_out, _whole_out((rows, 1, d_model)), _whole_out((rows, 1, d_gate))],
      out_shape=[jax.ShapeDtypeStruct(x.shape, x.dtype),
                 jax.ShapeDtypeStruct((rows, 1, d_model), x.dtype),
                 jax.ShapeDtypeStruct((rows, 1, d_gate), x.dtype)],
      scratch_shapes=([pltpu.VMEM((2, tile, d_model), jnp.float32)]
                      + [pltpu.VMEM(big[name].shape, big[name].dtype)
                         for name in ODD_BIG_WEIGHT_NAMES]
                      + [pltpu.SemaphoreType.DMA((len(ODD_BIG_WEIGHT_NAMES),))]),
      compiler_params=_params(),
      name="odd_layer",
  )(x, xs, *weights)


def kernel(x_prompt, x_sample, state_pool, state_conv, norm_mix, norm_ffn, norm_final, w_in_even, w_pool, pool_scale, conv_w, w_out_even, w_in_odd, norm_sg, w_s, b_s, w_out_odd, ffn_w_gate, ffn_w_up, ffn_w_down):
  depth = norm_mix.shape[0]
  assert depth == 2 and w_in_even.shape[0] == 1 and w_in_odd.shape[0] == 1
  assert x_sample.shape[1] == 1
  assert x_prompt.shape[1] % PROMPT_TILE == 0 and PROMPT_TILE % CHUNK == 0

  even_small = (norm_mix, w_pool[0], pool_scale, conv_w, norm_ffn)
  even_mats = (w_in_even, w_out_even, ffn_w_gate, ffn_w_up, ffn_w_down)
  even_mat_layer = (0, 0, 0, 0, 0)
  odd_mats = (w_in_odd, w_out_odd, ffn_w_gate, ffn_w_up, ffn_w_down)
  odd_mat_layer = (0, 0, 1, 1, 1)

  pool_hist = jnp.swapaxes(state_pool[0], 0, 1)
  conv_hist = state_conv[0]

  (x1_p, pool_p, conv_p, x1_s, _, conv_s, w_in_o, w_out_o, wg_o, wu_o, wd_o,
   pool_s) = _even_layer(x_prompt, x_sample, pool_hist, conv_hist, even_small, 0,
                         even_mats, even_mat_layer, odd_mats, odd_mat_layer,
                         PROMPT_TILE, PAST_LEN)

  odd_w = (norm_mix, w_in_o, norm_sg, w_s[0], b_s[0], w_out_o,
           norm_ffn, wg_o, wu_o, wd_o, norm_final.reshape(1, -1))
  y_p, y_s, v_s = _odd_layer(x1_p, x1_s, odd_w, 1, PROMPT_TILE)

  pool_s = jnp.swapaxes(pool_s, 0, 1)[None]

  return (y_p, y_s, pool_p[None], pool_s, conv_p[None], conv_s[None], v_s[None])
```

```python
import functools
import math

import jax
import jax.numpy as jnp
from jax import lax
from jax.experimental import pallas as pl
from jax.experimental.pallas import tpu as pltpu

POOL_WINDOWS = (2, 4, 8, 16)
POOL_GROUP = 128
POOL_HIST = max(POOL_WINDOWS) - 1
CONV_WIDTH = 3
CONV_HIST = CONV_WIDTH - 1
CHUNK = 128
N_SG_HEADS = 8
EPS = 1e-6
PAST_LEN = 16384

V7X_SUBLANES = 8
V7X_LANES = 128
V7X_BF16_SUBLANES = 16
V7X_MXU_DIM = 256
V7X_VMEM_LIMIT_BYTES = 60000 * 1024

POOL_PAD = -(-POOL_HIST // V7X_SUBLANES) * V7X_SUBLANES
CONV_PAD = -(-CONV_HIST // V7X_SUBLANES) * V7X_SUBLANES

PROMPT_TILE = 512
FETCH_ROWS = 128
FETCH_SLOTS = 4


def _bf16_dot(a, b):
  return jnp.dot(a.astype(jnp.bfloat16), b, preferred_element_type=jnp.float32)


def _rmsnorm(x, g):
  y = x * lax.rsqrt(jnp.mean(x * x, axis=-1, keepdims=True) + EPS)
  return y * g


def _gelu_exact(x):
  return 0.5 * x * (1.0 + lax.erf(x * math.sqrt(0.5)))


def _skew_cols(n):
  return V7X_LANES if (n // V7X_LANES) % V7X_SUBLANES == 0 else 0


def _skewed_dot(a, w_ref, n):
  return jnp.dot(a, w_ref[:, :n], preferred_element_type=jnp.float32)


def _pool_project(d, w_bd_ref, scale):
  halves = []
  for i in range(w_bd_ref.shape[0]):
    lo = i * V7X_MXU_DIM
    halves.append(_bf16_dot(d[:, lo:lo + V7X_MXU_DIM], w_bd_ref[i]))
  return jnp.concatenate(halves, axis=-1) * scale


def _ffn_stages(load_x1, w, emit, final_norm):
  st = {}

  def norm():
    st["x"] = load_x1()
    st["h"] = _rmsnorm(st["x"], w["g_ffn"][...]).astype(jnp.bfloat16)

  def gate_up():
    st["gate"] = jnp.dot(st["h"], w["wg"][...], preferred_element_type=jnp.float32)
    st["up"] = jnp.dot(st["h"], w["wu"][...], preferred_element_type=jnp.float32)

  def activate():
    gate = st["gate"]
    st["act"] = (gate * jax.nn.sigmoid(gate) * st["up"]).astype(jnp.bfloat16)

  def down():
    y = st["x"] + _skewed_dot(st["act"], w["wd"], st["x"].shape[1])
    emit(_rmsnorm(y, w["g_final"][...]) if final_norm else y)

  return [norm, gate_up, activate, down]


def _even_mix_stages(load_x, w, emit, mixers, d_pool, d_conv):
  st = {}

  def norm():
    st["x"] = load_x()
    st["h"] = _rmsnorm(st["x"], w["g_mix"][...]).astype(jnp.bfloat16)

  def in_proj():
    st["z"] = _skewed_dot(st["h"], w["w_in"], d_pool + 3 * d_conv)

  def mix():
    z = st["z"]
    p = z[:, :d_pool]
    xb = z[:, d_pool:d_pool + d_conv]
    bg = z[:, d_pool + d_conv:d_pool + 2 * d_conv]
    cg = z[:, d_pool + 2 * d_conv:]
    pooled, conv_y = mixers(p, cg * xb)
    st["pool_d"] = (pooled - p).astype(jnp.bfloat16)
    st["conv_out"] = (bg * conv_y).astype(jnp.bfloat16)

  def project():
    a_out = _pool_project(st["pool_d"], w["w_pool"], w["pool_scale"][...])
    st["mixed"] = jnp.concatenate([a_out.astype(jnp.bfloat16), st["conv_out"]], axis=-1)

  def out_proj():
    emit(st["x"] + _skewed_dot(st["mixed"], w["w_out"], st["x"].shape[1]))

  return [norm, in_proj, mix, project, out_proj]


def _rows_back(x, k):
  return pltpu.roll(x, k, axis=0)


def _even_prompt_mixers(t, p_hist, c_hist, conv_w, tile):
  assert all(w & (w - 1) == 0 and w <= POOL_PAD for w in POOL_WINDOWS)

  def mixers(p, c):
    pos = t * tile + lax.broadcasted_iota(jnp.int32, (POOL_PAD, POOL_GROUP), 0)
    pooled = []
    for g, win in enumerate(POOL_WINDOWS):
      lo = g * POOL_GROUP
      sums = jnp.concatenate([p_hist[:, lo:lo + POOL_GROUP], p[:, lo:lo + POOL_GROUP]],
                             axis=0)
      span = 1
      while span < win:
        sums = sums + _rows_back(sums, span)
        span *= 2
      sums = sums[POOL_PAD:]
      head = sums[:POOL_PAD] / jnp.minimum(pos + 1, win).astype(jnp.float32)
      pooled.append(jnp.concatenate([head, sums[POOL_PAD:] * (1.0 / win)], axis=0))
    p_hist[...] = p[tile - POOL_PAD:, :]
    c_rows = jnp.concatenate([c_hist[...], c], axis=0)
    y = _rows_back(c_rows, 2)[CONV_PAD:] * conv_w[0:1, :]
    y = y + _rows_back(c_rows, 1)[CONV_PAD:] * conv_w[1:2, :]
    y = y + c * conv_w[2:3, :]
    c_hist[...] = c[tile - CONV_PAD:, :]
    return jnp.concatenate(pooled, axis=-1), y
  return mixers


def _even_sample_mixers(pool_row, conv_hist_ref, conv_w, p_ref, conv_next_ref, start):
  def mixers(p, c):
    p_ref[...] = p
    for k in range(CONV_HIST - 1):
      conv_next_ref[:, k, :] = conv_hist_ref[:, k + 1, :]
    conv_next_ref[:, CONV_HIST - 1, :] = c
    pooled = []
    for g, win in enumerate(POOL_WINDOWS):
      lo = g * POOL_GROUP
      acc = p[:, lo:lo + POOL_GROUP]
      for k in range(1, win):
        acc = acc + pool_row(POOL_HIST - k)[:, lo:lo + POOL_GROUP]
      pooled.append(acc / float(min(start + 1, win)))
    y = conv_hist_ref[:, 0, :] * conv_w[0:1, :]
    y = y + conv_hist_ref[:, 1, :] * conv_w[1:2, :]
    y = y + c * conv_w[2:3, :]
    return jnp.concatenate(pooled, axis=-1), y
  return mixers


def _even_mix_head(t, p_hist, c_hist):
  @pl.when(t == 0)
  def _():
    p_hist[...] = jnp.zeros(p_hist.shape, jnp.float32)
    c_hist[...] = jnp.zeros(c_hist.shape, jnp.float32)


def _even_mix_tail(t, pool_state_ref, conv_state_ref, p_hist, c_hist, *, last_t):
  @pl.when(t == last_t)
  def _():
    pool_state_ref[...] = p_hist[POOL_PAD - POOL_HIST:, :]
    conv_state_ref[...] = c_hist[CONV_PAD - CONV_HIST:, :]


EVEN_SMALL_WEIGHT_NAMES = ("g_mix", "w_pool", "pool_scale", "conv_w", "g_ffn")
EVEN_BIG_WEIGHT_NAMES = ("w_in", "w_out", "wg", "wu", "wd")
ODD_WEIGHT_NAMES = ("g_mix", "w_in", "g_v", "w_s", "b_s", "w_out",
                    "g_ffn", "wg", "wu", "wd", "g_final")
ODD_MIXER_WEIGHT_NAMES = ("w_in", "w_out")
ODD_BIG_WEIGHT_NAMES = ODD_MIXER_WEIGHT_NAMES + ("wg", "wu", "wd")
N_CAST = 5


def _pool_block_diag(groups_ref, tiles_ref):
  per_tile = V7X_MXU_DIM // POOL_GROUP
  tiles_ref[...] = jnp.zeros(tiles_ref.shape, tiles_ref.dtype)
  for g in range(groups_ref.shape[0]):
    i, j = divmod(g, per_tile)
    lo = j * POOL_GROUP
    tiles_ref[i, lo:lo + POOL_GROUP, lo:lo + POOL_GROUP] = (
        groups_ref[g].astype(tiles_ref.dtype))


def _fetch_bf16(srcs, dsts, stage, sems):
  chunks, first_chunk = [], []
  for src, dst in zip(srcs, dsts):
    first_chunk.append(len(chunks))
    for r0 in range(0, src.shape[0], FETCH_ROWS):
      chunks.append((src, dst, r0, min(FETCH_ROWS, src.shape[0] - r0), src.shape[1]))

  n_slots = stage.shape[0]

  def copy(i):
    src, _, r0, n, n_cols = chunks[i]
    return pltpu.make_async_copy(src.at[pl.ds(r0, n), :],
                                 stage.at[i % n_slots, pl.ds(0, n), pl.ds(0, n_cols)],
                                 sems.at[i % n_slots])

  def run(lo, hi):
    if lo == 0:
      for src, dst in zip(srcs, dsts):
        if dst.shape[1] > src.shape[1]:
          dst[:, src.shape[1]:] = jnp.zeros(
              (dst.shape[0], dst.shape[1] - src.shape[1]), dst.dtype)
      for i in range(min(n_slots - 1, len(chunks))):
        copy(i).start()
    for i in range(lo, hi):
      _, dst, r0, n, n_cols = chunks[i]
      if i + n_slots - 1 < len(chunks):
        copy(i + n_slots - 1).start()
      copy(i).wait()
      dst[pl.ds(r0, n), :n_cols] = stage[i % n_slots, :n, :n_cols].astype(dst.dtype)

  return run, first_chunk + [len(chunks)]


def _run(stages):
  for stage in stages:
    stage()


def _pipelined_steps(s, n_tiles, mix_stages, ffn_stages):
  @pl.when(s == 0)
  def _():
    _run(mix_stages())

  @pl.when((s > 0) & (s < n_tiles))
  def _():
    mix, ffn = mix_stages(), ffn_stages()
    _run([ffn[0], ffn[1], mix[0], mix[1], mix[2], ffn[2], ffn[3], mix[3], mix[4]])

  @pl.when(s == n_tiles)
  def _():
    _run(ffn_stages())


def _set(ref):
  def emit(value):
    ref[...] = value
  return emit


def _set_rows(ref):
  def emit(value):
    ref[:, 0, :] = value
  return emit


def _even_kernel(*refs, n_tiles, tiles_per_seq, tile, start, d_pool, d_conv, layer,
                 norm_row):
  x_ref, xs_ref, pool_hist_hbm, conv_hist_ref = refs[:4]
  n_small, n_big = len(EVEN_SMALL_WEIGHT_NAMES), len(EVEN_BIG_WEIGHT_NAMES)
  w = dict(zip(EVEN_SMALL_WEIGHT_NAMES, refs[4:4 + n_small]))
  big_hbm = refs[4 + n_small:4 + n_small + n_big]
  cast_in = refs[4 + n_small + n_big:4 + n_small + n_big + N_CAST]
  outs = refs[4 + n_small + n_big + N_CAST:]
  y_ref, pool_state_ref, conv_state_ref, ys_ref, ps_ref, cs_ref = outs[:6]
  cast_out = outs[6:6 + N_CAST]
  pool_next_hbm = outs[6 + N_CAST]
  scratch = outs[7 + N_CAST:]
  p_hist, c_hist, x1_buf = scratch[:3]
  big_vmem = scratch[3:3 + n_big]
  stage, sems, hist_sems, pool_tiles = scratch[3 + n_big:]
  w.update(zip(EVEN_BIG_WEIGHT_NAMES, big_vmem))
  w["g_mix"] = w["g_mix"].at[pl.ds(norm_row, 1)]
  w["g_ffn"] = w["g_ffn"].at[pl.ds(norm_row, 1)]
  w["conv_w"] = w["conv_w"].at[0]
  pool_groups = w["w_pool"]
  w["w_pool"] = pool_tiles

  rows = xs_ref.shape[0]
  per_slot = stage.shape[2] // d_pool
  assert rows <= stage.shape[1] and POOL_HIST <= stage.shape[0] * per_slot

  def pool_row(k):
    return stage.at[k // per_slot, pl.ds(0, rows), pl.ds((k % per_slot) * d_pool, d_pool)]

  def row_in(k):
    return pltpu.make_async_copy(pool_hist_hbm.at[k], pool_row(k), hist_sems.at[0, k])

  def row_out(k):
    src = pool_row(k + 1) if k + 1 < POOL_HIST else ps_ref
    return pltpu.make_async_copy(src, pool_next_hbm.at[k], hist_sems.at[1, k])

  s = pl.program_id(0)
  t = s % tiles_per_seq
  slot = s % 2

  fetch, first_chunk = _fetch_bf16([m.at[layer[i]] for i, m in enumerate(big_hbm)],
                                   big_vmem, stage, sems)
  first_ffn_chunk = first_chunk[EVEN_BIG_WEIGHT_NAMES.index("wg")]

  @pl.when(s == 0)
  def _():
    _pool_block_diag(pool_groups, pool_tiles)
    fetch(0, first_ffn_chunk)

  @pl.when(s < n_tiles)
  def _():
    _even_mix_head(t, p_hist, c_hist)
    for src, dst in zip(cast_in, cast_out):
      n = src.shape[1]
      dst[:, :n] = src[...].astype(dst.dtype)
      if dst.shape[1] > n:
        dst[:, n:] = jnp.zeros((dst.shape[0], dst.shape[1] - n), dst.dtype)

  def mix_stages():
    return _even_mix_stages(
        lambda: x_ref[...], w, _set(x1_buf.at[slot]),
        _even_prompt_mixers(t, p_hist, c_hist, w["conv_w"], tile), d_pool, d_conv)

  def ffn_stages():
    return _ffn_stages(lambda: x1_buf[1 - slot], w, _set(y_ref), False)

  @pl.when(s == n_tiles)
  def _():
    for k in range(POOL_HIST):
      row_in(k).start()

  _pipelined_steps(s, n_tiles, mix_stages, ffn_stages)

  @pl.when(s == 0)
  def _():
    fetch(first_ffn_chunk, first_chunk[-1])

  @pl.when(s < n_tiles)
  def _():
    _even_mix_tail(t, pool_state_ref, conv_state_ref, p_hist, c_hist,
                   last_t=tiles_per_seq - 1)

  @pl.when(s == n_tiles + 1)
  def _():
    for k in range(POOL_HIST):
      row_in(k).wait()
    for k in range(POOL_HIST - 1):
      row_out(k).start()
    st = {}
    mixers = _even_sample_mixers(pool_row, conv_hist_ref, w["conv_w"],
                                 ps_ref, cs_ref, start)
    _run(_even_mix_stages(lambda: xs_ref[:, 0, :], w, lambda v: st.update(x1=v),
                          mixers, d_pool, d_conv))
    row_out(POOL_HIST - 1).start()
    _run(_ffn_stages(lambda: st["x1"], w, _set(ys_ref), False))
    for k in range(POOL_HIST):
      row_out(k).wait()


def _odd_mix_stages(load_x, w, emit, gating, d_gate):
  st = {}

  def norm():
    st["x"] = load_x()
    st["h"] = _rmsnorm(st["x"], w["g_mix"][...]).astype(jnp.bfloat16)

  def in_proj():
    st["z"] = _skewed_dot(st["h"], w["w_in"], 2 * d_gate)

  def activate():
    z = _gelu_exact(st["z"])
    st["u"] = z[:, :d_gate]
    st["v"] = _rmsnorm(z[:, d_gate:], w["g_v"][...])

  def gate():
    st["gated"] = (st["u"] * gating(st["v"])).astype(jnp.bfloat16)

  def out_proj():
    emit(st["x"] + _skewed_dot(st["gated"], w["w_out"], st["x"].shape[1]))

  return [norm, in_proj, activate, gate, out_proj]


def _odd_prompt_gating(w, tile, d_gate):
  def gating(v):
    vb = v.astype(jnp.bfloat16)
    n_chunks = tile // CHUNK
    head = d_gate // N_SG_HEADS
    row = lax.broadcasted_iota(jnp.int32, (CHUNK, CHUNK), 0)
    col = lax.broadcasted_iota(jnp.int32, (CHUNK, CHUNK), 1)
    causal = row >= col
    bias = w["b_s"][...].T
    per_head = []
    for hd in range(N_SG_HEADS):
      w_h = jnp.where(causal, w["w_s"][hd], 0.0).astype(jnp.bfloat16)
      rhs = jnp.concatenate(
          [vb[c * CHUNK:(c + 1) * CHUNK, hd * head:(hd + 1) * head]
           for c in range(n_chunks)], axis=1)
      o = jnp.dot(w_h, rhs, preferred_element_type=jnp.float32)
      per_head.append(o + bias[:, hd:hd + 1])
    return jnp.concatenate(
        [jnp.concatenate([o[:, c * head:(c + 1) * head] for o in per_head], axis=1)
         for c in range(n_chunks)], axis=0)
  return gating


def _odd_sample_gating(w, v_ref, d_gate):
  def gating(v):
    v_ref[:, 0, :] = v
    head = d_gate // N_SG_HEADS
    rows = v.shape[0]
    vb = v.astype(jnp.bfloat16).astype(jnp.float32)
    per_head = []
    for hd in range(N_SG_HEADS):
      w00 = w["w_s"][hd, 0:1, 0:1].astype(jnp.bfloat16).astype(jnp.float32)
      b0 = w["b_s"][hd:hd + 1, 0:1]
      per_head.append(vb[:, hd * head:(hd + 1) * head] * jnp.broadcast_to(w00, (rows, head))
                      + jnp.broadcast_to(b0, (rows, head)))
    return jnp.concatenate(per_head, axis=1)
  return gating


def _odd_kernel(*refs, n_tiles, tile, d_gate, norm_row):
  x_ref, xs_ref = refs[:2]
  n_w = len(ODD_WEIGHT_NAMES)
  w = dict(zip(ODD_WEIGHT_NAMES, refs[2:2 + n_w]))
  y_ref, ys_ref, vs_ref, x1_buf = refs[2 + n_w:2 + n_w + 4]
  big_vmem = refs[2 + n_w + 4:-1]
  sems = refs[-1]
  copies = {name: pltpu.make_async_copy(w[name], buf, sems.at[i])
            for i, (name, buf) in enumerate(zip(ODD_BIG_WEIGHT_NAMES, big_vmem))}
  w.update(zip(ODD_BIG_WEIGHT_NAMES, big_vmem))
  w["g_mix"] = w["g_mix"].at[pl.ds(norm_row, 1)]
  w["g_ffn"] = w["g_ffn"].at[pl.ds(norm_row, 1)]

  s = pl.program_id(0)
  slot = s % 2

  @pl.when(s == 0)
  def _():
    for name in ODD_BIG_WEIGHT_NAMES:
      copies[name].start()
    for name in ODD_MIXER_WEIGHT_NAMES:
      copies[name].wait()

  def mix_stages():
    return _odd_mix_stages(lambda: x_ref[...], w, _set(x1_buf.at[slot]),
                           _odd_prompt_gating(w, tile, d_gate), d_gate)

  def ffn_stages():
    return _ffn_stages(lambda: x1_buf[1 - slot], w, _set(y_ref), True)

  _pipelined_steps(s, n_tiles, mix_stages, ffn_stages)

  @pl.when(s == 0)
  def _():
    for name in ODD_BIG_WEIGHT_NAMES:
      if name not in ODD_MIXER_WEIGHT_NAMES:
        copies[name].wait()

  @pl.when(s == n_tiles + 1)
  def _():
    st = {}
    _run(_odd_mix_stages(lambda: xs_ref[...], w, lambda v: st.update(x1=v),
                         _odd_sample_gating(w, vs_ref, d_gate), d_gate))
    _run(_ffn_stages(lambda: st["x1"], w, _set_rows(ys_ref), True))


def _resident(arr):
  zeros = (0,) * arr.ndim
  return pl.BlockSpec(arr.shape, lambda s: zeros, pipeline_mode=pl.Buffered(1))


def _whole_out(shape):
  zeros = (0,) * len(shape)
  return pl.BlockSpec(shape, lambda s: zeros)


def _params():
  return pltpu.CompilerParams(dimension_semantics=("arbitrary",),
                              vmem_limit_bytes=V7X_VMEM_LIMIT_BYTES)


def _prompt_specs(x, tile):
  batch, seq, d_model = x.shape
  tiles_per_seq = seq // tile
  n_tiles = batch * tiles_per_seq
  mix_tile = lambda s: jnp.clip(s, 0, n_tiles - 1)
  ffn_tile = lambda s: jnp.clip(s - 1, 0, n_tiles - 1)
  block = lambda which: pl.BlockSpec(
      (None, tile, d_model),
      lambda s: (which(s) // tiles_per_seq, which(s) % tiles_per_seq, 0))
  per_seq = lambda rows, ch: pl.BlockSpec(
      (None, rows, ch), lambda s: (mix_tile(s) // tiles_per_seq, 0, 0))
  return n_tiles, tiles_per_seq, mix_tile, block(mix_tile), block(ffn_tile), per_seq


def _cast_slab_spec(rows, cols, n_steps, step, layer):
  slab = rows // n_steps
  steps_per_slab = 1
  while slab % V7X_BF16_SUBLANES:
    slab *= 2
    steps_per_slab *= 2
  src = pl.BlockSpec((None, slab, cols), lambda s: (layer, step(s) // steps_per_slab, 0))
  dst = pl.BlockSpec((slab, cols + _skew_cols(cols)),
                     lambda s: (step(s) // steps_per_slab, 0))
  return src, dst


def _even_layer(x, xs, pool_hist, conv_hist, small_weights, norm_row, mats, mat_layer,
                next_mats, next_layer, tile, start):
  n_tiles, tiles_per_seq, step, tok_in, tok_out, per_seq = _prompt_specs(x, tile)
  batch = x.shape[0]
  rows, _, d_model = xs.shape
  d_pool = pool_hist.shape[-1]
  d_conv = conv_hist.shape[-1]
  cast_specs = [_cast_slab_spec(m.shape[1], m.shape[2], n_tiles, step, next_layer[i])
                for i, m in enumerate(next_mats)]
  f32 = x.dtype
  skewed = lambda m: (m.shape[1], m.shape[2] + _skew_cols(m.shape[2]))
  widest = max(m.shape[2] for m in mats)
  return pl.pallas_call(
      functools.partial(_even_kernel, n_tiles=n_tiles,
                        tiles_per_seq=tiles_per_seq, tile=tile, start=start,
                        d_pool=d_pool, d_conv=d_conv, layer=tuple(mat_layer),
                        norm_row=norm_row),
      grid=(n_tiles + 2,),
      in_specs=([tok_in, _resident(xs), pl.BlockSpec(memory_space=pl.ANY),
                 _resident(conv_hist)]
                + [_resident(w) for w in small_weights]
                + [pl.BlockSpec(memory_space=pl.ANY) for _ in mats]
                + [c[0] for c in cast_specs]),
      out_specs=([tok_out, per_seq(POOL_HIST, d_pool), per_seq(CONV_HIST, d_conv),
                  _whole_out((rows, d_model)), _whole_out((rows, d_pool)),
                  _whole_out(conv_hist.shape)] + [c[1] for c in cast_specs]
                 + [pl.BlockSpec(memory_space=pl.ANY)]),
      out_shape=([jax.ShapeDtypeStruct(x.shape, f32),
                  jax.ShapeDtypeStruct((batch, POOL_HIST, d_pool), f32),
                  jax.ShapeDtypeStruct((batch, CONV_HIST, d_conv), f32),
                  jax.ShapeDtypeStruct((rows, d_model), f32),
                  jax.ShapeDtypeStruct((rows, d_pool), f32),
                  jax.ShapeDtypeStruct(conv_hist.shape, f32)]
                 + [jax.ShapeDtypeStruct(skewed(m), jnp.bfloat16) for m in next_mats]
                 + [jax.ShapeDtypeStruct(pool_hist.shape, f32)]),
      scratch_shapes=([pltpu.VMEM((POOL_PAD, d_pool), jnp.float32),
                       pltpu.VMEM((CONV_PAD, d_conv), jnp.float32),
                       pltpu.VMEM((2, tile, d_model), jnp.float32)]
                      + [pltpu.VMEM(skewed(m), jnp.bfloat16) for m in mats]
                      + [pltpu.VMEM((FETCH_SLOTS, FETCH_ROWS, widest), jnp.float32),
                         pltpu.SemaphoreType.DMA((FETCH_SLOTS,)),
                         pltpu.SemaphoreType.DMA((2, POOL_HIST)),
                         pltpu.VMEM((len(POOL_WINDOWS) * POOL_GROUP // V7X_MXU_DIM,
                                     V7X_MXU_DIM, V7X_MXU_DIM), jnp.bfloat16)]),
      compiler_params=_params(),
      name="even_layer",
  )(x, xs, pool_hist, conv_hist, *small_weights, *mats, *next_mats)


def _odd_layer(x, xs, weights, norm_row, tile):
  n_tiles, _, _, tok_in, tok_out, _ = _prompt_specs(x, tile)
  rows, d_model = xs.shape
  d_gate = weights[ODD_WEIGHT_NAMES.index("g_v")].shape[-1]
  big = {name: w for name, w in zip(ODD_WEIGHT_NAMES, weights)
         if name in ODD_BIG_WEIGHT_NAMES}
  return pl.pallas_call(
      functools.partial(_odd_kernel, n_tiles=n_tiles, tile=tile, d_gate=d_gate,
                        norm_row=norm_row),
      grid=(n_tiles + 2,),
      in_specs=([tok_in, _resident(xs)]
                + [pl.BlockSpec(memory_space=pl.ANY) if name in ODD_BIG_WEIGHT_NAMES
                   else _resident(w) for name, w in zip(ODD_WEIGHT_NAMES, weights)]),
      out_specs=[tok_out, _whole_out((rows, 1, d_model)), _whole_out((rows, 1, d_gate))],
      out_shape=[jax.ShapeDtypeStruct(x.shape, x.dtype),
                 jax.ShapeDtypeStruct((rows, 1, d_model), x.dtype),
                 jax.ShapeDtypeStruct((rows, 1, d_gate), x.dtype)],
      scratch_shapes=([pltpu.VMEM((2, tile, d_model), jnp.float32)]
                      + [pltpu.VMEM(big[name].shape, big[name].dtype)
                         for name in ODD_BIG_WEIGHT_NAMES]
                      + [pltpu.SemaphoreType.DMA((len(ODD_BIG_WEIGHT_NAMES),))]),
      compiler_params=_params(),
      name="odd_layer",
  )(x, xs, *weights)


def kernel(x_prompt, x_sample, state_pool, state_conv, norm_mix, norm_ffn, norm_final, w_in_even, w_pool, pool_scale, conv_w, w_out_even, w_in_odd, norm_sg, w_s, b_s, w_out_odd, ffn_w_gate, ffn_w_up, ffn_w_down):
  depth = norm_mix.shape[0]
  assert depth == 2 and w_in_even.shape[0] == 1 and w_in_odd.shape[0] == 1
  assert x_sample.shape[1] == 1
  assert x_prompt.shape[1] % PROMPT_TILE == 0 and PROMPT_TILE % CHUNK == 0

  even_small = (norm_mix, w_pool[0], pool_scale, conv_w, norm_ffn)
  even_mats = (w_in_even, w_out_even, ffn_w_gate, ffn_w_up, ffn_w_down)
  even_mat_layer = (0, 0, 0, 0, 0)
  odd_mats = (w_in_odd, w_out_odd, ffn_w_gate, ffn_w_up, ffn_w_down)
  odd_mat_layer = (0, 0, 1, 1, 1)

  pool_hist = jnp.swapaxes(state_pool[0], 0, 1)
  conv_hist = state_conv[0]

  (x1_p, pool_p, conv_p, x1_s, _, conv_s, w_in_o, w_out_o, wg_o, wu_o, wd_o,
   pool_s) = _even_layer(x_prompt, x_sample, pool_hist, conv_hist, even_small, 0,
                         even_mats, even_mat_layer, odd_mats, odd_mat_layer,
                         PROMPT_TILE, PAST_LEN)

  odd_w = (norm_mix, w_in_o, norm_sg, w_s[0], b_s[0], w_out_o,
           norm_ffn, wg_o, wu_o, wd_o, norm_final.reshape(1, -1))
  y_p, y_s, v_s = _odd_layer(x1_p, x1_s, odd_w, 1, PROMPT_TILE)

  pool_s = jnp.swapaxes(pool_s, 0, 1)[None]

  return (y_p, y_s, pool_p[None], pool_s, conv_p[None], conv_s[None], v_s[None])
```

```python
import functools
import math

import jax
import jax.numpy as jnp
from jax import lax
from jax.experimental import pallas as pl
from jax.experimental.pallas import tpu as pltpu

POOL_WINDOWS = (2, 4, 8, 16)
POOL_GROUP = 128
POOL_HIST = max(POOL_WINDOWS) - 1
CONV_WIDTH = 3
CONV_HIST = CONV_WIDTH - 1
CHUNK = 128
N_SG_HEADS = 8
EPS = 1e-6
PAST_LEN = 16384

V7X_SUBLANES = 8
V7X_LANES = 128
V7X_BF16_SUBLANES = 16
V7X_MXU_DIM = 256
V7X_VMEM_LIMIT_BYTES = 60000 * 1024

POOL_PAD = -(-POOL_HIST // V7X_SUBLANES) * V7X_SUBLANES
CONV_PAD = -(-CONV_HIST // V7X_SUBLANES) * V7X_SUBLANES

PROMPT_TILE = 512
FETCH_ROWS = 128
FETCH_SLOTS = 4


def _bf16_dot(a, b):
  return jnp.dot(a.astype(jnp.bfloat16), b, preferred_element_type=jnp.float32)


def _rmsnorm(x, g):
  y = x * lax.rsqrt(jnp.mean(x * x, axis=-1, keepdims=True) + EPS)
  return y * g


def _gelu_exact(x):
  return 0.5 * x * (1.0 + lax.erf(x * math.sqrt(0.5)))


def _skew_cols(n):
  return V7X_LANES if (n // V7X_LANES) % V7X_SUBLANES == 0 else 0


def _skewed_dot(a, w_ref, n):
  return jnp.dot(a, w_ref[:, :n], preferred_element_type=jnp.float32)


def _pool_project(d, w_bd_ref, scale):
  halves = []
  for i in range(w_bd_ref.shape[0]):
    lo = i * V7X_MXU_DIM
    halves.append(_bf16_dot(d[:, lo:lo + V7X_MXU_DIM], w_bd_ref[i]))
  return jnp.concatenate(halves, axis=-1) * scale


def _ffn_stages(load_x1, w, emit, final_norm):
  st = {}

  def norm():
    st["x"] = load_x1()
    st["h"] = _rmsnorm(st["x"], w["g_ffn"][...]).astype(jnp.bfloat16)

  def gate_up():
    st["gate"] = jnp.dot(st["h"], w["wg"][...], preferred_element_type=jnp.float32)
    st["up"] = jnp.dot(st["h"], w["wu"][...], preferred_element_type=jnp.float32)

  def activate():
    gate = st["gate"]
    st["act"] = (gate * jax.nn.sigmoid(gate) * st["up"]).astype(jnp.bfloat16)

  def down():
    y = st["x"] + _skewed_dot(st["act"], w["wd"], st["x"].shape[1])
    emit(_rmsnorm(y, w["g_final"][...]) if final_norm else y)

  return [norm, gate_up, activate, down]


def _even_mix_stages(load_x, w, emit, mixers, d_pool, d_conv):
  st = {}

  def norm():
    st["x"] = load_x()
    st["h"] = _rmsnorm(st["x"], w["g_mix"][...]).astype(jnp.bfloat16)

  def in_proj():
    st["z"] = _skewed_dot(st["h"], w["w_in"], d_pool + 3 * d_conv)

  def mix():
    z = st["z"]
    p = z[:, :d_pool]
    xb = z[:, d_pool:d_pool + d_conv]
    bg = z[:, d_pool + d_conv:d_pool + 2 * d_conv]
    cg = z[:, d_pool + 2 * d_conv:]
    pooled, conv_y = mixers(p, cg * xb)
    st["pool_d"] = (pooled - p).astype(jnp.bfloat16)
    st["conv_out"] = (bg * conv_y).astype(jnp.bfloat16)

  def project():
    a_out = _pool_project(st["pool_d"], w["w_pool"], w["pool_scale"][...])
    st["mixed"] = jnp.concatenate([a_out.astype(jnp.bfloat16), st["conv_out"]], axis=-1)

  def out_proj():
    emit(st["x"] + _skewed_dot(st["mixed"], w["w_out"], st["x"].shape[1]))

  return [norm, in_proj, mix, project, out_proj]


def _rows_back(x, k):
  return pltpu.roll(x, k, axis=0)


def _even_prompt_mixers(t, p_hist, c_hist, conv_w, tile):
  assert all(w & (w - 1) == 0 and w <= POOL_PAD for w in POOL_WINDOWS)

  def mixers(p, c):
    pos = t * tile + lax.broadcasted_iota(jnp.int32, (POOL_PAD, POOL_GROUP), 0)
    pooled = []
    for g, win in enumerate(POOL_WINDOWS):
      lo = g * POOL_GROUP
      sums = jnp.concatenate([p_hist[:, lo:lo + POOL_GROUP], p[:, lo:lo + POOL_GROUP]],
                             axis=0)
      span = 1
      while span < win:
        sums = sums + _rows_back(sums, span)
        span *= 2
      sums = sums[POOL_PAD:]
      head = sums[:POOL_PAD] / jnp.minimum(pos + 1, win).astype(jnp.float32)
      pooled.append(jnp.concatenate([head, sums[POOL_PAD:] * (1.0 / win)], axis=0))
    p_hist[...] = p[tile - POOL_PAD:, :]
    c_rows = jnp.concatenate([c_hist[...], c], axis=0)
    y = _rows_back(c_rows, 2)[CONV_PAD:] * conv_w[0:1, :]
    y = y + _rows_back(c_rows, 1)[CONV_PAD:] * conv_w[1:2, :]
    y = y + c * conv_w[2:3, :]
    c_hist[...] = c[tile - CONV_PAD:, :]
    return jnp.concatenate(pooled, axis=-1), y
  return mixers


def _even_sample_mixers(pool_row, conv_hist_ref, conv_w, p_ref, conv_next_ref, start):
  def mixers(p, c):
    p_ref[...] = p
    for k in range(CONV_HIST - 1):
      conv_next_ref[:, k, :] = conv_hist_ref[:, k + 1, :]
    conv_next_ref[:, CONV_HIST - 1, :] = c
    pooled = []
    for g, win in enumerate(POOL_WINDOWS):
      lo = g * POOL_GROUP
      acc = p[:, lo:lo + POOL_GROUP]
      for k in range(1, win):
        acc = acc + pool_row(POOL_HIST - k)[:, lo:lo + POOL_GROUP]
      pooled.append(acc / float(min(start + 1, win)))
    y = conv_hist_ref[:, 0, :] * conv_w[0:1, :]
    y = y + conv_hist_ref[:, 1, :] * conv_w[1:2, :]
    y = y + c * conv_w[2:3, :]
    return jnp.concatenate(pooled, axis=-1), y
  return mixers


def _even_mix_head(t, p_hist, c_hist):
  @pl.when(t == 0)
  def _():
    p_hist[...] = jnp.zeros(p_hist.shape, jnp.float32)
    c_hist[...] = jnp.zeros(c_hist.shape, jnp.float32)


def _even_mix_tail(t, pool_state_ref, conv_state_ref, p_hist, c_hist, *, last_t):
  @pl.when(t == last_t)
  def _():
    pool_state_ref[...] = p_hist[POOL_PAD - POOL_HIST:, :]
    conv_state_ref[...] = c_hist[CONV_PAD - CONV_HIST:, :]


EVEN_SMALL_WEIGHT_NAMES = ("g_mix", "w_pool", "pool_scale", "conv_w", "g_ffn")
EVEN_BIG_WEIGHT_NAMES = ("w_in", "w_out", "wg", "wu", "wd")
ODD_WEIGHT_NAMES = ("g_mix", "w_in", "g_v", "w_s", "b_s", "w_out",
                    "g_ffn", "wg", "wu", "wd", "g_final")
ODD_MIXER_WEIGHT_NAMES = ("w_in", "w_out")
ODD_BIG_WEIGHT_NAMES = ODD_MIXER_WEIGHT_NAMES + ("wg", "wu", "wd")
N_CAST = 5


def _pool_block_diag(groups_ref, tiles_ref):
  per_tile = V7X_MXU_DIM // POOL_GROUP
  tiles_ref[...] = jnp.zeros(tiles_ref.shape, tiles_ref.dtype)
  for g in range(groups_ref.shape[0]):
    i, j = divmod(g, per_tile)
    lo = j * POOL_GROUP
    tiles_ref[i, lo:lo + POOL_GROUP, lo:lo + POOL_GROUP] = (
        groups_ref[g].astype(tiles_ref.dtype))


def _fetch_bf16(srcs, dsts, stage, sems):
  chunks = []
  for src, dst in zip(srcs, dsts):
    n_rows, n_cols = src.shape
    if dst.shape[1] > n_cols:
      dst[:, n_cols:] = jnp.zeros((n_rows, dst.shape[1] - n_cols), dst.dtype)
    for r0 in range(0, n_rows, FETCH_ROWS):
      chunks.append((src, dst, r0, min(FETCH_ROWS, n_rows - r0), n_cols))

  n_slots = stage.shape[0]

  def copy(i):
    src, _, r0, n, n_cols = chunks[i]
    return pltpu.make_async_copy(src.at[pl.ds(r0, n), :],
                                 stage.at[i % n_slots, pl.ds(0, n), pl.ds(0, n_cols)],
                                 sems.at[i % n_slots])

  for i in range(min(n_slots - 1, len(chunks))):
    copy(i).start()
  for i, (_, dst, r0, n, n_cols) in enumerate(chunks):
    if i + n_slots - 1 < len(chunks):
      copy(i + n_slots - 1).start()
    copy(i).wait()
    dst[pl.ds(r0, n), :n_cols] = stage[i % n_slots, :n, :n_cols].astype(dst.dtype)


def _run(stages):
  for stage in stages:
    stage()


def _pipelined_steps(s, n_tiles, mix_stages, ffn_stages):
  @pl.when(s == 0)
  def _():
    _run(mix_stages(0))

  for slot in (0, 1):
    @pl.when((s > 0) & (s < n_tiles) & (s % 2 == slot))
    def _():
      mix, ffn = mix_stages(slot), ffn_stages(1 - slot)
      _run([ffn[0], ffn[1], mix[0], mix[1], mix[2], ffn[2], ffn[3], mix[3], mix[4]])

  @pl.when(s == n_tiles)
  def _():
    _run(ffn_stages((n_tiles - 1) % 2))


def _set(ref):
  def emit(value):
    ref[...] = value
  return emit


def _set_rows(ref):
  def emit(value):
    ref[:, 0, :] = value
  return emit


def _even_kernel(*refs, n_tiles, tiles_per_seq, tile, start, d_pool, d_conv, layer,
                 norm_row):
  x_ref, xs_ref, pool_hist_hbm, conv_hist_ref = refs[:4]
  n_small, n_big = len(EVEN_SMALL_WEIGHT_NAMES), len(EVEN_BIG_WEIGHT_NAMES)
  w = dict(zip(EVEN_SMALL_WEIGHT_NAMES, refs[4:4 + n_small]))
  big_hbm = refs[4 + n_small:4 + n_small + n_big]
  cast_in = refs[4 + n_small + n_big:4 + n_small + n_big + N_CAST]
  outs = refs[4 + n_small + n_big + N_CAST:]
  y_ref, pool_state_ref, conv_state_ref, ys_ref, ps_ref, cs_ref = outs[:6]
  cast_out = outs[6:6 + N_CAST]
  pool_next_hbm = outs[6 + N_CAST]
  scratch = outs[7 + N_CAST:]
  p_hist, c_hist, x1_buf = scratch[:3]
  big_vmem = scratch[3:3 + n_big]
  stage, sems, hist_sems, pool_tiles = scratch[3 + n_big:]
  w.update(zip(EVEN_BIG_WEIGHT_NAMES, big_vmem))
  w["g_mix"] = w["g_mix"].at[pl.ds(norm_row, 1)]
  w["g_ffn"] = w["g_ffn"].at[pl.ds(norm_row, 1)]
  w["conv_w"] = w["conv_w"].at[0]
  pool_groups = w["w_pool"]
  w["w_pool"] = pool_tiles

  rows = xs_ref.shape[0]
  per_slot = stage.shape[2] // d_pool
  assert rows <= stage.shape[1] and POOL_HIST <= stage.shape[0] * per_slot

  def pool_row(k):
    return stage.at[k // per_slot, pl.ds(0, rows), pl.ds((k % per_slot) * d_pool, d_pool)]

  def row_in(k):
    return pltpu.make_async_copy(pool_hist_hbm.at[k], pool_row(k), hist_sems.at[0, k])

  def row_out(k):
    src = pool_row(k + 1) if k + 1 < POOL_HIST else ps_ref
    return pltpu.make_async_copy(src, pool_next_hbm.at[k], hist_sems.at[1, k])

  s = pl.program_id(0)
  t = s % tiles_per_seq

  @pl.when(s == 0)
  def _():
    _pool_block_diag(pool_groups, pool_tiles)
    _fetch_bf16([m.at[layer[i]] for i, m in enumerate(big_hbm)], big_vmem, stage, sems)

  @pl.when(s < n_tiles)
  def _():
    _even_mix_head(t, p_hist, c_hist)
    for src, dst in zip(cast_in, cast_out):
      n = src.shape[1]
      dst[:, :n] = src[...].astype(dst.dtype)
      if dst.shape[1] > n:
        dst[:, n:] = jnp.zeros((dst.shape[0], dst.shape[1] - n), dst.dtype)

  def mix_stages(slot):
    return _even_mix_stages(
        lambda: x_ref[...], w, _set(x1_buf.at[slot]),
        _even_prompt_mixers(t, p_hist, c_hist, w["conv_w"], tile), d_pool, d_conv)

  def ffn_stages(slot):
    return _ffn_stages(lambda: x1_buf[slot], w, _set(y_ref), False)

  @pl.when(s == n_tiles)
  def _():
    for k in range(POOL_HIST):
      row_in(k).start()

  _pipelined_steps(s, n_tiles, mix_stages, ffn_stages)

  @pl.when(s < n_tiles)
  def _():
    _even_mix_tail(t, pool_state_ref, conv_state_ref, p_hist, c_hist,
                   last_t=tiles_per_seq - 1)

  @pl.when(s == n_tiles + 1)
  def _():
    for k in range(POOL_HIST):
      row_in(k).wait()
    for k in range(POOL_HIST - 1):
      row_out(k).start()
    st = {}
    mixers = _even_sample_mixers(pool_row, conv_hist_ref, w["conv_w"],
                                 ps_ref, cs_ref, start)
    _run(_even_mix_stages(lambda: xs_ref[:, 0, :], w, lambda v: st.update(x1=v),
                          mixers, d_pool, d_conv))
    row_out(POOL_HIST - 1).start()
    _run(_ffn_stages(lambda: st["x1"], w, _set(ys_ref), False))
    for k in range(POOL_HIST):
      row_out(k).wait()


def _odd_mix_stages(load_x, w, emit, gating, d_gate):
  st = {}

  def norm():
    st["x"] = load_x()
    st["h"] = _rmsnorm(st["x"], w["g_mix"][...]).astype(jnp.bfloat16)

  def in_proj():
    st["z"] = _skewed_dot(st["h"], w["w_in"], 2 * d_gate)

  def activate():
    z = _gelu_exact(st["z"])
    st["u"] = z[:, :d_gate]
    st["v"] = _rmsnorm(z[:, d_gate:], w["g_v"][...])

  def gate():
    st["gated"] = (st["u"] * gating(st["v"])).astype(jnp.bfloat16)

  def out_proj():
    emit(st["x"] + _skewed_dot(st["gated"], w["w_out"], st["x"].shape[1]))

  return [norm, in_proj, activate, gate, out_proj]


def _odd_prompt_gating(w, tile, d_gate):
  def gating(v):
    vb = v.astype(jnp.bfloat16)
    n_chunks = tile // CHUNK
    head = d_gate // N_SG_HEADS
    row = lax.broadcasted_iota(jnp.int32, (CHUNK, CHUNK), 0)
    col = lax.broadcasted_iota(jnp.int32, (CHUNK, CHUNK), 1)
    causal = row >= col
    bias = w["b_s"][...].T
    per_head = []
    for hd in range(N_SG_HEADS):
      w_h = jnp.where(causal, w["w_s"][hd], 0.0).astype(jnp.bfloat16)
      rhs = jnp.concatenate(
          [vb[c * CHUNK:(c + 1) * CHUNK, hd * head:(hd + 1) * head]
           for c in range(n_chunks)], axis=1)
      o = jnp.dot(w_h, rhs, preferred_element_type=jnp.float32)
      per_head.append(o + bias[:, hd:hd + 1])
    return jnp.concatenate(
        [jnp.concatenate([o[:, c * head:(c + 1) * head] for o in per_head], axis=1)
         for c in range(n_chunks)], axis=0)
  return gating


def _odd_sample_gating(w, v_ref, d_gate):
  def gating(v):
    v_ref[:, 0, :] = v
    head = d_gate // N_SG_HEADS
    rows = v.shape[0]
    vb = v.astype(jnp.bfloat16).astype(jnp.float32)
    per_head = []
    for hd in range(N_SG_HEADS):
      w00 = w["w_s"][hd, 0:1, 0:1].astype(jnp.bfloat16).astype(jnp.float32)
      b0 = w["b_s"][hd:hd + 1, 0:1]
      per_head.append(vb[:, hd * head:(hd + 1) * head] * jnp.broadcast_to(w00, (rows, head))
                      + jnp.broadcast_to(b0, (rows, head)))
    return jnp.concatenate(per_head, axis=1)
  return gating


def _odd_kernel(*refs, n_tiles, tile, d_gate, norm_row):
  x_ref, xs_ref = refs[:2]
  n_w = len(ODD_WEIGHT_NAMES)
  w = dict(zip(ODD_WEIGHT_NAMES, refs[2:2 + n_w]))
  y_ref, ys_ref, vs_ref, x1_buf = refs[2 + n_w:2 + n_w + 4]
  big_vmem = refs[2 + n_w + 4:-1]
  sems = refs[-1]
  copies = {name: pltpu.make_async_copy(w[name], buf, sems.at[i])
            for i, (name, buf) in enumerate(zip(ODD_BIG_WEIGHT_NAMES, big_vmem))}
  w.update(zip(ODD_BIG_WEIGHT_NAMES, big_vmem))
  w["g_mix"] = w["g_mix"].at[pl.ds(norm_row, 1)]
  w["g_ffn"] = w["g_ffn"].at[pl.ds(norm_row, 1)]

  s = pl.program_id(0)

  @pl.when(s == 0)
  def _():
    for name in ODD_BIG_WEIGHT_NAMES:
      copies[name].start()
    for name in ODD_MIXER_WEIGHT_NAMES:
      copies[name].wait()

  def mix_stages(slot):
    return _odd_mix_stages(lambda: x_ref[...], w, _set(x1_buf.at[slot]),
                           _odd_prompt_gating(w, tile, d_gate), d_gate)

  def ffn_stages(slot):
    return _ffn_stages(lambda: x1_buf[slot], w, _set(y_ref), True)

  _pipelined_steps(s, n_tiles, mix_stages, ffn_stages)

  @pl.when(s == 0)
  def _():
    for name in ODD_BIG_WEIGHT_NAMES:
      if name not in ODD_MIXER_WEIGHT_NAMES:
        copies[name].wait()

  @pl.when(s == n_tiles + 1)
  def _():
    st = {}
    _run(_odd_mix_stages(lambda: xs_ref[...], w, lambda v: st.update(x1=v),
                         _odd_sample_gating(w, vs_ref, d_gate), d_gate))
    _run(_ffn_stages(lambda: st["x1"], w, _set_rows(ys_ref), True))


def _resident(arr):
  zeros = (0,) * arr.ndim
  return pl.BlockSpec(arr.shape, lambda s: zeros, pipeline_mode=pl.Buffered(1))


def _whole_out(shape):
  zeros = (0,) * len(shape)
  return pl.BlockSpec(shape, lambda s: zeros)


def _params():
  return pltpu.CompilerParams(dimension_semantics=("arbitrary",),
                              vmem_limit_bytes=V7X_VMEM_LIMIT_BYTES)


def _prompt_specs(x, tile):
  batch, seq, d_model = x.shape
  tiles_per_seq = seq // tile
  n_tiles = batch * tiles_per_seq
  mix_tile = lambda s: jnp.clip(s, 0, n_tiles - 1)
  ffn_tile = lambda s: jnp.clip(s - 1, 0, n_tiles - 1)
  block = lambda which: pl.BlockSpec(
      (None, tile, d_model),
      lambda s: (which(s) // tiles_per_seq, which(s) % tiles_per_seq, 0))
  per_seq = lambda rows, ch: pl.BlockSpec(
      (None, rows, ch), lambda s: (mix_tile(s) // tiles_per_seq, 0, 0))
  return n_tiles, tiles_per_seq, mix_tile, block(mix_tile), block(ffn_tile), per_seq


def _cast_slab_spec(rows, cols, n_steps, step, layer):
  slab = rows // n_steps
  steps_per_slab = 1
  while slab % V7X_BF16_SUBLANES:
    slab *= 2
    steps_per_slab *= 2
  src = pl.BlockSpec((None, slab, cols), lambda s: (layer, step(s) // steps_per_slab, 0))
  dst = pl.BlockSpec((slab, cols + _skew_cols(cols)),
                     lambda s: (step(s) // steps_per_slab, 0))
  return src, dst


def _even_layer(x, xs, pool_hist, conv_hist, small_weights, norm_row, mats, mat_layer,
                next_mats, next_layer, tile, start):
  n_tiles, tiles_per_seq, step, tok_in, tok_out, per_seq = _prompt_specs(x, tile)
  batch = x.shape[0]
  rows, _, d_model = xs.shape
  d_pool = pool_hist.shape[-1]
  d_conv = conv_hist.shape[-1]
  cast_specs = [_cast_slab_spec(m.shape[1], m.shape[2], n_tiles, step, next_layer[i])
                for i, m in enumerate(next_mats)]
  f32 = x.dtype
  skewed = lambda m: (m.shape[1], m.shape[2] + _skew_cols(m.shape[2]))
  widest = max(m.shape[2] for m in mats)
  return pl.pallas_call(
      functools.partial(_even_kernel, n_tiles=n_tiles,
                        tiles_per_seq=tiles_per_seq, tile=tile, start=start,
                        d_pool=d_pool, d_conv=d_conv, layer=tuple(mat_layer),
                        norm_row=norm_row),
      grid=(n_tiles + 2,),
      in_specs=([tok_in, _resident(xs), pl.BlockSpec(memory_space=pl.ANY),
                 _resident(conv_hist)]
                + [_resident(w) for w in small_weights]
                + [pl.BlockSpec(memory_space=pl.ANY) for _ in mats]
                + [c[0] for c in cast_specs]),
      out_specs=([tok_out, per_seq(POOL_HIST, d_pool), per_seq(CONV_HIST, d_conv),
                  _whole_out((rows, d_model)), _whole_out((rows, d_pool)),
                  _whole_out(conv_hist.shape)] + [c[1] for c in cast_specs]
                 + [pl.BlockSpec(memory_space=pl.ANY)]),
      out_shape=([jax.ShapeDtypeStruct(x.shape, f32),
                  jax.ShapeDtypeStruct((batch, POOL_HIST, d_pool), f32),
                  jax.ShapeDtypeStruct((batch, CONV_HIST, d_conv), f32),
                  jax.ShapeDtypeStruct((rows, d_model), f32),
                  jax.ShapeDtypeStruct((rows, d_pool), f32),
                  jax.ShapeDtypeStruct(conv_hist.shape, f32)]
                 + [jax.ShapeDtypeStruct(skewed(m), jnp.bfloat16) for m in next_mats]
                 + [jax.ShapeDtypeStruct(pool_hist.shape, f32)]),
      scratch_shapes=([pltpu.VMEM((POOL_PAD, d_pool), jnp.float32),
                       pltpu.VMEM((CONV_PAD, d_conv), jnp.float32),
                       pltpu.VMEM((2, tile, d_model), jnp.float32)]
                      + [pltpu.VMEM(skewed(m), jnp.bfloat16) for m in mats]
                      + [pltpu.VMEM((FETCH_SLOTS, FETCH_ROWS, widest), jnp.float32),
                         pltpu.SemaphoreType.DMA((FETCH_SLOTS,)),
                         pltpu.SemaphoreType.DMA((2, POOL_HIST)),
                         pltpu.VMEM((len(POOL_WINDOWS) * POOL_GROUP // V7X_MXU_DIM,
                                     V7X_MXU_DIM, V7X_MXU_DIM), jnp.bfloat16)]),
      compiler_params=_params(),
      name="even_layer",
  )(x, xs, pool_hist, conv_hist, *small_weights, *mats, *next_mats)


def _odd_layer(x, xs, weights, norm_row, tile):
  n_tiles, _, _, tok_in, tok_out, _ = _prompt_specs(x, tile)
  rows, d_model = xs.shape
  d_gate = weights[ODD_WEIGHT_NAMES.index("g_v")].shape[-1]
  big = {name: w for name, w in zip(ODD_WEIGHT_NAMES, weights)
         if name in ODD_BIG_WEIGHT_NAMES}
  return pl.pallas_call(
      functools.partial(_odd_kernel, n_tiles=n_tiles, tile=tile, d_gate=d_gate,
                        norm_row=norm_row),
      grid=(n_tiles + 2,),
      in_specs=([tok_in, _resident(xs)]
                + [pl.BlockSpec(memory_space=pl.ANY) if name in ODD_BIG_WEIGHT_NAMES
                   else _resident(w) for name, w in zip(ODD_WEIGHT_NAMES, weights)]),
      out_specs=[tok_out, _whole_out((rows, 1, d_model)), _whole_out((rows, 1, d_gate))],
      out_shape=[jax.ShapeDtypeStruct(x.shape, x.dtype),
                 jax.ShapeDtypeStruct((rows, 1, d_model), x.dtype),
                 jax.ShapeDtypeStruct((rows, 1, d_gate), x.dtype)],
      scratch_shapes=([pltpu.VMEM((2, tile, d_model), jnp.float32)]
                      + [pltpu.VMEM(big[name].shape, big[name].dtype)
                         for name in ODD_BIG_WEIGHT_NAMES]
                      + [pltpu.SemaphoreType.DMA((len(ODD_BIG_WEIGHT_NAMES),))]),
      compiler_params=_params(),
      name="odd_layer",
  )(x, xs, *weights)


def kernel(x_prompt, x_sample, state_pool, state_conv, norm_mix, norm_ffn, norm_final, w_in_even, w_pool, pool_scale, conv_w, w_out_even, w_in_odd, norm_sg, w_s, b_s, w_out_odd, ffn_w_gate, ffn_w_up, ffn_w_down):
  depth = norm_mix.shape[0]
  assert depth == 2 and w_in_even.shape[0] == 1 and w_in_odd.shape[0] == 1
  assert x_sample.shape[1] == 1
  assert x_prompt.shape[1] % PROMPT_TILE == 0 and PROMPT_TILE % CHUNK == 0

  even_small = (norm_mix, w_pool[0], pool_scale, conv_w, norm_ffn)
  even_mats = (w_in_even, w_out_even, ffn_w_gate, ffn_w_up, ffn_w_down)
  even_mat_layer = (0, 0, 0, 0, 0)
  odd_mats = (w_in_odd, w_out_odd, ffn_w_gate, ffn_w_up, ffn_w_down)
  odd_mat_layer = (0, 0, 1, 1, 1)

  pool_hist = jnp.swapaxes(state_pool[0], 0, 1)
  conv_hist = state_conv[0]

  (x1_p, pool_p, conv_p, x1_s, _, conv_s, w_in_o, w_out_o, wg_o, wu_o, wd_o,
   pool_s) = _even_layer(x_prompt, x_sample, pool_hist, conv_hist, even_small, 0,
                         even_mats, even_mat_layer, odd_mats, odd_mat_layer,
                         PROMPT_TILE, PAST_LEN)

  odd_w = (norm_mix, w_in_o, norm_sg, w_s[0], b_s[0], w_out_o,
           norm_ffn, wg_o, wu_o, wd_o, norm_final.reshape(1, -1))
  y_p, y_s, v_s = _odd_layer(x1_p, x1_s, odd_w, 1, PROMPT_TILE)

  pool_s = jnp.swapaxes(pool_s, 0, 1)[None]

  return (y_p, y_s, pool_p[None], pool_s, conv_p[None], conv_s[None], v_s[None])
```

```python
import functools
import math

import jax
import jax.numpy as jnp
from jax import lax
from jax.experimental import pallas as pl
from jax.experimental.pallas import tpu as pltpu

POOL_WINDOWS = (2, 4, 8, 16)
POOL_GROUP = 128
POOL_HIST = max(POOL_WINDOWS) - 1
CONV_WIDTH = 3
CONV_HIST = CONV_WIDTH - 1
CHUNK = 128
N_SG_HEADS = 8
EPS = 1e-6
PAST_LEN = 16384

V7X_SUBLANES = 8
V7X_LANES = 128
V7X_BF16_SUBLANES = 16
V7X_MXU_DIM = 256
V7X_VMEM_LIMIT_BYTES = 60000 * 1024

POOL_PAD = -(-POOL_HIST // V7X_SUBLANES) * V7X_SUBLANES
CONV_PAD = -(-CONV_HIST // V7X_SUBLANES) * V7X_SUBLANES

PROMPT_TILE = 512
FETCH_ROWS = 128
FETCH_SLOTS = 4


def _bf16_dot(a, b):
  return jnp.dot(a.astype(jnp.bfloat16), b, preferred_element_type=jnp.float32)


def _rmsnorm(x, g):
  y = x * lax.rsqrt(jnp.mean(x * x, axis=-1, keepdims=True) + EPS)
  return y * g


def _gelu_exact(x):
  return 0.5 * x * (1.0 + lax.erf(x * math.sqrt(0.5)))


def _skew_cols(n):
  return V7X_LANES if (n // V7X_LANES) % V7X_SUBLANES == 0 else 0


def _skewed_dot(a, w_ref, n):
  return jnp.dot(a, w_ref[:, :n], preferred_element_type=jnp.float32)


def _pool_project(d, w_bd_ref, scale):
  halves = []
  for i in range(w_bd_ref.shape[0]):
    lo = i * V7X_MXU_DIM
    halves.append(_bf16_dot(d[:, lo:lo + V7X_MXU_DIM], w_bd_ref[i]))
  return jnp.concatenate(halves, axis=-1) * scale


def _ffn_stages(load_x1, w, emit, final_norm):
  st = {}

  def norm():
    st["x"] = load_x1()
    st["h"] = _rmsnorm(st["x"], w["g_ffn"][...]).astype(jnp.bfloat16)

  def gate_up():
    st["gate"] = jnp.dot(st["h"], w["wg"][...], preferred_element_type=jnp.float32)
    st["up"] = jnp.dot(st["h"], w["wu"][...], preferred_element_type=jnp.float32)

  def activate():
    gate = st["gate"].astype(jnp.bfloat16)
    st["act"] = gate * jax.nn.sigmoid(gate) * st["up"].astype(jnp.bfloat16)

  def down():
    y = st["x"] + _skewed_dot(st["act"], w["wd"], st["x"].shape[1])
    emit(_rmsnorm(y, w["g_final"][...]) if final_norm else y)

  return [norm, gate_up, activate, down]


def _even_mix_stages(load_x, w, emit, mixers, d_pool, d_conv):
  st = {}

  def norm():
    st["x"] = load_x()
    st["h"] = _rmsnorm(st["x"], w["g_mix"][...]).astype(jnp.bfloat16)

  def in_proj():
    st["z"] = _skewed_dot(st["h"], w["w_in"], d_pool + 3 * d_conv)

  def mix():
    z = st["z"]
    p = z[:, :d_pool]
    xb = z[:, d_pool:d_pool + d_conv]
    bg = z[:, d_pool + d_conv:d_pool + 2 * d_conv]
    cg = z[:, d_pool + 2 * d_conv:]
    pooled, conv_y = mixers(p, cg * xb)
    st["pool_d"] = (pooled - p).astype(jnp.bfloat16)
    st["conv_out"] = (bg * conv_y).astype(jnp.bfloat16)

  def project():
    a_out = _pool_project(st["pool_d"], w["w_pool"], w["pool_scale"][...])
    st["mixed"] = jnp.concatenate([a_out.astype(jnp.bfloat16), st["conv_out"]], axis=-1)

  def out_proj():
    emit(st["x"] + _skewed_dot(st["mixed"], w["w_out"], st["x"].shape[1]))

  return [norm, in_proj, mix, project, out_proj]


def _rows_back(x, k):
  return pltpu.roll(x, k, axis=0)


def _even_prompt_mixers(t, p_hist, c_hist, conv_w, tile):
  assert all(w & (w - 1) == 0 and w <= POOL_PAD for w in POOL_WINDOWS)

  def mixers(p, c):
    pos = t * tile + lax.broadcasted_iota(jnp.int32, (POOL_PAD, POOL_GROUP), 0)
    pooled = []
    for g, win in enumerate(POOL_WINDOWS):
      lo = g * POOL_GROUP
      sums = jnp.concatenate([p_hist[:, lo:lo + POOL_GROUP], p[:, lo:lo + POOL_GROUP]],
                             axis=0)
      span = 1
      while span < win:
        sums = sums + _rows_back(sums, span)
        span *= 2
      sums = sums[POOL_PAD:]
      head = sums[:POOL_PAD] / jnp.minimum(pos + 1, win).astype(jnp.float32)
      pooled.append(jnp.concatenate([head, sums[POOL_PAD:] * (1.0 / win)], axis=0))
    p_hist[...] = p[tile - POOL_PAD:, :]
    c_rows = jnp.concatenate([c_hist[...], c], axis=0)
    y = _rows_back(c_rows, 2)[CONV_PAD:] * conv_w[0:1, :]
    y = y + _rows_back(c_rows, 1)[CONV_PAD:] * conv_w[1:2, :]
    y = y + c * conv_w[2:3, :]
    c_hist[...] = c[tile - CONV_PAD:, :]
    return jnp.concatenate(pooled, axis=-1), y
  return mixers


def _even_sample_mixers(pool_row, conv_hist_ref, conv_w, p_ref, conv_next_ref, start):
  def mixers(p, c):
    p_ref[...] = p
    for k in range(CONV_HIST - 1):
      conv_next_ref[:, k, :] = conv_hist_ref[:, k + 1, :]
    conv_next_ref[:, CONV_HIST - 1, :] = c
    pooled = []
    for g, win in enumerate(POOL_WINDOWS):
      lo = g * POOL_GROUP
      acc = p[:, lo:lo + POOL_GROUP]
      for k in range(1, win):
        acc = acc + pool_row(POOL_HIST - k)[:, lo:lo + POOL_GROUP]
      pooled.append(acc / float(min(start + 1, win)))
    y = conv_hist_ref[:, 0, :] * conv_w[0:1, :]
    y = y + conv_hist_ref[:, 1, :] * conv_w[1:2, :]
    y = y + c * conv_w[2:3, :]
    return jnp.concatenate(pooled, axis=-1), y
  return mixers


def _even_mix_head(t, p_hist, c_hist):
  @pl.when(t == 0)
  def _():
    p_hist[...] = jnp.zeros(p_hist.shape, jnp.float32)
    c_hist[...] = jnp.zeros(c_hist.shape, jnp.float32)


def _even_mix_tail(t, pool_state_ref, conv_state_ref, p_hist, c_hist, *, last_t):
  @pl.when(t == last_t)
  def _():
    pool_state_ref[...] = p_hist[POOL_PAD - POOL_HIST:, :]
    conv_state_ref[...] = c_hist[CONV_PAD - CONV_HIST:, :]


EVEN_SMALL_WEIGHT_NAMES = ("g_mix", "w_pool", "pool_scale", "conv_w", "g_ffn")
EVEN_BIG_WEIGHT_NAMES = ("w_in", "w_out", "wg", "wu", "wd")
ODD_WEIGHT_NAMES = ("g_mix", "w_in", "g_v", "w_s", "b_s", "w_out",
                    "g_ffn", "wg", "wu", "wd", "g_final")
ODD_MIXER_WEIGHT_NAMES = ("w_in", "w_out")
ODD_BIG_WEIGHT_NAMES = ODD_MIXER_WEIGHT_NAMES + ("wg", "wu", "wd")
N_CAST = 5


def _pool_block_diag(groups_ref, tiles_ref):
  per_tile = V7X_MXU_DIM // POOL_GROUP
  tiles_ref[...] = jnp.zeros(tiles_ref.shape, tiles_ref.dtype)
  for g in range(groups_ref.shape[0]):
    i, j = divmod(g, per_tile)
    lo = j * POOL_GROUP
    tiles_ref[i, lo:lo + POOL_GROUP, lo:lo + POOL_GROUP] = (
        groups_ref[g].astype(tiles_ref.dtype))


def _fetch_bf16(srcs, dsts, stage, sems):
  chunks = []
  for src, dst in zip(srcs, dsts):
    n_rows, n_cols = src.shape
    if dst.shape[1] > n_cols:
      dst[:, n_cols:] = jnp.zeros((n_rows, dst.shape[1] - n_cols), dst.dtype)
    for r0 in range(0, n_rows, FETCH_ROWS):
      chunks.append((src, dst, r0, min(FETCH_ROWS, n_rows - r0), n_cols))

  n_slots = stage.shape[0]

  def copy(i):
    src, _, r0, n, n_cols = chunks[i]
    return pltpu.make_async_copy(src.at[pl.ds(r0, n), :],
                                 stage.at[i % n_slots, pl.ds(0, n), pl.ds(0, n_cols)],
                                 sems.at[i % n_slots])

  for i in range(min(n_slots - 1, len(chunks))):
    copy(i).start()
  for i, (_, dst, r0, n, n_cols) in enumerate(chunks):
    if i + n_slots - 1 < len(chunks):
      copy(i + n_slots - 1).start()
    copy(i).wait()
    dst[pl.ds(r0, n), :n_cols] = stage[i % n_slots, :n, :n_cols].astype(dst.dtype)


def _run(stages):
  for stage in stages:
    stage()


def _pipelined_steps(s, n_tiles, mix_stages, ffn_stages):
  @pl.when(s == 0)
  def _():
    _run(mix_stages())

  @pl.when((s > 0) & (s < n_tiles))
  def _():
    mix, ffn = mix_stages(), ffn_stages()
    _run([ffn[0], ffn[1], mix[0], mix[1], mix[2], ffn[2], ffn[3], mix[3], mix[4]])

  @pl.when(s == n_tiles)
  def _():
    _run(ffn_stages())


def _set(ref):
  def emit(value):
    ref[...] = value
  return emit


def _set_rows(ref):
  def emit(value):
    ref[:, 0, :] = value
  return emit


def _even_kernel(*refs, n_tiles, tiles_per_seq, tile, start, d_pool, d_conv, layer,
                 norm_row):
  x_ref, xs_ref, pool_hist_hbm, conv_hist_ref = refs[:4]
  n_small, n_big = len(EVEN_SMALL_WEIGHT_NAMES), len(EVEN_BIG_WEIGHT_NAMES)
  w = dict(zip(EVEN_SMALL_WEIGHT_NAMES, refs[4:4 + n_small]))
  big_hbm = refs[4 + n_small:4 + n_small + n_big]
  cast_in = refs[4 + n_small + n_big:4 + n_small + n_big + N_CAST]
  outs = refs[4 + n_small + n_big + N_CAST:]
  y_ref, pool_state_ref, conv_state_ref, ys_ref, ps_ref, cs_ref = outs[:6]
  cast_out = outs[6:6 + N_CAST]
  pool_next_hbm = outs[6 + N_CAST]
  scratch = outs[7 + N_CAST:]
  p_hist, c_hist, x1_buf = scratch[:3]
  big_vmem = scratch[3:3 + n_big]
  stage, sems, hist_sems, pool_tiles = scratch[3 + n_big:]
  w.update(zip(EVEN_BIG_WEIGHT_NAMES, big_vmem))
  w["g_mix"] = w["g_mix"].at[pl.ds(norm_row, 1)]
  w["g_ffn"] = w["g_ffn"].at[pl.ds(norm_row, 1)]
  w["conv_w"] = w["conv_w"].at[0]
  pool_groups = w["w_pool"]
  w["w_pool"] = pool_tiles

  rows = xs_ref.shape[0]
  per_slot = stage.shape[2] // d_pool
  assert rows <= stage.shape[1] and POOL_HIST <= stage.shape[0] * per_slot

  def pool_row(k):
    return stage.at[k // per_slot, pl.ds(0, rows), pl.ds((k % per_slot) * d_pool, d_pool)]

  def row_in(k):
    return pltpu.make_async_copy(pool_hist_hbm.at[k], pool_row(k), hist_sems.at[0, k])

  def row_out(k):
    src = pool_row(k + 1) if k + 1 < POOL_HIST else ps_ref
    return pltpu.make_async_copy(src, pool_next_hbm.at[k], hist_sems.at[1, k])

  s = pl.program_id(0)
  t = s % tiles_per_seq
  slot = s % 2

  @pl.when(s == 0)
  def _():
    _pool_block_diag(pool_groups, pool_tiles)
    _fetch_bf16([m.at[layer[i]] for i, m in enumerate(big_hbm)], big_vmem, stage, sems)

  @pl.when(s < n_tiles)
  def _():
    _even_mix_head(t, p_hist, c_hist)
    for src, dst in zip(cast_in, cast_out):
      n = src.shape[1]
      dst[:, :n] = src[...].astype(dst.dtype)
      if dst.shape[1] > n:
        dst[:, n:] = jnp.zeros((dst.shape[0], dst.shape[1] - n), dst.dtype)

  def mix_stages():
    return _even_mix_stages(
        lambda: x_ref[...], w, _set(x1_buf.at[slot]),
        _even_prompt_mixers(t, p_hist, c_hist, w["conv_w"], tile), d_pool, d_conv)

  def ffn_stages():
    return _ffn_stages(lambda: x1_buf[1 - slot], w, _set(y_ref), False)

  @pl.when(s == n_tiles)
  def _():
    for k in range(POOL_HIST):
      row_in(k).start()

  _pipelined_steps(s, n_tiles, mix_stages, ffn_stages)

  @pl.when(s < n_tiles)
  def _():
    _even_mix_tail(t, pool_state_ref, conv_state_ref, p_hist, c_hist,
                   last_t=tiles_per_seq - 1)

  @pl.when(s == n_tiles + 1)
  def _():
    for k in range(POOL_HIST):
      row_in(k).wait()
    for k in range(POOL_HIST - 1):
      row_out(k).start()
    st = {}
    mixers = _even_sample_mixers(pool_row, conv_hist_ref, w["conv_w"],
                                 ps_ref, cs_ref, start)
    _run(_even_mix_stages(lambda: xs_ref[:, 0, :], w, lambda v: st.update(x1=v),
                          mixers, d_pool, d_conv))
    row_out(POOL_HIST - 1).start()
    _run(_ffn_stages(lambda: st["x1"], w, _set(ys_ref), False))
    for k in range(POOL_HIST):
      row_out(k).wait()


def _odd_mix_stages(load_x, w, emit, gating, d_gate):
  st = {}

  def norm():
    st["x"] = load_x()
    st["h"] = _rmsnorm(st["x"], w["g_mix"][...]).astype(jnp.bfloat16)

  def in_proj():
    st["z"] = _skewed_dot(st["h"], w["w_in"], 2 * d_gate)

  def activate():
    z = _gelu_exact(st["z"])
    st["u"] = z[:, :d_gate]
    st["v"] = _rmsnorm(z[:, d_gate:], w["g_v"][...])

  def gate():
    st["gated"] = (st["u"] * gating(st["v"])).astype(jnp.bfloat16)

  def out_proj():
    emit(st["x"] + _skewed_dot(st["gated"], w["w_out"], st["x"].shape[1]))

  return [norm, in_proj, activate, gate, out_proj]


def _odd_prompt_gating(w, tile, d_gate):
  def gating(v):
    vb = v.astype(jnp.bfloat16)
    n_chunks = tile // CHUNK
    head = d_gate // N_SG_HEADS
    row = lax.broadcasted_iota(jnp.int32, (CHUNK, CHUNK), 0)
    col = lax.broadcasted_iota(jnp.int32, (CHUNK, CHUNK), 1)
    causal = row >= col
    bias = w["b_s"][...].T
    per_head = []
    for hd in range(N_SG_HEADS):
      w_h = jnp.where(causal, w["w_s"][hd], 0.0).astype(jnp.bfloat16)
      rhs = jnp.concatenate(
          [vb[c * CHUNK:(c + 1) * CHUNK, hd * head:(hd + 1) * head]
           for c in range(n_chunks)], axis=1)
      o = jnp.dot(w_h, rhs, preferred_element_type=jnp.float32)
      per_head.append(o + bias[:, hd:hd + 1])
    return jnp.concatenate(
        [jnp.concatenate([o[:, c * head:(c + 1) * head] for o in per_head], axis=1)
         for c in range(n_chunks)], axis=0)
  return gating


def _odd_sample_gating(w, v_ref, d_gate):
  def gating(v):
    v_ref[:, 0, :] = v
    head = d_gate // N_SG_HEADS
    rows = v.shape[0]
    vb = v.astype(jnp.bfloat16).astype(jnp.float32)
    per_head = []
    for hd in range(N_SG_HEADS):
      w00 = w["w_s"][hd, 0:1, 0:1].astype(jnp.bfloat16).astype(jnp.float32)
      b0 = w["b_s"][hd:hd + 1, 0:1]
      per_head.append(vb[:, hd * head:(hd + 1) * head] * jnp.broadcast_to(w00, (rows, head))
                      + jnp.broadcast_to(b0, (rows, head)))
    return jnp.concatenate(per_head, axis=1)
  return gating


def _odd_kernel(*refs, n_tiles, tile, d_gate, norm_row):
  x_ref, xs_ref = refs[:2]
  n_w = len(ODD_WEIGHT_NAMES)
  w = dict(zip(ODD_WEIGHT_NAMES, refs[2:2 + n_w]))
  y_ref, ys_ref, vs_ref, x1_buf = refs[2 + n_w:2 + n_w + 4]
  big_vmem = refs[2 + n_w + 4:-1]
  sems = refs[-1]
  copies = {name: pltpu.make_async_copy(w[name], buf, sems.at[i])
            for i, (name, buf) in enumerate(zip(ODD_BIG_WEIGHT_NAMES, big_vmem))}
  w.update(zip(ODD_BIG_WEIGHT_NAMES, big_vmem))
  w["g_mix"] = w["g_mix"].at[pl.ds(norm_row, 1)]
  w["g_ffn"] = w["g_ffn"].at[pl.ds(norm_row, 1)]

  s = pl.program_id(0)
  slot = s % 2

  @pl.when(s == 0)
  def _():
    for name in ODD_BIG_WEIGHT_NAMES:
      copies[name].start()
    for name in ODD_MIXER_WEIGHT_NAMES:
      copies[name].wait()

  def mix_stages():
    return _odd_mix_stages(lambda: x_ref[...], w, _set(x1_buf.at[slot]),
                           _odd_prompt_gating(w, tile, d_gate), d_gate)

  def ffn_stages():
    return _ffn_stages(lambda: x1_buf[1 - slot], w, _set(y_ref), True)

  _pipelined_steps(s, n_tiles, mix_stages, ffn_stages)

  @pl.when(s == 0)
  def _():
    for name in ODD_BIG_WEIGHT_NAMES:
      if name not in ODD_MIXER_WEIGHT_NAMES:
        copies[name].wait()

  @pl.when(s == n_tiles + 1)
  def _():
    st = {}
    _run(_odd_mix_stages(lambda: xs_ref[...], w, lambda v: st.update(x1=v),
                         _odd_sample_gating(w, vs_ref, d_gate), d_gate))
    _run(_ffn_stages(lambda: st["x1"], w, _set_rows(ys_ref), True))


def _resident(arr):
  zeros = (0,) * arr.ndim
  return pl.BlockSpec(arr.shape, lambda s: zeros, pipeline_mode=pl.Buffered(1))


def _whole_out(shape):
  zeros = (0,) * len(shape)
  return pl.BlockSpec(shape, lambda s: zeros)


def _params():
  return pltpu.CompilerParams(dimension_semantics=("arbitrary",),
                              vmem_limit_bytes=V7X_VMEM_LIMIT_BYTES)


def _prompt_specs(x, tile):
  batch, seq, d_model = x.shape
  tiles_per_seq = seq // tile
  n_tiles = batch * tiles_per_seq
  mix_tile = lambda s: jnp.clip(s, 0, n_tiles - 1)
  ffn_tile = lambda s: jnp.clip(s - 1, 0, n_tiles - 1)
  block = lambda which: pl.BlockSpec(
      (None, tile, d_model),
      lambda s: (which(s) // tiles_per_seq, which(s) % tiles_per_seq, 0))
  per_seq = lambda rows, ch: pl.BlockSpec(
      (None, rows, ch), lambda s: (mix_tile(s) // tiles_per_seq, 0, 0))
  return n_tiles, tiles_per_seq, mix_tile, block(mix_tile), block(ffn_tile), per_seq


def _cast_slab_spec(rows, cols, n_steps, step, layer):
  slab = rows // n_steps
  steps_per_slab = 1
  while slab % V7X_BF16_SUBLANES:
    slab *= 2
    steps_per_slab *= 2
  src = pl.BlockSpec((None, slab, cols), lambda s: (layer, step(s) // steps_per_slab, 0))
  dst = pl.BlockSpec((slab, cols + _skew_cols(cols)),
                     lambda s: (step(s) // steps_per_slab, 0))
  return src, dst


def _even_layer(x, xs, pool_hist, conv_hist, small_weights, norm_row, mats, mat_layer,
                next_mats, next_layer, tile, start):
  n_tiles, tiles_per_seq, step, tok_in, tok_out, per_seq = _prompt_specs(x, tile)
  batch = x.shape[0]
  rows, _, d_model = xs.shape
  d_pool = pool_hist.shape[-1]
  d_conv = conv_hist.shape[-1]
  cast_specs = [_cast_slab_spec(m.shape[1], m.shape[2], n_tiles, step, next_layer[i])
                for i, m in enumerate(next_mats)]
  f32 = x.dtype
  skewed = lambda m: (m.shape[1], m.shape[2] + _skew_cols(m.shape[2]))
  widest = max(m.shape[2] for m in mats)
  return pl.pallas_call(
      functools.partial(_even_kernel, n_tiles=n_tiles,
                        tiles_per_seq=tiles_per_seq, tile=tile, start=start,
                        d_pool=d_pool, d_conv=d_conv, layer=tuple(mat_layer),
                        norm_row=norm_row),
      grid=(n_tiles + 2,),
      in_specs=([tok_in, _resident(xs), pl.BlockSpec(memory_space=pl.ANY),
                 _resident(conv_hist)]
                + [_resident(w) for w in small_weights]
                + [pl.BlockSpec(memory_space=pl.ANY) for _ in mats]
                + [c[0] for c in cast_specs]),
      out_specs=([tok_out, per_seq(POOL_HIST, d_pool), per_seq(CONV_HIST, d_conv),
                  _whole_out((rows, d_model)), _whole_out((rows, d_pool)),
                  _whole_out(conv_hist.shape)] + [c[1] for c in cast_specs]
                 + [pl.BlockSpec(memory_space=pl.ANY)]),
      out_shape=([jax.ShapeDtypeStruct(x.shape, f32),
                  jax.ShapeDtypeStruct((batch, POOL_HIST, d_pool), f32),
                  jax.ShapeDtypeStruct((batch, CONV_HIST, d_conv), f32),
                  jax.ShapeDtypeStruct((rows, d_model), f32),
                  jax.ShapeDtypeStruct((rows, d_pool), f32),
                  jax.ShapeDtypeStruct(conv_hist.shape, f32)]
                 + [jax.ShapeDtypeStruct(skewed(m), jnp.bfloat16) for m in next_mats]
                 + [jax.ShapeDtypeStruct(pool_hist.shape, f32)]),
      scratch_shapes=([pltpu.VMEM((POOL_PAD, d_pool), jnp.float32),
                       pltpu.VMEM((CONV_PAD, d_conv), jnp.float32),
                       pltpu.VMEM((2, tile, d_model), jnp.float32)]
                      + [pltpu.VMEM(skewed(m), jnp.bfloat16) for m in mats]
                      + [pltpu.VMEM((FETCH_SLOTS, FETCH_ROWS, widest), jnp.float32),
                         pltpu.SemaphoreType.DMA((FETCH_SLOTS,)),
                         pltpu.SemaphoreType.DMA((2, POOL_HIST)),
                         pltpu.VMEM((len(POOL_WINDOWS) * POOL_GROUP // V7X_MXU_DIM,
                                     V7X_MXU_DIM, V7X_MXU_DIM), jnp.bfloat16)]),
      compiler_params=_params(),
      name="even_layer",
  )(x, xs, pool_hist, conv_hist, *small_weights, *mats, *next_mats)


def _odd_layer(x, xs, weights, norm_row, tile):
  n_tiles, _, _, tok_in, tok_out, _ = _prompt_specs(x, tile)
  rows, d_model = xs.shape
  d_gate = weights[ODD_WEIGHT_NAMES.index("g_v")].shape[-1]
  big = {name: w for name, w in zip(ODD_WEIGHT_NAMES, weights)
         if name in ODD_BIG_WEIGHT_NAMES}
  return pl.pallas_call(
      functools.partial(_odd_kernel, n_tiles=n_tiles, tile=tile, d_gate=d_gate,
                        norm_row=norm_row),
      grid=(n_tiles + 2,),
      in_specs=([tok_in, _resident(xs)]
                + [pl.BlockSpec(memory_space=pl.ANY) if name in ODD_BIG_WEIGHT_NAMES
                   else _resident(w) for name, w in zip(ODD_WEIGHT_NAMES, weights)]),
      out_specs=[tok_out, _whole_out((rows, 1, d_model)), _whole_out((rows, 1, d_gate))],
      out_shape=[jax.ShapeDtypeStruct(x.shape, x.dtype),
                 jax.ShapeDtypeStruct((rows, 1, d_model), x.dtype),
                 jax.ShapeDtypeStruct((rows, 1, d_gate), x.dtype)],
      scratch_shapes=([pltpu.VMEM((2, tile, d_model), jnp.float32)]
                      + [pltpu.VMEM(big[name].shape, big[name].dtype)
                         for name in ODD_BIG_WEIGHT_NAMES]
                      + [pltpu.SemaphoreType.DMA((len(ODD_BIG_WEIGHT_NAMES),))]),
      compiler_params=_params(),
      name="odd_layer",
  )(x, xs, *weights)


def kernel(x_prompt, x_sample, state_pool, state_conv, norm_mix, norm_ffn, norm_final, w_in_even, w_pool, pool_scale, conv_w, w_out_even, w_in_odd, norm_sg, w_s, b_s, w_out_odd, ffn_w_gate, ffn_w_up, ffn_w_down):
  depth = norm_mix.shape[0]
  assert depth == 2 and w_in_even.shape[0] == 1 and w_in_odd.shape[0] == 1
  assert x_sample.shape[1] == 1
  assert x_prompt.shape[1] % PROMPT_TILE == 0 and PROMPT_TILE % CHUNK == 0

  even_small = (norm_mix, w_pool[0], pool_scale, conv_w, norm_ffn)
  even_mats = (w_in_even, w_out_even, ffn_w_gate, ffn_w_up, ffn_w_down)
  even_mat_layer = (0, 0, 0, 0, 0)
  odd_mats = (w_in_odd, w_out_odd, ffn_w_gate, ffn_w_up, ffn_w_down)
  odd_mat_layer = (0, 0, 1, 1, 1)

  pool_hist = jnp.swapaxes(state_pool[0], 0, 1)
  conv_hist = state_conv[0]

  (x1_p, pool_p, conv_p, x1_s, _, conv_s, w_in_o, w_out_o, wg_o, wu_o, wd_o,
   pool_s) = _even_layer(x_prompt, x_sample, pool_hist, conv_hist, even_small, 0,
                         even_mats, even_mat_layer, odd_mats, odd_mat_layer,
                         PROMPT_TILE, PAST_LEN)

  odd_w = (norm_mix, w_in_o, norm_sg, w_s[0], b_s[0], w_out_o,
           norm_ffn, wg_o, wu_o, wd_o, norm_final.reshape(1, -1))
  y_p, y_s, v_s = _odd_layer(x1_p, x1_s, odd_w, 1, PROMPT_TILE)

  pool_s = jnp.swapaxes(pool_s, 0, 1)[None]

  return (y_p, y_s, pool_p[None], pool_s, conv_p[None], conv_s[None], v_s[None])
```

```python
import functools
import math

import jax
import jax.numpy as jnp
from jax import lax
from jax.experimental import pallas as pl
from jax.experimental.pallas import tpu as pltpu

POOL_WINDOWS = (2, 4, 8, 16)
POOL_GROUP = 128
POOL_HIST = max(POOL_WINDOWS) - 1
CONV_WIDTH = 3
CONV_HIST = CONV_WIDTH - 1
CHUNK = 128
N_SG_HEADS = 8
EPS = 1e-6
PAST_LEN = 16384

V7X_SUBLANES = 8
V7X_LANES = 128
V7X_BF16_SUBLANES = 16
V7X_MXU_DIM = 256
V7X_VMEM_LIMIT_BYTES = 60000 * 1024

POOL_PAD = -(-POOL_HIST // V7X_SUBLANES) * V7X_SUBLANES
CONV_PAD = -(-CONV_HIST // V7X_SUBLANES) * V7X_SUBLANES

PROMPT_TILE = 512
FETCH_ROWS = 128
FETCH_SLOTS = 4


def _bf16_dot(a, b):
  return jnp.dot(a.astype(jnp.bfloat16), b, preferred_element_type=jnp.float32)


def _rmsnorm(x, g):
  y = x * lax.rsqrt(jnp.mean(x * x, axis=-1, keepdims=True) + EPS)
  return y * g


def _gelu_exact(x):
  return 0.5 * x * (1.0 + lax.erf(x * math.sqrt(0.5)))


def _skew_cols(n):
  return V7X_LANES if (n // V7X_LANES) % V7X_SUBLANES == 0 else 0


def _skewed_dot(a, w_ref, n):
  return jnp.dot(a, w_ref[:, :n], preferred_element_type=jnp.float32)


def _pool_project(d, w_bd_ref, scale):
  halves = []
  for i in range(w_bd_ref.shape[0]):
    lo = i * V7X_MXU_DIM
    halves.append(_bf16_dot(d[:, lo:lo + V7X_MXU_DIM], w_bd_ref[i]))
  return jnp.concatenate(halves, axis=-1) * scale


def _ffn_stages(load_x1, w, emit, final_norm):
  st = {}

  def norm():
    st["x"] = load_x1()
    st["h"] = _rmsnorm(st["x"], w["g_ffn"][...]).astype(jnp.bfloat16)

  def gate_up():
    st["gate"] = jnp.dot(st["h"], w["wg"][...], preferred_element_type=jnp.float32)
    st["up"] = jnp.dot(st["h"], w["wu"][...], preferred_element_type=jnp.float32)

  def activate():
    gate = st["gate"]
    st["act"] = (gate * jax.nn.sigmoid(gate) * st["up"]).astype(jnp.bfloat16)

  def down():
    y = st["x"] + _skewed_dot(st["act"], w["wd"], st["x"].shape[1])
    emit(_rmsnorm(y, w["g_final"][...]) if final_norm else y)

  return [norm, gate_up, activate, down]


def _even_mix_stages(load_x, w, emit, mixers, d_pool, d_conv):
  st = {}

  def norm():
    st["x"] = load_x()
    st["h"] = _rmsnorm(st["x"], w["g_mix"][...]).astype(jnp.bfloat16)

  def in_proj():
    st["z"] = _skewed_dot(st["h"], w["w_in"], d_pool + 3 * d_conv)

  def mix():
    z = st["z"]
    p = z[:, :d_pool]
    xb = z[:, d_pool:d_pool + d_conv]
    bg = z[:, d_pool + d_conv:d_pool + 2 * d_conv]
    cg = z[:, d_pool + 2 * d_conv:]
    pooled, conv_y = mixers(p, cg * xb)
    st["pool_d"] = (pooled - p).astype(jnp.bfloat16)
    st["conv_out"] = (bg * conv_y).astype(jnp.bfloat16)

  def project():
    a_out = _pool_project(st["pool_d"], w["w_pool"], w["pool_scale"][...])
    st["mixed"] = jnp.concatenate([a_out.astype(jnp.bfloat16), st["conv_out"]], axis=-1)

  def out_proj():
    emit(st["x"] + _skewed_dot(st["mixed"], w["w_out"], st["x"].shape[1]))

  return [norm, in_proj, mix, project, out_proj]


def _rows_back(x, k):
  return pltpu.roll(x, k, axis=0)


def _even_prompt_mixers(t, p_hist, c_hist, conv_w, tile):
  assert all(w & (w - 1) == 0 and w <= POOL_PAD for w in POOL_WINDOWS)

  def mixers(p, c):
    pos = t * tile + lax.broadcasted_iota(jnp.int32, (POOL_PAD, POOL_GROUP), 0)
    pooled = []
    for g, win in enumerate(POOL_WINDOWS):
      lo = g * POOL_GROUP
      sums = jnp.concatenate([p_hist[:, lo:lo + POOL_GROUP], p[:, lo:lo + POOL_GROUP]],
                             axis=0)
      span = 1
      while span < win:
        sums = sums + _rows_back(sums, span)
        span *= 2
      sums = sums[POOL_PAD:]
      head = sums[:POOL_PAD] / jnp.minimum(pos + 1, win).astype(jnp.float32)
      pooled.append(jnp.concatenate([head, sums[POOL_PAD:] * (1.0 / win)], axis=0))
    p_hist[...] = p[tile - POOL_PAD:, :]
    c_rows = jnp.concatenate([c_hist[...], c], axis=0)
    y = _rows_back(c_rows, 2)[CONV_PAD:] * conv_w[0:1, :]
    y = y + _rows_back(c_rows, 1)[CONV_PAD:] * conv_w[1:2, :]
    y = y + c * conv_w[2:3, :]
    c_hist[...] = c[tile - CONV_PAD:, :]
    return jnp.concatenate(pooled, axis=-1), y
  return mixers


def _even_sample_mixers(pool_row, conv_hist_ref, conv_w, p_ref, conv_next_ref, start):
  def mixers(p, c):
    p_ref[...] = p
    for k in range(CONV_HIST - 1):
      conv_next_ref[:, k, :] = conv_hist_ref[:, k + 1, :]
    conv_next_ref[:, CONV_HIST - 1, :] = c
    pooled = []
    for g, win in enumerate(POOL_WINDOWS):
      lo = g * POOL_GROUP
      acc = p[:, lo:lo + POOL_GROUP]
      for k in range(1, win):
        acc = acc + pool_row(POOL_HIST - k)[:, lo:lo + POOL_GROUP]
      pooled.append(acc / float(min(start + 1, win)))
    y = conv_hist_ref[:, 0, :] * conv_w[0:1, :]
    y = y + conv_hist_ref[:, 1, :] * conv_w[1:2, :]
    y = y + c * conv_w[2:3, :]
    return jnp.concatenate(pooled, axis=-1), y
  return mixers


def _even_mix_head(t, p_hist, c_hist):
  @pl.when(t == 0)
  def _():
    p_hist[...] = jnp.zeros(p_hist.shape, jnp.float32)
    c_hist[...] = jnp.zeros(c_hist.shape, jnp.float32)


def _even_mix_tail(t, pool_state_ref, conv_state_ref, p_hist, c_hist, *, last_t):
  @pl.when(t == last_t)
  def _():
    pool_state_ref[...] = p_hist[POOL_PAD - POOL_HIST:, :]
    conv_state_ref[...] = c_hist[CONV_PAD - CONV_HIST:, :]


EVEN_SMALL_WEIGHT_NAMES = ("g_mix", "w_pool", "pool_scale", "conv_w", "g_ffn")
EVEN_BIG_WEIGHT_NAMES = ("w_in", "w_out", "wg", "wu", "wd")
ODD_WEIGHT_NAMES = ("g_mix", "w_in", "g_v", "w_s", "b_s", "w_out",
                    "g_ffn", "wg", "wu", "wd", "g_final")
ODD_MIXER_WEIGHT_NAMES = ("w_in", "w_out")
ODD_BIG_WEIGHT_NAMES = ODD_MIXER_WEIGHT_NAMES + ("wg", "wu", "wd")
N_CAST = 5


def _pool_block_diag(groups_ref, tiles_ref):
  per_tile = V7X_MXU_DIM // POOL_GROUP
  tiles_ref[...] = jnp.zeros(tiles_ref.shape, tiles_ref.dtype)
  for g in range(groups_ref.shape[0]):
    i, j = divmod(g, per_tile)
    lo = j * POOL_GROUP
    tiles_ref[i, lo:lo + POOL_GROUP, lo:lo + POOL_GROUP] = (
        groups_ref[g].astype(tiles_ref.dtype))


def _fetch_bf16(srcs, dsts, stage, sems):
  chunks = []
  for src, dst in zip(srcs, dsts):
    n_rows, n_cols = src.shape
    if dst.shape[1] > n_cols:
      dst[:, n_cols:] = jnp.zeros((n_rows, dst.shape[1] - n_cols), dst.dtype)
    for r0 in range(0, n_rows, FETCH_ROWS):
      chunks.append((src, dst, r0, min(FETCH_ROWS, n_rows - r0), n_cols))

  n_slots = stage.shape[0]

  def copy(i):
    src, _, r0, n, n_cols = chunks[i]
    return pltpu.make_async_copy(src.at[pl.ds(r0, n), :],
                                 stage.at[i % n_slots, pl.ds(0, n), pl.ds(0, n_cols)],
                                 sems.at[i % n_slots])

  for i in range(min(n_slots - 1, len(chunks))):
    copy(i).start()
  for i, (_, dst, r0, n, n_cols) in enumerate(chunks):
    if i + n_slots - 1 < len(chunks):
      copy(i + n_slots - 1).start()
    copy(i).wait()
    dst[pl.ds(r0, n), :n_cols] = stage[i % n_slots, :n, :n_cols].astype(dst.dtype)


def _run(stages):
  for stage in stages:
    stage()


def _pipelined_steps(s, n_tiles, mix_stages, ffn_stages):
  @pl.when(s == 0)
  def _():
    _run(mix_stages())

  @pl.when((s > 0) & (s < n_tiles))
  def _():
    mix, ffn = mix_stages(), ffn_stages()
    _run([ffn[0], ffn[1], mix[0], mix[1], mix[2], ffn[2], ffn[3], mix[3], mix[4]])

  @pl.when(s == n_tiles)
  def _():
    _run(ffn_stages())


def _set(ref):
  def emit(value):
    ref[...] = value
  return emit


def _set_rows(ref):
  def emit(value):
    ref[:, 0, :] = value
  return emit


def _even_kernel(*refs, n_tiles, tiles_per_seq, tile, start, d_pool, d_conv, layer,
                 norm_row):
  x_ref, xs_ref, pool_hist_hbm, conv_hist_ref = refs[:4]
  n_small, n_big = len(EVEN_SMALL_WEIGHT_NAMES), len(EVEN_BIG_WEIGHT_NAMES)
  w = dict(zip(EVEN_SMALL_WEIGHT_NAMES, refs[4:4 + n_small]))
  big_hbm = refs[4 + n_small:4 + n_small + n_big]
  cast_in = refs[4 + n_small + n_big:4 + n_small + n_big + N_CAST]
  outs = refs[4 + n_small + n_big + N_CAST:]
  y_ref, pool_state_ref, conv_state_ref, ys_ref, ps_ref, cs_ref = outs[:6]
  cast_out = outs[6:6 + N_CAST]
  pool_next_hbm = outs[6 + N_CAST]
  scratch = outs[7 + N_CAST:]
  p_hist, c_hist, x1_buf = scratch[:3]
  big_vmem = scratch[3:3 + n_big]
  stage, sems, hist_sems, pool_tiles = scratch[3 + n_big:]
  w.update(zip(EVEN_BIG_WEIGHT_NAMES, big_vmem))
  w["g_mix"] = w["g_mix"].at[pl.ds(norm_row, 1)]
  w["g_ffn"] = w["g_ffn"].at[pl.ds(norm_row, 1)]
  w["conv_w"] = w["conv_w"].at[0]
  pool_groups = w["w_pool"]
  w["w_pool"] = pool_tiles

  rows = xs_ref.shape[0]
  per_slot = stage.shape[2] // d_pool
  assert rows <= stage.shape[1] and POOL_HIST <= stage.shape[0] * per_slot

  def pool_row(k):
    return stage.at[k // per_slot, pl.ds(0, rows), pl.ds((k % per_slot) * d_pool, d_pool)]

  def row_in(k):
    return pltpu.make_async_copy(pool_hist_hbm.at[k], pool_row(k), hist_sems.at[0, k])

  def row_out(k):
    src = pool_row(k + 1) if k + 1 < POOL_HIST else ps_ref
    return pltpu.make_async_copy(src, pool_next_hbm.at[k], hist_sems.at[1, k])

  s = pl.program_id(0)
  t = s % tiles_per_seq

  @pl.when(s == 0)
  def _():
    _pool_block_diag(pool_groups, pool_tiles)
    _fetch_bf16([m.at[layer[i]] for i, m in enumerate(big_hbm)], big_vmem, stage, sems)

  @pl.when(s < n_tiles)
  def _():
    _even_mix_head(t, p_hist, c_hist)
    for src, dst in zip(cast_in, cast_out):
      n = src.shape[1]
      dst[:, :n] = src[...].astype(dst.dtype)
      if dst.shape[1] > n:
        dst[:, n:] = jnp.zeros((dst.shape[0], dst.shape[1] - n), dst.dtype)

  def mix_stages():
    return _even_mix_stages(
        lambda: x_ref[...], w, _set(x1_buf),
        _even_prompt_mixers(t, p_hist, c_hist, w["conv_w"], tile), d_pool, d_conv)

  def ffn_stages():
    return _ffn_stages(lambda: x1_buf[...], w, _set(y_ref), False)

  @pl.when(s == n_tiles)
  def _():
    for k in range(POOL_HIST):
      row_in(k).start()

  _pipelined_steps(s, n_tiles, mix_stages, ffn_stages)

  @pl.when(s < n_tiles)
  def _():
    _even_mix_tail(t, pool_state_ref, conv_state_ref, p_hist, c_hist,
                   last_t=tiles_per_seq - 1)

  @pl.when(s == n_tiles + 1)
  def _():
    for k in range(POOL_HIST):
      row_in(k).wait()
    for k in range(POOL_HIST - 1):
      row_out(k).start()
    st = {}
    mixers = _even_sample_mixers(pool_row, conv_hist_ref, w["conv_w"],
                                 ps_ref, cs_ref, start)
    _run(_even_mix_stages(lambda: xs_ref[:, 0, :], w, lambda v: st.update(x1=v),
                          mixers, d_pool, d_conv))
    row_out(POOL_HIST - 1).start()
    _run(_ffn_stages(lambda: st["x1"], w, _set(ys_ref), False))
    for k in range(POOL_HIST):
      row_out(k).wait()


def _odd_mix_stages(load_x, w, emit, gating, d_gate):
  st = {}

  def norm():
    st["x"] = load_x()
    st["h"] = _rmsnorm(st["x"], w["g_mix"][...]).astype(jnp.bfloat16)

  def in_proj():
    st["z"] = _skewed_dot(st["h"], w["w_in"], 2 * d_gate)

  def activate():
    z = _gelu_exact(st["z"])
    st["u"] = z[:, :d_gate]
    st["v"] = _rmsnorm(z[:, d_gate:], w["g_v"][...])

  def gate():
    st["gated"] = (st["u"] * gating(st["v"])).astype(jnp.bfloat16)

  def out_proj():
    emit(st["x"] + _skewed_dot(st["gated"], w["w_out"], st["x"].shape[1]))

  return [norm, in_proj, activate, gate, out_proj]


def _odd_prompt_gating(w, tile, d_gate):
  def gating(v):
    vb = v.astype(jnp.bfloat16)
    n_chunks = tile // CHUNK
    head = d_gate // N_SG_HEADS
    row = lax.broadcasted_iota(jnp.int32, (CHUNK, CHUNK), 0)
    col = lax.broadcasted_iota(jnp.int32, (CHUNK, CHUNK), 1)
    causal = row >= col
    bias = w["b_s"][...].T
    per_head = []
    for hd in range(N_SG_HEADS):
      w_h = jnp.where(causal, w["w_s"][hd], 0.0).astype(jnp.bfloat16)
      rhs = jnp.concatenate(
          [vb[c * CHUNK:(c + 1) * CHUNK, hd * head:(hd + 1) * head]
           for c in range(n_chunks)], axis=1)
      o = jnp.dot(w_h, rhs, preferred_element_type=jnp.float32)
      per_head.append(o + bias[:, hd:hd + 1])
    return jnp.concatenate(
        [jnp.concatenate([o[:, c * head:(c + 1) * head] for o in per_head], axis=1)
         for c in range(n_chunks)], axis=0)
  return gating


def _odd_sample_gating(w, v_ref, d_gate):
  def gating(v):
    v_ref[:, 0, :] = v
    head = d_gate // N_SG_HEADS
    rows = v.shape[0]
    vb = v.astype(jnp.bfloat16).astype(jnp.float32)
    per_head = []
    for hd in range(N_SG_HEADS):
      w00 = w["w_s"][hd, 0:1, 0:1].astype(jnp.bfloat16).astype(jnp.float32)
      b0 = w["b_s"][hd:hd + 1, 0:1]
      per_head.append(vb[:, hd * head:(hd + 1) * head] * jnp.broadcast_to(w00, (rows, head))
                      + jnp.broadcast_to(b0, (rows, head)))
    return jnp.concatenate(per_head, axis=1)
  return gating


def _odd_kernel(*refs, n_tiles, tile, d_gate, norm_row):
  x_ref, xs_ref = refs[:2]
  n_w = len(ODD_WEIGHT_NAMES)
  w = dict(zip(ODD_WEIGHT_NAMES, refs[2:2 + n_w]))
  y_ref, ys_ref, vs_ref, x1_buf = refs[2 + n_w:2 + n_w + 4]
  big_vmem = refs[2 + n_w + 4:-1]
  sems = refs[-1]
  copies = {name: pltpu.make_async_copy(w[name], buf, sems.at[i])
            for i, (name, buf) in enumerate(zip(ODD_BIG_WEIGHT_NAMES, big_vmem))}
  w.update(zip(ODD_BIG_WEIGHT_NAMES, big_vmem))
  w["g_mix"] = w["g_mix"].at[pl.ds(norm_row, 1)]
  w["g_ffn"] = w["g_ffn"].at[pl.ds(norm_row, 1)]

  s = pl.program_id(0)

  @pl.when(s == 0)
  def _():
    for name in ODD_BIG_WEIGHT_NAMES:
      copies[name].start()
    for name in ODD_MIXER_WEIGHT_NAMES:
      copies[name].wait()

  def mix_stages():
    return _odd_mix_stages(lambda: x_ref[...], w, _set(x1_buf),
                           _odd_prompt_gating(w, tile, d_gate), d_gate)

  def ffn_stages():
    return _ffn_stages(lambda: x1_buf[...], w, _set(y_ref), True)

  _pipelined_steps(s, n_tiles, mix_stages, ffn_stages)

  @pl.when(s == 0)
  def _():
    for name in ODD_BIG_WEIGHT_NAMES:
      if name not in ODD_MIXER_WEIGHT_NAMES:
        copies[name].wait()

  @pl.when(s == n_tiles + 1)
  def _():
    st = {}
    _run(_odd_mix_stages(lambda: xs_ref[...], w, lambda v: st.update(x1=v),
                         _odd_sample_gating(w, vs_ref, d_gate), d_gate))
    _run(_ffn_stages(lambda: st["x1"], w, _set_rows(ys_ref), True))


def _resident(arr):
  zeros = (0,) * arr.ndim
  return pl.BlockSpec(arr.shape, lambda s: zeros, pipeline_mode=pl.Buffered(1))


def _whole_out(shape):
  zeros = (0,) * len(shape)
  return pl.BlockSpec(shape, lambda s: zeros)


def _params():
  return pltpu.CompilerParams(dimension_semantics=("arbitrary",),
                              vmem_limit_bytes=V7X_VMEM_LIMIT_BYTES)


def _prompt_specs(x, tile):
  batch, seq, d_model = x.shape
  tiles_per_seq = seq // tile
  n_tiles = batch * tiles_per_seq
  mix_tile = lambda s: jnp.clip(s, 0, n_tiles - 1)
  ffn_tile = lambda s: jnp.clip(s - 1, 0, n_tiles - 1)
  block = lambda which: pl.BlockSpec(
      (None, tile, d_model),
      lambda s: (which(s) // tiles_per_seq, which(s) % tiles_per_seq, 0))
  per_seq = lambda rows, ch: pl.BlockSpec(
      (None, rows, ch), lambda s: (mix_tile(s) // tiles_per_seq, 0, 0))
  return n_tiles, tiles_per_seq, mix_tile, block(mix_tile), block(ffn_tile), per_seq


def _cast_slab_spec(rows, cols, n_steps, step, layer):
  slab = rows // n_steps
  steps_per_slab = 1
  while slab % V7X_BF16_SUBLANES:
    slab *= 2
    steps_per_slab *= 2
  src = pl.BlockSpec((None, slab, cols), lambda s: (layer, step(s) // steps_per_slab, 0))
  dst = pl.BlockSpec((slab, cols + _skew_cols(cols)),
                     lambda s: (step(s) // steps_per_slab, 0))
  return src, dst


def _even_layer(x, xs, pool_hist, conv_hist, small_weights, norm_row, mats, mat_layer,
                next_mats, next_layer, tile, start):
  n_tiles, tiles_per_seq, step, tok_in, tok_out, per_seq = _prompt_specs(x, tile)
  batch = x.shape[0]
  rows, _, d_model = xs.shape
  d_pool = pool_hist.shape[-1]
  d_conv = conv_hist.shape[-1]
  cast_specs = [_cast_slab_spec(m.shape[1], m.shape[2], n_tiles, step, next_layer[i])
                for i, m in enumerate(next_mats)]
  f32 = x.dtype
  skewed = lambda m: (m.shape[1], m.shape[2] + _skew_cols(m.shape[2]))
  widest = max(m.shape[2] for m in mats)
  return pl.pallas_call(
      functools.partial(_even_kernel, n_tiles=n_tiles,
                        tiles_per_seq=tiles_per_seq, tile=tile, start=start,
                        d_pool=d_pool, d_conv=d_conv, layer=tuple(mat_layer),
                        norm_row=norm_row),
      grid=(n_tiles + 2,),
      in_specs=([tok_in, _resident(xs), pl.BlockSpec(memory_space=pl.ANY),
                 _resident(conv_hist)]
                + [_resident(w) for w in small_weights]
                + [pl.BlockSpec(memory_space=pl.ANY) for _ in mats]
                + [c[0] for c in cast_specs]),
      out_specs=([tok_out, per_seq(POOL_HIST, d_pool), per_seq(CONV_HIST, d_conv),
                  _whole_out((rows, d_model)), _whole_out((rows, d_pool)),
                  _whole_out(conv_hist.shape)] + [c[1] for c in cast_specs]
                 + [pl.BlockSpec(memory_space=pl.ANY)]),
      out_shape=([jax.ShapeDtypeStruct(x.shape, f32),
                  jax.ShapeDtypeStruct((batch, POOL_HIST, d_pool), f32),
                  jax.ShapeDtypeStruct((batch, CONV_HIST, d_conv), f32),
                  jax.ShapeDtypeStruct((rows, d_model), f32),
                  jax.ShapeDtypeStruct((rows, d_pool), f32),
                  jax.ShapeDtypeStruct(conv_hist.shape, f32)]
                 + [jax.ShapeDtypeStruct(skewed(m), jnp.bfloat16) for m in next_mats]
                 + [jax.ShapeDtypeStruct(pool_hist.shape, f32)]),
      scratch_shapes=([pltpu.VMEM((POOL_PAD, d_pool), jnp.float32),
                       pltpu.VMEM((CONV_PAD, d_conv), jnp.float32),
                       pltpu.VMEM((tile, d_model), jnp.float32)]
                      + [pltpu.VMEM(skewed(m), jnp.bfloat16) for m in mats]
                      + [pltpu.VMEM((FETCH_SLOTS, FETCH_ROWS, widest), jnp.float32),
                         pltpu.SemaphoreType.DMA((FETCH_SLOTS,)),
                         pltpu.SemaphoreType.DMA((2, POOL_HIST)),
                         pltpu.VMEM((len(POOL_WINDOWS) * POOL_GROUP // V7X_MXU_DIM,
                                     V7X_MXU_DIM, V7X_MXU_DIM), jnp.bfloat16)]),
      compiler_params=_params(),
      name="even_layer",
  )(x, xs, pool_hist, conv_hist, *small_weights, *mats, *next_mats)


def _odd_layer(x, xs, weights, norm_row, tile):
  n_tiles, _, _, tok_in, tok_out, _ = _prompt_specs(x, tile)
  rows, d_model = xs.shape
  d_gate = weights[ODD_WEIGHT_NAMES.index("g_v")].shape[-1]
  big = {name: w for name, w in zip(ODD_WEIGHT_NAMES, weights)
         if name in ODD_BIG_WEIGHT_NAMES}
  return pl.pallas_call(
      functools.partial(_odd_kernel, n_tiles=n_tiles, tile=tile, d_gate=d_gate,
                        norm_row=norm_row),
      grid=(n_tiles + 2,),
      in_specs=([tok_in, _resident(xs)]
                + [pl.BlockSpec(memory_space=pl.ANY) if name in ODD_BIG_WEIGHT_NAMES
                   else _resident(w) for name, w in zip(ODD_WEIGHT_NAMES, weights)]),
      out_specs=[tok_out, _whole_out((rows, 1, d_model)), _whole_out((rows, 1, d_gate))],
      out_shape=[jax.ShapeDtypeStruct(x.shape, x.dtype),
                 jax.ShapeDtypeStruct((rows, 1, d_model), x.dtype),
                 jax.ShapeDtypeStruct((rows, 1, d_gate), x.dtype)],
      scratch_shapes=([pltpu.VMEM((tile, d_model), jnp.float32)]
                      + [pltpu.VMEM(big[name].shape, big[name].dtype)
                         for name in ODD_BIG_WEIGHT_NAMES]
                      + [pltpu.SemaphoreType.DMA((len(ODD_BIG_WEIGHT_NAMES),))]),
      compiler_params=_params(),
      name="odd_layer",
  )(x, xs, *weights)


def kernel(x_prompt, x_sample, state_pool, state_conv, norm_mix, norm_ffn, norm_final, w_in_even, w_pool, pool_scale, conv_w, w_out_even, w_in_odd, norm_sg, w_s, b_s, w_out_odd, ffn_w_gate, ffn_w_up, ffn_w_down):
  depth = norm_mix.shape[0]
  assert depth == 2 and w_in_even.shape[0] == 1 and w_in_odd.shape[0] == 1
  assert x_sample.shape[1] == 1
  assert x_prompt.shape[1] % PROMPT_TILE == 0 and PROMPT_TILE % CHUNK == 0

  even_small = (norm_mix, w_pool[0], pool_scale, conv_w, norm_ffn)
  even_mats = (w_in_even, w_out_even, ffn_w_gate, ffn_w_up, ffn_w_down)
  even_mat_layer = (0, 0, 0, 0, 0)
  odd_mats = (w_in_odd, w_out_odd, ffn_w_gate, ffn_w_up, ffn_w_down)
  odd_mat_layer = (0, 0, 1, 1, 1)

  pool_hist = jnp.swapaxes(state_pool[0], 0, 1)
  conv_hist = state_conv[0]

  (x1_p, pool_p, conv_p, x1_s, _, conv_s, w_in_o, w_out_o, wg_o, wu_o, wd_o,
   pool_s) = _even_layer(x_prompt, x_sample, pool_hist, conv_hist, even_small, 0,
                         even_mats, even_mat_layer, odd_mats, odd_mat_layer,
                         PROMPT_TILE, PAST_LEN)

  odd_w = (norm_mix, w_in_o, norm_sg, w_s[0], b_s[0], w_out_o,
           norm_ffn, wg_o, wu_o, wd_o, norm_final.reshape(1, -1))
  y_p, y_s, v_s = _odd_layer(x1_p, x1_s, odd_w, 1, PROMPT_TILE)

  pool_s = jnp.swapaxes(pool_s, 0, 1)[None]

  return (y_p, y_s, pool_p[None], pool_s, conv_p[None], conv_s[None], v_s[None])
```

```python
import functools
import math

import jax
import jax.numpy as jnp
from jax import lax
from jax.experimental import pallas as pl
from jax.experimental.pallas import tpu as pltpu

POOL_WINDOWS = (2, 4, 8, 16)
POOL_GROUP = 128
POOL_HIST = max(POOL_WINDOWS) - 1
CONV_WIDTH = 3
CONV_HIST = CONV_WIDTH - 1
CHUNK = 128
N_SG_HEADS = 8
EPS = 1e-6
PAST_LEN = 16384

V7X_SUBLANES = 8
V7X_LANES = 128
V7X_BF16_SUBLANES = 16
V7X_MXU_DIM = 256
V7X_VMEM_LIMIT_BYTES = 60000 * 1024

POOL_PAD = -(-POOL_HIST // V7X_SUBLANES) * V7X_SUBLANES
CONV_PAD = -(-CONV_HIST // V7X_SUBLANES) * V7X_SUBLANES

PROMPT_TILE = 512
FETCH_ROWS = 128
FETCH_SLOTS = 4


def _bf16_dot(a, b):
  return jnp.dot(a.astype(jnp.bfloat16), b, preferred_element_type=jnp.float32)


def _rmsnorm(x, g):
  y = x * lax.rsqrt(jnp.mean(x * x, axis=-1, keepdims=True) + EPS)
  return y * g


def _gelu_exact(x):
  return 0.5 * x * (1.0 + lax.erf(x * math.sqrt(0.5)))


def _skew_cols(n):
  return V7X_LANES if (n // V7X_LANES) % V7X_SUBLANES == 0 else 0


def _skewed_dot(a, w_ref, n):
  return jnp.dot(a, w_ref[:, :n], preferred_element_type=jnp.float32)


def _pool_project(d, w_bd_ref, scale):
  halves = []
  for i in range(w_bd_ref.shape[0]):
    lo = i * V7X_MXU_DIM
    halves.append(_bf16_dot(d[:, lo:lo + V7X_MXU_DIM], w_bd_ref[i]))
  return jnp.concatenate(halves, axis=-1) * scale


def _ffn_stages(load_x1, w, emit, final_norm):
  st = {}

  def norm():
    st["x"] = load_x1()
    st["h"] = _rmsnorm(st["x"], w["g_ffn"][...]).astype(jnp.bfloat16)

  def gate_up():
    st["gate"] = jnp.dot(st["h"], w["wg"][...], preferred_element_type=jnp.float32)
    st["up"] = jnp.dot(st["h"], w["wu"][...], preferred_element_type=jnp.float32)

  def activate():
    gate = st["gate"]
    st["act"] = (gate * jax.nn.sigmoid(gate) * st["up"]).astype(jnp.bfloat16)

  def down():
    y = st["x"] + _skewed_dot(st["act"], w["wd"], st["x"].shape[1])
    emit(_rmsnorm(y, w["g_final"][...]) if final_norm else y)

  return [norm, gate_up, activate, down]


def _even_mix_stages(load_x, w, emit, mixers, d_pool, d_conv):
  st = {}

  def norm():
    st["x"] = load_x()
    st["h"] = _rmsnorm(st["x"], w["g_mix"][...]).astype(jnp.bfloat16)

  def in_proj():
    st["z"] = _skewed_dot(st["h"], w["w_in"], d_pool + 3 * d_conv)

  def mix():
    z = st["z"]
    p = z[:, :d_pool]
    xb = z[:, d_pool:d_pool + d_conv]
    bg = z[:, d_pool + d_conv:d_pool + 2 * d_conv]
    cg = z[:, d_pool + 2 * d_conv:]
    pooled, conv_y = mixers(p, cg * xb)
    st["pool_d"] = (pooled - p).astype(jnp.bfloat16)
    st["conv_out"] = (bg * conv_y).astype(jnp.bfloat16)

  def project():
    a_out = _pool_project(st["pool_d"], w["w_pool"], w["pool_scale"][...])
    st["mixed"] = jnp.concatenate([a_out.astype(jnp.bfloat16), st["conv_out"]], axis=-1)

  def out_proj():
    emit(st["x"] + _skewed_dot(st["mixed"], w["w_out"], st["x"].shape[1]))

  return [norm, in_proj, mix, project, out_proj]


def _rows_back(x, k):
  return pltpu.roll(x, k, axis=0)


def _even_prompt_mixers(t, p_hist, c_hist, conv_w, tile):
  assert all(w & (w - 1) == 0 and w <= POOL_PAD for w in POOL_WINDOWS)

  def mixers(p, c):
    pos = t * tile + lax.broadcasted_iota(jnp.int32, (POOL_PAD, POOL_GROUP), 0)
    pooled = []
    for g, win in enumerate(POOL_WINDOWS):
      lo = g * POOL_GROUP
      sums = jnp.concatenate([p_hist[:, lo:lo + POOL_GROUP], p[:, lo:lo + POOL_GROUP]],
                             axis=0)
      span = 1
      while span < win:
        sums = sums + _rows_back(sums, span)
        span *= 2
      sums = sums[POOL_PAD:]
      head = sums[:POOL_PAD] / jnp.minimum(pos + 1, win).astype(jnp.float32)
      pooled.append(jnp.concatenate([head, sums[POOL_PAD:] * (1.0 / win)], axis=0))
    p_hist[...] = p[tile - POOL_PAD:, :]
    c_rows = jnp.concatenate([c_hist[...], c], axis=0)
    y = _rows_back(c_rows, 2)[CONV_PAD:] * conv_w[0:1, :]
    y = y + _rows_back(c_rows, 1)[CONV_PAD:] * conv_w[1:2, :]
    y = y + c * conv_w[2:3, :]
    c_hist[...] = c[tile - CONV_PAD:, :]
    return jnp.concatenate(pooled, axis=-1), y
  return mixers


def _even_sample_mixers(pool_row, conv_hist_ref, conv_w, p_ref, conv_next_ref, start):
  def mixers(p, c):
    p_ref[...] = p
    for k in range(CONV_HIST - 1):
      conv_next_ref[:, k, :] = conv_hist_ref[:, k + 1, :]
    conv_next_ref[:, CONV_HIST - 1, :] = c
    pooled = []
    for g, win in enumerate(POOL_WINDOWS):
      lo = g * POOL_GROUP
      acc = p[:, lo:lo + POOL_GROUP]
      for k in range(1, win):
        acc = acc + pool_row(POOL_HIST - k)[:, lo:lo + POOL_GROUP]
      pooled.append(acc / float(min(start + 1, win)))
    y = conv_hist_ref[:, 0, :] * conv_w[0:1, :]
    y = y + conv_hist_ref[:, 1, :] * conv_w[1:2, :]
    y = y + c * conv_w[2:3, :]
    return jnp.concatenate(pooled, axis=-1), y
  return mixers


def _even_mix_head(t, p_hist, c_hist):
  @pl.when(t == 0)
  def _():
    p_hist[...] = jnp.zeros(p_hist.shape, jnp.float32)
    c_hist[...] = jnp.zeros(c_hist.shape, jnp.float32)


def _even_mix_tail(t, pool_state_ref, conv_state_ref, p_hist, c_hist, *, last_t):
  @pl.when(t == last_t)
  def _():
    pool_state_ref[...] = p_hist[POOL_PAD - POOL_HIST:, :]
    conv_state_ref[...] = c_hist[CONV_PAD - CONV_HIST:, :]


EVEN_SMALL_WEIGHT_NAMES = ("g_mix", "w_pool", "pool_scale", "conv_w", "g_ffn")
EVEN_BIG_WEIGHT_NAMES = ("w_in", "w_out", "wg", "wu", "wd")
ODD_WEIGHT_NAMES = ("g_mix", "w_in", "g_v", "w_s", "b_s", "w_out",
                    "g_ffn", "wg", "wu", "wd", "g_final")
ODD_MIXER_WEIGHT_NAMES = ("w_in", "w_out")
ODD_BIG_WEIGHT_NAMES = ODD_MIXER_WEIGHT_NAMES + ("wg", "wu", "wd")
N_CAST = 5


def _pool_block_diag(groups_ref, tiles_ref):
  per_tile = V7X_MXU_DIM // POOL_GROUP
  tiles_ref[...] = jnp.zeros(tiles_ref.shape, tiles_ref.dtype)
  for g in range(groups_ref.shape[0]):
    i, j = divmod(g, per_tile)
    lo = j * POOL_GROUP
    tiles_ref[i, lo:lo + POOL_GROUP, lo:lo + POOL_GROUP] = (
        groups_ref[g].astype(tiles_ref.dtype))


def _fetch_bf16(srcs, dsts, stage, sems):
  chunks = []
  for src, dst in zip(srcs, dsts):
    n_rows, n_cols = src.shape
    if dst.shape[1] > n_cols:
      dst[:, n_cols:] = jnp.zeros((n_rows, dst.shape[1] - n_cols), dst.dtype)
    for r0 in range(0, n_rows, FETCH_ROWS):
      chunks.append((src, dst, r0, min(FETCH_ROWS, n_rows - r0), n_cols))

  n_slots = stage.shape[0]

  def copy(i):
    src, _, r0, n, n_cols = chunks[i]
    return pltpu.make_async_copy(src.at[pl.ds(r0, n), :],
                                 stage.at[i % n_slots, pl.ds(0, n), pl.ds(0, n_cols)],
                                 sems.at[i % n_slots])

  for i in range(min(n_slots - 1, len(chunks))):
    copy(i).start()
  for i, (_, dst, r0, n, n_cols) in enumerate(chunks):
    if i + n_slots - 1 < len(chunks):
      copy(i + n_slots - 1).start()
    copy(i).wait()
    dst[pl.ds(r0, n), :n_cols] = stage[i % n_slots, :n, :n_cols].astype(dst.dtype)


def _run(stages):
  for stage in stages:
    stage()


def _pipelined_steps(s, n_tiles, mix_stages, ffn_stages):
  @pl.when(s == 0)
  def _():
    _run(mix_stages())

  @pl.when((s > 0) & (s < n_tiles))
  def _():
    mix, ffn = mix_stages(), ffn_stages()
    _run([ffn[0], ffn[1], mix[0], mix[1], mix[2], ffn[2], ffn[3], mix[3], mix[4]])

  @pl.when(s == n_tiles)
  def _():
    _run(ffn_stages())


def _set(ref):
  def emit(value):
    ref[...] = value
  return emit


def _set_rows(ref):
  def emit(value):
    ref[:, 0, :] = value
  return emit


def _even_kernel(*refs, n_tiles, tiles_per_seq, tile, start, d_pool, d_conv, layer,
                 norm_row):
  x_ref, xs_ref, pool_hist_hbm, conv_hist_ref = refs[:4]
  n_small, n_big = len(EVEN_SMALL_WEIGHT_NAMES), len(EVEN_BIG_WEIGHT_NAMES)
  w = dict(zip(EVEN_SMALL_WEIGHT_NAMES, refs[4:4 + n_small]))
  big_hbm = refs[4 + n_small:4 + n_small + n_big]
  cast_in = refs[4 + n_small + n_big:4 + n_small + n_big + N_CAST]
  outs = refs[4 + n_small + n_big + N_CAST:]
  y_ref, pool_state_ref, conv_state_ref, ys_ref, ps_ref, cs_ref = outs[:6]
  cast_out = outs[6:6 + N_CAST]
  pool_next_hbm = outs[6 + N_CAST]
  scratch = outs[7 + N_CAST:]
  p_hist, c_hist, x1_buf = scratch[:3]
  big_vmem = scratch[3:3 + n_big]
  stage, sems, hist_sems, pool_tiles = scratch[3 + n_big:]
  w.update(zip(EVEN_BIG_WEIGHT_NAMES, big_vmem))
  w["g_mix"] = w["g_mix"].at[pl.ds(norm_row, 1)]
  w["g_ffn"] = w["g_ffn"].at[pl.ds(norm_row, 1)]
  w["conv_w"] = w["conv_w"].at[0]
  pool_groups = w["w_pool"]
  w["w_pool"] = pool_tiles

  rows = xs_ref.shape[0]
  per_slot = stage.shape[2] // d_pool
  assert rows <= stage.shape[1] and POOL_HIST <= stage.shape[0] * per_slot

  def pool_row(k):
    return stage.at[k // per_slot, pl.ds(0, rows), pl.ds((k % per_slot) * d_pool, d_pool)]

  def row_in(k):
    return pltpu.make_async_copy(pool_hist_hbm.at[k], pool_row(k), hist_sems.at[0, k])

  def row_out(k):
    src = pool_row(k + 1) if k + 1 < POOL_HIST else ps_ref
    return pltpu.make_async_copy(src, pool_next_hbm.at[k], hist_sems.at[1, k])

  s = pl.program_id(0)
  t = s % tiles_per_seq
  slot = s % 2

  @pl.when(s == 0)
  def _():
    _pool_block_diag(pool_groups, pool_tiles)
    _fetch_bf16([m.at[layer[i]] for i, m in enumerate(big_hbm)], big_vmem, stage, sems)

  @pl.when(s < n_tiles)
  def _():
    _even_mix_head(t, p_hist, c_hist)
    for src, dst in zip(cast_in, cast_out):
      n = src.shape[1]
      dst[:, :n] = src[...].astype(dst.dtype)
      if dst.shape[1] > n:
        dst[:, n:] = jnp.zeros((dst.shape[0], dst.shape[1] - n), dst.dtype)

  def mix_stages():
    return _even_mix_stages(
        lambda: x_ref[...], w, _set(x1_buf.at[slot]),
        _even_prompt_mixers(t, p_hist, c_hist, w["conv_w"], tile), d_pool, d_conv)

  def ffn_stages():
    return _ffn_stages(lambda: x1_buf[1 - slot], w, _set(y_ref), False)

  @pl.when(s == n_tiles)
  def _():
    for k in range(POOL_HIST):
      row_in(k).start()

  _pipelined_steps(s, n_tiles, mix_stages, ffn_stages)

  @pl.when(s < n_tiles)
  def _():
    _even_mix_tail(t, pool_state_ref, conv_state_ref, p_hist, c_hist,
                   last_t=tiles_per_seq - 1)

  @pl.when(s == n_tiles + 1)
  def _():
    for k in range(POOL_HIST):
      row_in(k).wait()
    for k in range(POOL_HIST - 1):
      row_out(k).start()
    st = {}
    mixers = _even_sample_mixers(pool_row, conv_hist_ref, w["conv_w"],
                                 ps_ref, cs_ref, start)
    _run(_even_mix_stages(lambda: xs_ref[:, 0, :], w, lambda v: st.update(x1=v),
                          mixers, d_pool, d_conv))
    row_out(POOL_HIST - 1).start()
    _run(_ffn_stages(lambda: st["x1"], w, _set(ys_ref), False))
    for k in range(POOL_HIST):
      row_out(k).wait()


def _odd_mix_stages(load_x, w, emit, gating, d_gate):
  st = {}

  def norm():
    st["x"] = load_x()
    st["h"] = _rmsnorm(st["x"], w["g_mix"][...]).astype(jnp.bfloat16)

  def in_proj():
    st["z"] = _skewed_dot(st["h"], w["w_in"], 2 * d_gate)

  def activate():
    z = _gelu_exact(st["z"])
    st["u"] = z[:, :d_gate]
    st["v"] = _rmsnorm(z[:, d_gate:], w["g_v"][...])

  def gate():
    st["gated"] = (st["u"] * gating(st["v"])).astype(jnp.bfloat16)

  def out_proj():
    emit(st["x"] + _skewed_dot(st["gated"], w["w_out"], st["x"].shape[1]))

  return [norm, in_proj, activate, gate, out_proj]


def _prepare_gating(w, ws_causal, bias_t):
  row = lax.broadcasted_iota(jnp.int32, (CHUNK, CHUNK), 0)
  col = lax.broadcasted_iota(jnp.int32, (CHUNK, CHUNK), 1)
  for hd in range(N_SG_HEADS):
    ws_causal[hd] = jnp.where(row >= col, w["w_s"][hd], 0.0).astype(ws_causal.dtype)
  bias_t[...] = w["b_s"][...].T


def _odd_prompt_gating(ws_causal, bias_t, tile, d_gate):
  def gating(v):
    vb = v.astype(jnp.bfloat16)
    n_chunks = tile // CHUNK
    head = d_gate // N_SG_HEADS
    bias = bias_t[...]
    per_head = []
    for hd in range(N_SG_HEADS):
      w_h = ws_causal[hd]
      rhs = jnp.concatenate(
          [vb[c * CHUNK:(c + 1) * CHUNK, hd * head:(hd + 1) * head]
           for c in range(n_chunks)], axis=1)
      o = jnp.dot(w_h, rhs, preferred_element_type=jnp.float32)
      per_head.append(o + bias[:, hd:hd + 1])
    return jnp.concatenate(
        [jnp.concatenate([o[:, c * head:(c + 1) * head] for o in per_head], axis=1)
         for c in range(n_chunks)], axis=0)
  return gating


def _odd_sample_gating(w, v_ref, d_gate):
  def gating(v):
    v_ref[:, 0, :] = v
    head = d_gate // N_SG_HEADS
    rows = v.shape[0]
    vb = v.astype(jnp.bfloat16).astype(jnp.float32)
    per_head = []
    for hd in range(N_SG_HEADS):
      w00 = w["w_s"][hd, 0:1, 0:1].astype(jnp.bfloat16).astype(jnp.float32)
      b0 = w["b_s"][hd:hd + 1, 0:1]
      per_head.append(vb[:, hd * head:(hd + 1) * head] * jnp.broadcast_to(w00, (rows, head))
                      + jnp.broadcast_to(b0, (rows, head)))
    return jnp.concatenate(per_head, axis=1)
  return gating


def _odd_kernel(*refs, n_tiles, tile, d_gate, norm_row):
  x_ref, xs_ref = refs[:2]
  n_w = len(ODD_WEIGHT_NAMES)
  w = dict(zip(ODD_WEIGHT_NAMES, refs[2:2 + n_w]))
  y_ref, ys_ref, vs_ref, x1_buf, ws_causal, bias_t = refs[2 + n_w:2 + n_w + 6]
  big_vmem = refs[2 + n_w + 6:-1]
  sems = refs[-1]
  copies = {name: pltpu.make_async_copy(w[name], buf, sems.at[i])
            for i, (name, buf) in enumerate(zip(ODD_BIG_WEIGHT_NAMES, big_vmem))}
  w.update(zip(ODD_BIG_WEIGHT_NAMES, big_vmem))
  w["g_mix"] = w["g_mix"].at[pl.ds(norm_row, 1)]
  w["g_ffn"] = w["g_ffn"].at[pl.ds(norm_row, 1)]

  s = pl.program_id(0)
  slot = s % 2

  @pl.when(s == 0)
  def _():
    for name in ODD_BIG_WEIGHT_NAMES:
      copies[name].start()
    _prepare_gating(w, ws_causal, bias_t)
    for name in ODD_MIXER_WEIGHT_NAMES:
      copies[name].wait()

  def mix_stages():
    return _odd_mix_stages(lambda: x_ref[...], w, _set(x1_buf.at[slot]),
                           _odd_prompt_gating(ws_causal, bias_t, tile, d_gate), d_gate)

  def ffn_stages():
    return _ffn_stages(lambda: x1_buf[1 - slot], w, _set(y_ref), True)

  _pipelined_steps(s, n_tiles, mix_stages, ffn_stages)

  @pl.when(s == 0)
  def _():
    for name in ODD_BIG_WEIGHT_NAMES:
      if name not in ODD_MIXER_WEIGHT_NAMES:
        copies[name].wait()

  @pl.when(s == n_tiles + 1)
  def _():
    st = {}
    _run(_odd_mix_stages(lambda: xs_ref[...], w, lambda v: st.update(x1=v),
                         _odd_sample_gating(w, vs_ref, d_gate), d_gate))
    _run(_ffn_stages(lambda: st["x1"], w, _set_rows(ys_ref), True))


def _resident(arr):
  zeros = (0,) * arr.ndim
  return pl.BlockSpec(arr.shape, lambda s: zeros, pipeline_mode=pl.Buffered(1))


def _whole_out(shape):
  zeros = (0,) * len(shape)
  return pl.BlockSpec(shape, lambda s: zeros)


def _params():
  return pltpu.CompilerParams(dimension_semantics=("arbitrary",),
                              vmem_limit_bytes=V7X_VMEM_LIMIT_BYTES)


def _prompt_specs(x, tile):
  batch, seq, d_model = x.shape
  tiles_per_seq = seq // tile
  n_tiles = batch * tiles_per_seq
  mix_tile = lambda s: jnp.clip(s, 0, n_tiles - 1)
  ffn_tile = lambda s: jnp.clip(s - 1, 0, n_tiles - 1)
  block = lambda which: pl.BlockSpec(
      (None, tile, d_model),
      lambda s: (which(s) // tiles_per_seq, which(s) % tiles_per_seq, 0))
  per_seq = lambda rows, ch: pl.BlockSpec(
      (None, rows, ch), lambda s: (mix_tile(s) // tiles_per_seq, 0, 0))
  return n_tiles, tiles_per_seq, mix_tile, block(mix_tile), block(ffn_tile), per_seq


def _cast_slab_spec(rows, cols, n_steps, step, layer):
  slab = rows // n_steps
  steps_per_slab = 1
  while slab % V7X_BF16_SUBLANES:
    slab *= 2
    steps_per_slab *= 2
  src = pl.BlockSpec((None, slab, cols), lambda s: (layer, step(s) // steps_per_slab, 0))
  dst = pl.BlockSpec((slab, cols + _skew_cols(cols)),
                     lambda s: (step(s) // steps_per_slab, 0))
  return src, dst


def _even_layer(x, xs, pool_hist, conv_hist, small_weights, norm_row, mats, mat_layer,
                next_mats, next_layer, tile, start):
  n_tiles, tiles_per_seq, step, tok_in, tok_out, per_seq = _prompt_specs(x, tile)
  batch = x.shape[0]
  rows, _, d_model = xs.shape
  d_pool = pool_hist.shape[-1]
  d_conv = conv_hist.shape[-1]
  cast_specs = [_cast_slab_spec(m.shape[1], m.shape[2], n_tiles, step, next_layer[i])
                for i, m in enumerate(next_mats)]
  f32 = x.dtype
  skewed = lambda m: (m.shape[1], m.shape[2] + _skew_cols(m.shape[2]))
  widest = max(m.shape[2] for m in mats)
  return pl.pallas_call(
      functools.partial(_even_kernel, n_tiles=n_tiles,
                        tiles_per_seq=tiles_per_seq, tile=tile, start=start,
                        d_pool=d_pool, d_conv=d_conv, layer=tuple(mat_layer),
                        norm_row=norm_row),
      grid=(n_tiles + 2,),
      in_specs=([tok_in, _resident(xs), pl.BlockSpec(memory_space=pl.ANY),
                 _resident(conv_hist)]
                + [_resident(w) for w in small_weights]
                + [pl.BlockSpec(memory_space=pl.ANY) for _ in mats]
                + [c[0] for c in cast_specs]),
      out_specs=([tok_out, per_seq(POOL_HIST, d_pool), per_seq(CONV_HIST, d_conv),
                  _whole_out((rows, d_model)), _whole_out((rows, d_pool)),
                  _whole_out(conv_hist.shape)] + [c[1] for c in cast_specs]
                 + [pl.BlockSpec(memory_space=pl.ANY)]),
      out_shape=([jax.ShapeDtypeStruct(x.shape, f32),
                  jax.ShapeDtypeStruct((batch, POOL_HIST, d_pool), f32),
                  jax.ShapeDtypeStruct((batch, CONV_HIST, d_conv), f32),
                  jax.ShapeDtypeStruct((rows, d_model), f32),
                  jax.ShapeDtypeStruct((rows, d_pool), f32),
                  jax.ShapeDtypeStruct(conv_hist.shape, f32)]
                 + [jax.ShapeDtypeStruct(skewed(m), jnp.bfloat16) for m in next_mats]
                 + [jax.ShapeDtypeStruct(pool_hist.shape, f32)]),
      scratch_shapes=([pltpu.VMEM((POOL_PAD, d_pool), jnp.float32),
                       pltpu.VMEM((CONV_PAD, d_conv), jnp.float32),
                       pltpu.VMEM((2, tile, d_model), jnp.float32)]
                      + [pltpu.VMEM(skewed(m), jnp.bfloat16) for m in mats]
                      + [pltpu.VMEM((FETCH_SLOTS, FETCH_ROWS, widest), jnp.float32),
                         pltpu.SemaphoreType.DMA((FETCH_SLOTS,)),
                         pltpu.SemaphoreType.DMA((2, POOL_HIST)),
                         pltpu.VMEM((len(POOL_WINDOWS) * POOL_GROUP // V7X_MXU_DIM,
                                     V7X_MXU_DIM, V7X_MXU_DIM), jnp.bfloat16)]),
      compiler_params=_params(),
      name="even_layer",
  )(x, xs, pool_hist, conv_hist, *small_weights, *mats, *next_mats)


def _odd_layer(x, xs, weights, norm_row, tile):
  n_tiles, _, _, tok_in, tok_out, _ = _prompt_specs(x, tile)
  rows, d_model = xs.shape
  d_gate = weights[ODD_WEIGHT_NAMES.index("g_v")].shape[-1]
  big = {name: w for name, w in zip(ODD_WEIGHT_NAMES, weights)
         if name in ODD_BIG_WEIGHT_NAMES}
  return pl.pallas_call(
      functools.partial(_odd_kernel, n_tiles=n_tiles, tile=tile, d_gate=d_gate,
                        norm_row=norm_row),
      grid=(n_tiles + 2,),
      in_specs=([tok_in, _resident(xs)]
                + [pl.BlockSpec(memory_space=pl.ANY) if name in ODD_BIG_WEIGHT_NAMES
                   else _resident(w) for name, w in zip(ODD_WEIGHT_NAMES, weights)]),
      out_specs=[tok_out, _whole_out((rows, 1, d_model)), _whole_out((rows, 1, d_gate))],
      out_shape=[jax.ShapeDtypeStruct(x.shape, x.dtype),
                 jax.ShapeDtypeStruct((rows, 1, d_model), x.dtype),
                 jax.ShapeDtypeStruct((rows, 1, d_gate), x.dtype)],
      scratch_shapes=([pltpu.VMEM((2, tile, d_model), jnp.float32),
                       pltpu.VMEM((N_SG_HEADS, CHUNK, CHUNK), jnp.bfloat16),
                       pltpu.VMEM((CHUNK, N_SG_HEADS), jnp.float32)]
                      + [pltpu.VMEM(big[name].shape, big[name].dtype)
                         for name in ODD_BIG_WEIGHT_NAMES]
                      + [pltpu.SemaphoreType.DMA((len(ODD_BIG_WEIGHT_NAMES),))]),
      compiler_params=_params(),
      name="odd_layer",
  )(x, xs, *weights)


def kernel(x_prompt, x_sample, state_pool, state_conv, norm_mix, norm_ffn, norm_final, w_in_even, w_pool, pool_scale, conv_w, w_out_even, w_in_odd, norm_sg, w_s, b_s, w_out_odd, ffn_w_gate, ffn_w_up, ffn_w_down):
  depth = norm_mix.shape[0]
  assert depth == 2 and w_in_even.shape[0] == 1 and w_in_odd.shape[0] == 1
  assert x_sample.shape[1] == 1
  assert x_prompt.shape[1] % PROMPT_TILE == 0 and PROMPT_TILE % CHUNK == 0

  even_small = (norm_mix, w_pool[0], pool_scale, conv_w, norm_ffn)
  even_mats = (w_in_even, w_out_even, ffn_w_gate, ffn_w_up, ffn_w_down)
  even_mat_layer = (0, 0, 0, 0, 0)
  odd_mats = (w_in_odd, w_out_odd, ffn_w_gate, ffn_w_up, ffn_w_down)
  odd_mat_layer = (0, 0, 1, 1, 1)

  pool_hist = jnp.swapaxes(state_pool[0], 0, 1)
  conv_hist = state_conv[0]

  (x1_p, pool_p, conv_p, x1_s, _, conv_s, w_in_o, w_out_o, wg_o, wu_o, wd_o,
   pool_s) = _even_layer(x_prompt, x_sample, pool_hist, conv_hist, even_small, 0,
                         even_mats, even_mat_layer, odd_mats, odd_mat_layer,
                         PROMPT_TILE, PAST_LEN)

  odd_w = (norm_mix, w_in_o, norm_sg, w_s[0], b_s[0], w_out_o,
           norm_ffn, wg_o, wu_o, wd_o, norm_final.reshape(1, -1))
  y_p, y_s, v_s = _odd_layer(x1_p, x1_s, odd_w, 1, PROMPT_TILE)

  pool_s = jnp.swapaxes(pool_s, 0, 1)[None]

  return (y_p, y_s, pool_p[None], pool_s, conv_p[None], conv_s[None], v_s[None])
```
